```python
import math
import jax, jax.numpy as jnp
from jax import lax
import numpy as np

D_MODEL = 1024
BATCH = 16
SEQ = 2048
DEPTH = 2

GDN_HEADS = 4
GDN_DK = 128
GDN_DV = 128
GDN_CONV = 4
GDN_CHUNK = 64
MLSTM_HEADS = 4
MLSTM_DK = 128
MLSTM_DV = 128
MLSTM_CHUNK = 64
NSA_HEADS = 8
NSA_GROUPS = 2
NSA_HPG = NSA_HEADS // NSA_GROUPS
NSA_DH = 64
CMP_BLOCK = 32
CMP_STRIDE = 16
SEL_BLOCK = 64
SEL_TOPN = 4
WINDOW = 512
Q_BLOCK = 128
REL_BUCKETS = 32
REL_MAX_DIST = 128
N_BRANCH = 3
BRANCH_W = 512
D_FF = -(-8 * D_MODEL // (3 * 256)) * 256
PLE_DIM = 256
EPS = 1e-6
NEG = -1e30
FORCE_SCORE = 1e4

IN_SIZES = [
    GDN_HEADS * GDN_DK, GDN_HEADS * GDN_DK, GDN_HEADS * GDN_DV, GDN_HEADS * GDN_DV, GDN_HEADS, GDN_HEADS,
    MLSTM_HEADS * MLSTM_DK, MLSTM_HEADS * MLSTM_DK, MLSTM_HEADS * MLSTM_DV, MLSTM_HEADS * MLSTM_DV, MLSTM_HEADS, MLSTM_HEADS,
    NSA_HEADS * NSA_DH] + [NSA_GROUPS * NSA_DH] * 6 + [NSA_HEADS * 3, N_BRANCH * D_MODEL]
D_IN = sum(IN_SIZES)

kernel_name = "hybrid_gdn_mlstm_nsa_block"


def rms_f32(x, g):
    xf = x.astype(jnp.float32)
    return xf * lax.rsqrt(jnp.mean(xf * xf, -1, keepdims=True) + EPS) * g.astype(jnp.float32)


def rmsnorm(x, g):
    return rms_f32(x, g).astype(x.dtype)


def l2norm(x):
    return x * lax.rsqrt(jnp.sum(x * x, -1, keepdims=True) + 1e-6)


def split_cols(y, sizes):
    return jnp.split(y, np.cumsum(sizes)[:-1].tolist(), axis=-1)


def causal_conv(x, w):
    K, C = w.shape
    return lax.conv_general_dilated(x, w[:, None, :].astype(x.dtype), window_strides=(1,),
                                    padding=[(K - 1, 0)], dimension_numbers=('NWC', 'WIO', 'NWC'),
                                    feature_group_count=C)


def to_chunks(t, L):
    Bsz, S, H, d = t.shape
    return t.reshape(Bsz, S // L, L, H, d).transpose(1, 0, 3, 2, 4)


def from_chunks(t):
    N, Bsz, H, L, d = t.shape
    return t.transpose(1, 0, 3, 2, 4).reshape(Bsz, N * L, H, d)


def rel_bucket(dist):
    max_exact = REL_BUCKETS // 2
    d = jnp.maximum(dist, 0)
    df = jnp.maximum(d, 1).astype(jnp.float32)
    large = max_exact + (jnp.log(df / max_exact) / math.log(REL_MAX_DIST / max_exact)
                         * (REL_BUCKETS - max_exact)).astype(jnp.int32)
    large = jnp.minimum(large, REL_BUCKETS - 1)
    return jnp.where(d < max_exact, d, large)


def masked_softmax(s, valid):
    s = jnp.where(valid, s.astype(jnp.float32), NEG)
    e = jnp.exp(s - jnp.max(s, -1, keepdims=True)) * valid
    return e / jnp.maximum(jnp.sum(e, -1, keepdims=True), 1e-30)


def gated_deltanet(q, k, v, z, a, b, conv_w, a_log, dt_bias, norm_w):
    dt = q.dtype
    Bsz, S, _ = q.shape
    H, dk, dv, L = GDN_HEADS, GDN_DK, GDN_DV, GDN_CHUNK
    f32 = jnp.float32
    qkv = jax.nn.silu(causal_conv(jnp.concatenate([q, k, v], -1), conv_w)).astype(f32)
    q, k, v = jnp.split(qkv, [H * dk, 2 * H * dk], axis=-1)
    q = l2norm(q.reshape(Bsz, S, H, dk)) * (dk ** -0.5)
    k = l2norm(k.reshape(Bsz, S, H, dk))
    v = v.reshape(Bsz, S, H, dv)
    beta = jax.nn.sigmoid(b.astype(f32))
    g = -jnp.exp(a_log.astype(f32)) * jax.nn.softplus(a.astype(f32) + dt_bias.astype(f32))
    qc, kc, vc = to_chunks(q, L), to_chunks(k, L), to_chunks(v, L)
    bc = to_chunks(beta[..., None], L)
    gc = jnp.cumsum(to_chunks(g[..., None], L)[..., 0], axis=-1)
    tril = jnp.tril(jnp.ones((L, L), bool))
    strict = jnp.tril(jnp.ones((L, L), bool), -1)
    diff = gc[..., :, None] - gc[..., None, :]
    decay = jnp.where(tril, jnp.exp(jnp.where(tril, diff, 0.0)), 0.0)
    kb = kc * bc
    X = jnp.where(strict, jnp.einsum('nbhid,nbhjd->nbhij', kb, kc) * decay, 0.0)
    eye = jnp.eye(L, dtype=f32)
    T = lax.linalg.triangular_solve(eye + X, jnp.broadcast_to(eye, X.shape), left_side=True,
                                    lower=True, unit_diagonal=True)
    u = T @ (vc * bc)
    w = T @ (kb * jnp.exp(gc)[..., None])
    attn = jnp.einsum('nbhid,nbhjd->nbhij', qc, kc) * decay

    def step(S_, xs):
        q_, k_, u_, w_, attn_, g_ = xs
        v_new = u_ - w_ @ S_
        o = (q_ * jnp.exp(g_)[..., None]) @ S_ + attn_ @ v_new
        gl = g_[..., -1:]
        S_ = S_ * jnp.exp(gl)[..., None] + jnp.einsum('bhld,bhle->bhde', k_ * jnp.exp(gl - g_)[..., None], v_new)
        return S_, o

    S0 = jnp.zeros((Bsz, H, dk, dv), f32)
    _, o = lax.scan(step, S0, (qc, kc, u, w, attn, gc))
    o = from_chunks(o)
    o = rms_f32(o, norm_w) * jax.nn.silu(z.reshape(Bsz, S, H, dv).astype(f32))
    return o.reshape(Bsz, S, H * dv).astype(dt)


def mlstm(q, k, v, o_pre, i_pre, f_pre, b_i, b_f, norm_w):
    dt = q.dtype
    Bsz, S, _ = q.shape
    H, dk, dv, L = MLSTM_HEADS, MLSTM_DK, MLSTM_DV, MLSTM_CHUNK
    f32 = jnp.float32
    q = q.reshape(Bsz, S, H, dk).astype(f32)
    k = k.reshape(Bsz, S, H, dk).astype(f32) * (dk ** -0.5)
    v = v.reshape(Bsz, S, H, dv).astype(f32)
    it = i_pre.astype(f32) + b_i.astype(f32)
    lf = jax.nn.log_sigmoid(f_pre.astype(f32) + b_f.astype(f32))
    qc, kc, vc = to_chunks(q, L), to_chunks(k, L), to_chunks(v, L)
    itc = to_chunks(it[..., None], L)[..., 0]
    bcum = jnp.cumsum(to_chunks(lf[..., None], L)[..., 0], axis=-1)
    tril = jnp.tril(jnp.ones((L, L), bool))
    Dm = jnp.where(tril, bcum[..., :, None] - bcum[..., None, :] + itc[..., None, :], NEG)
    dmax = jnp.max(Dm, -1)
    qk = jnp.einsum('nbhid,nbhjd->nbhij', qc, kc)

    def step(carry, xs):
        Cb, nb, m = carry
        q_, k_, v_, it_, b_, D_, dmax_, qk_ = xs
        a = b_ + m[..., None]
        mt = jnp.maximum(a, dmax_)
        Sm = jnp.exp(D_ - mt[..., None]) * qk_
        si = jnp.exp(a - mt)
        num = si[..., None] * (q_ @ Cb) + Sm @ v_
        den = si * jnp.einsum('bhld,bhd->bhl', q_, nb) + jnp.sum(Sm, -1)
        h = num / jnp.maximum(jnp.abs(den), jnp.exp(-mt))[..., None]
        bl = b_[..., -1]
        ds = bl[..., None] - b_ + it_
        m_new = jnp.maximum(bl + m, jnp.max(ds, -1))
        wk = k_ * jnp.exp(ds - m_new[..., None])[..., None]
        sc = jnp.exp(bl + m - m_new)
        Cb = sc[..., None, None] * Cb + jnp.einsum('bhld,bhle->bhde', wk, v_)
        nb = sc[..., None] * nb + jnp.sum(wk, -2)
        return (Cb, nb, m_new), h

    init = (jnp.zeros((Bsz, H, dk, dv), f32), jnp.zeros((Bsz, H, dk), f32), jnp.zeros((Bsz, H), f32))
    _, h = lax.scan(step, init, (qc, kc, vc, itc, bcum, Dm, dmax, qk))
    h = rms_f32(from_chunks(h), norm_w) * jax.nn.sigmoid(o_pre.reshape(Bsz, S, H, dv).astype(f32))
    return h.reshape(Bsz, S, H * dv).astype(dt)


def nsa(q, k_cmp, v_cmp, k_slc, v_slc, k_win, v_win, gate_pre, q_norm, k_norm, cmp_pe, w_cmp, rel_bias):
    dt = q.dtype
    Bsz, S, _ = q.shape
    H, G, hpg, dh = NSA_HEADS, NSA_GROUPS, NSA_HPG, NSA_DH
    f32 = jnp.float32
    q = rms_f32(q.reshape(Bsz, S, G, hpg, dh), q_norm) * (dh ** -0.5)

    def kv(t):
        return t.reshape(Bsz, S, G, dh).astype(f32)

    n_cmp = (S - CMP_BLOCK) // CMP_STRIDE + 1
    cmp_start = np.arange(n_cmp) * CMP_STRIDE
    cmp_end = cmp_start + CMP_BLOCK - 1
    blk_idx = cmp_start[:, None] + np.arange(CMP_BLOCK)[None]

    def compress(t, pe, w):
        blocks = t[:, blk_idx] + pe.astype(f32)[:, None, :]
        return jnp.einsum('bnlgd,lde->bnge', blocks, w.astype(f32))

    ck = rms_f32(compress(kv(k_cmp), cmp_pe[0], w_cmp[0]), k_norm[0])
    cv = compress(kv(v_cmp), cmp_pe[1], w_cmp[1])
    t_pos = np.arange(S)
    dist_c = (t_pos[:, None] - cmp_end[None]).astype(np.int32)
    valid_c = dist_c >= 0
    bias_c = rel_bias[rel_bucket(jnp.asarray(dist_c))].reshape(S, n_cmp, G, hpg).transpose(2, 3, 0, 1)
    s_c = jnp.einsum('bsgkd,bngd->bgksn', q, ck) + bias_c
    p_c = masked_softmax(s_c, valid_c)
    o_cmp = jnp.einsum('bgksn,bngd->bsgkd', p_c, cv)

    n_sel = S // SEL_BLOCK
    topn = min(SEL_TOPN, n_sel)
    sel_start = np.arange(n_sel) * SEL_BLOCK
    sel_end = sel_start + SEL_BLOCK - 1
    overlap = ((cmp_start[:, None] <= sel_end[None]) & (cmp_end[:, None] >= sel_start[None])).astype(np.float32)
    imp = jnp.einsum('bgksn,nj->bgsj', p_c, jnp.asarray(overlap))
    cur = t_pos // SEL_BLOCK
    jj = np.arange(n_sel)
    forced = (jj[None] == 0) | (jj[None] == cur[:, None])
    causal_blk = jj[None] <= cur[:, None]
    imp = jnp.where(forced, FORCE_SCORE, jnp.where(causal_blk, imp, -1.0))
    _, sel_idx = lax.top_k(imp, topn)

    ks_b = rms_f32(kv(k_slc), k_norm[1]).reshape(Bsz, n_sel, SEL_BLOCK, G, dh).transpose(0, 3, 1, 2, 4)
    vs_b = kv(v_slc).reshape(Bsz, n_sel, SEL_BLOCK, G, dh).transpose(0, 3, 1, 2, 4)
    kw_pad = jnp.pad(rms_f32(kv(k_win), k_norm[2]), ((0, 0), (WINDOW, 0), (0, 0), (0, 0)))
    vw_pad = jnp.pad(kv(v_win), ((0, 0), (WINDOW, 0), (0, 0), (0, 0)))
    nq = S // Q_BLOCK
    q_blocks = q.reshape(Bsz, nq, Q_BLOCK, G, hpg, dh).transpose(1, 0, 2, 3, 4, 5)
    idx_blocks = sel_idx.reshape(Bsz, G, nq, Q_BLOCK, topn).transpose(2, 0, 1, 3, 4)
    rb = rel_bias.reshape(REL_BUCKETS, G, hpg)
    b_ar = jnp.arange(Bsz)[:, None, None, None]
    g_ar = jnp.arange(G)[None, :, None, None]
    Lw = WINDOW + Q_BLOCK

    def block_fn(xs):
        qb, ib, c = xs
        t = c * Q_BLOCK + jnp.arange(Q_BLOCK)
        Ksel = ks_b[b_ar, g_ar, ib].reshape(Bsz, G, Q_BLOCK, topn * SEL_BLOCK, dh)
        Vsel = vs_b[b_ar, g_ar, ib].reshape(Bsz, G, Q_BLOCK, topn * SEL_BLOCK, dh)
        kpos = (ib[..., None] * SEL_BLOCK + jnp.arange(SEL_BLOCK)).reshape(Bsz, G, Q_BLOCK, topn * SEL_BLOCK)
        dist = t[None, None, :, None] - kpos
        bias = rb[rel_bucket(dist), g_ar].transpose(0, 1, 4, 2, 3)
        s = jnp.einsum('bqgkd,bgqnd->bgkqn', qb, Ksel) + bias
        p = masked_softmax(s, (dist >= 0)[:, :, None])
        o_s = jnp.einsum('bgkqn,bgqnd->bqgkd', p, Vsel)
        start = c * Q_BLOCK
        Kw = lax.dynamic_slice_in_dim(kw_pad, start, Lw, axis=1)
        Vw = lax.dynamic_slice_in_dim(vw_pad, start, Lw, axis=1)
        spos = start - WINDOW + jnp.arange(Lw)
        dw = t[:, None] - spos[None]
        valid_w = (dw >= 0) & (dw < WINDOW) & (spos[None] >= 0)
        bias_w = rel_bias[rel_bucket(dw)].reshape(Q_BLOCK, Lw, G, hpg).transpose(2, 3, 0, 1)
        s = jnp.einsum('bqgkd,bngd->bgkqn', qb, Kw) + bias_w
        p = masked_softmax(s, valid_w)
        o_w = jnp.einsum('bgkqn,bngd->bqgkd', p, Vw)
        return o_s, o_w

    o_s, o_w = lax.map(block_fn, (q_blocks, idx_blocks, jnp.arange(nq)))
    o_s = o_s.transpose(1, 0, 2, 3, 4, 5).reshape(Bsz, S, G, hpg, dh)
    o_w = o_w.transpose(1, 0, 2, 3, 4, 5).reshape(Bsz, S, G, hpg, dh)
    gates = jax.nn.sigmoid(gate_pre.astype(f32)).reshape(Bsz, S, G, hpg, 3)
    out = gates[..., 0:1] * o_cmp + gates[..., 1:2] * o_s + gates[..., 2:3] * o_w
    return out.reshape(Bsz, S, H * dh).astype(dt)


def setup_inputs(seed: int = 0) -> dict:
    key = jax.random.key(seed)
    ks = jax.random.split(key, 26)
    f32 = jnp.float32

    def nrm(k, shape, scale):
        return jax.random.normal(k, shape, f32) * scale

    def gain(k, shape):
        return 1.0 + 0.02 * jax.random.normal(k, shape, f32)

    dt_init = jnp.exp(jax.random.uniform(ks[5], (DEPTH, GDN_HEADS), f32, math.log(1e-3), math.log(1e-1)))
    return {
        "x": nrm(ks[0], (BATCH, SEQ, D_MODEL), 1.0),
        "p": nrm(ks[1], (DEPTH, BATCH, SEQ, PLE_DIM), 1.0),
        "rel_bias": nrm(ks[2], (REL_BUCKETS, NSA_HEADS), 0.3),
        "norm_mix": gain(ks[3], (DEPTH, D_MODEL)),
        "w_in": nrm(ks[4], (DEPTH, D_MODEL, D_IN), D_MODEL ** -0.5),
        "conv_w": nrm(ks[6], (DEPTH, GDN_CONV, 2 * GDN_HEADS * GDN_DK + GDN_HEADS * GDN_DV), GDN_CONV ** -0.5),
        "gdn_a_log": jnp.log(jax.random.uniform(ks[7], (DEPTH, GDN_HEADS), f32, 1.0, 16.0)),
        "gdn_dt_bias": dt_init + jnp.log(-jnp.expm1(-dt_init)),
        "gdn_norm": gain(ks[8], (DEPTH, GDN_DV)),
        "mlstm_b_i": nrm(ks[9], (DEPTH, MLSTM_HEADS), 0.1),
        "mlstm_b_f": jnp.linspace(3.0, 6.0, MLSTM_HEADS, dtype=f32)[None] + nrm(ks[10], (DEPTH, MLSTM_HEADS), 0.1),
        "mlstm_norm": gain(ks[11], (DEPTH, MLSTM_DV)),
        "nsa_q_norm": gain(ks[12], (DEPTH, NSA_DH)),
        "nsa_k_norm": gain(ks[13], (DEPTH, 3, NSA_DH)),
        "nsa_cmp_pe": nrm(ks[14], (DEPTH, 2, CMP_BLOCK, NSA_DH), 0.02),
        "nsa_w_cmp": nrm(ks[15], (DEPTH, 2, CMP_BLOCK, NSA_DH, NSA_DH), (CMP_BLOCK * NSA_DH) ** -0.5),
        "w_branch": nrm(ks[16], (DEPTH, N_BRANCH, BRANCH_W, D_MODEL), BRANCH_W ** -0.5),
        "w_out": nrm(ks[17], (DEPTH, D_MODEL, D_MODEL), D_MODEL ** -0.5),
        "norm_ffn": gain(ks[18], (DEPTH, D_MODEL)),
        "w_ffn_in": nrm(ks[19], (DEPTH, D_MODEL, 2 * D_FF), D_MODEL ** -0.5),
        "w_ffn_out": nrm(ks[20], (DEPTH, D_FF, D_MODEL), D_FF ** -0.5),
        "norm_ple": gain(ks[21], (DEPTH, D_MODEL)),
        "w_ple_gate": nrm(ks[22], (DEPTH, D_MODEL, D_MODEL), D_MODEL ** -0.5),
        "w_ple_proj": nrm(ks[23], (DEPTH, PLE_DIM, D_MODEL), PLE_DIM ** -0.5),
    }


def reference(x, p, rel_bias, norm_mix, w_in, conv_w, gdn_a_log, gdn_dt_bias, gdn_norm, mlstm_b_i, mlstm_b_f,
              mlstm_norm, nsa_q_norm, nsa_k_norm, nsa_cmp_pe, nsa_w_cmp, w_branch, w_out, norm_ffn, w_ffn_in,
              w_ffn_out, norm_ple, w_ple_gate, w_ple_proj):
    Bsz, S, D = x.shape
    h = x
    for l in range(DEPTH):
        u = rmsnorm(h, norm_mix[l])
        (aq, ak, av, az, aa, ab,
         bq, bk, bv, bo, bi, bf,
         cq, ckc, cvc, cks, cvs, ckw, cvw, cg, mg) = split_cols(u @ w_in[l], IN_SIZES)
        y_a = gated_deltanet(aq, ak, av, az, aa, ab, conv_w[l], gdn_a_log[l], gdn_dt_bias[l], gdn_norm[l])
        y_b = mlstm(bq, bk, bv, bo, bi, bf, mlstm_b_i[l], mlstm_b_f[l], mlstm_norm[l])
        y_c = nsa(cq, ckc, cvc, cks, cvs, ckw, cvw, cg, nsa_q_norm[l], nsa_k_norm[l], nsa_cmp_pe[l],
                  nsa_w_cmp[l], rel_bias)
        ys = jnp.stack([y_a, y_b, y_c], axis=2)
        br = jnp.einsum('bsnc,ncd->bsnd', ys, w_branch[l])
        gates = jax.nn.sigmoid(mg.astype(jnp.float32)).reshape(Bsz, S, N_BRANCH, D).astype(br.dtype)
        merged = jnp.sum(gates * br, axis=2)
        h = h + merged @ w_out[l]
        u = rmsnorm(h, norm_ffn[l])
        gt, up = jnp.split(u @ w_ffn_in[l], 2, axis=-1)
        h = h + (jax.nn.silu(gt) * up) @ w_ffn_out[l]
        u = rmsnorm(h, norm_ple[l])
        h = h + jax.nn.sigmoid(u @ w_ple_gate[l]) * (p[l] @ w_ple_proj[l])
    return h
```

```python
import functools
import math

import numpy as np
import jax
import jax.numpy as jnp
from jax import lax
from jax.experimental import pallas as pl
from jax.experimental.pallas import tpu as pltpu

D_MODEL = 1024
SEQ = 2048
N_HEADS_REC = 4
D_HEAD_REC = 128
CHUNK = 64
GDN_CONV = 4
NSA_HEADS = 8
NSA_GROUPS = 2
NSA_HPG = NSA_HEADS // NSA_GROUPS
NSA_DH = 64
CMP_BLOCK = 32
CMP_STRIDE = 16
SEL_BLOCK = 64
SEL_TOPN = 4
WINDOW = 512
Q_BLOCK = 128
N_QBLK = SEQ // Q_BLOCK
N_SEL = SEQ // SEL_BLOCK
N_SEG = SEQ // CMP_STRIDE
REL_BUCKETS = 32
REL_MAX_DIST = 128
N_BRANCH = 3
BRANCH_W = 512
D_FF = 2816
PLE_DIM = 256
EPS = 1e-6
NEG = -1e30
FORCE_SCORE = 1e4

LANES = 128
VMEM_LIMIT = 48 * 1024 * 1024

F32 = jnp.float32
BF16 = jnp.bfloat16
HP = lax.Precision.HIGHEST

CB_MG = 0
CB_A = 24
CB_B = 40
CB_CQ = 56
CB_CKC, CB_CVC, CB_CKS, CB_CVS, CB_CKW, CB_CVW = 60, 61, 62, 63, 64, 65
CB_SMALL = 66
N_COLS = 68 * LANES
SM_AA, SM_AB, SM_BI, SM_BF, SM_CG = 0, 4, 8, 12, 16


def _mm(a, b, precision=None):
    return lax.dot_general(a, b, (((1,), (0,)), ((), ())), precision=precision,
                           preferred_element_type=F32)


def _mm_nt(a, b, precision=None):
    return lax.dot_general(a, b, (((1,), (1,)), ((), ())), precision=precision,
                           preferred_element_type=F32)


def _mm_tn(a, b, precision=None):
    return lax.dot_general(a, b, (((0,), (0,)), ((), ())), precision=precision,
                           preferred_element_type=F32)


def _bf(x):
    return x.astype(BF16)


def _sigmoid(x):
    return 1.0 / (1.0 + jnp.exp(-x))


def _softplus(x):
    return jnp.maximum(x, 0.0) + jnp.log1p(jnp.exp(-jnp.abs(x)))


def _rms_rows(x, g):
    return x * lax.rsqrt(jnp.mean(x * x, axis=-1, keepdims=True) + EPS) * g


def _params(*sem):
    return pltpu.CompilerParams(dimension_semantics=sem, vmem_limit_bytes=VMEM_LIMIT)


def _proj_kernel(x_ref, g_ref, w_ref, o_ref, u_ref):
    @pl.when(pl.program_id(1) == 0)
    def _():
        u_ref[...] = _bf(_rms_rows(x_ref[...], g_ref[...]))

    o_ref[...] = _mm(u_ref[...], w_ref[...])


def _in_proj(x2, gain, w):
    m, d = x2.shape
    n = w.shape[1]
    tm, tn = 1024, 512
    return pl.pallas_call(
        _proj_kernel,
        grid=(m // tm, n // tn),
        in_specs=[pl.BlockSpec((tm, d), lambda i, j: (i, 0)),
                  pl.BlockSpec((1, d), lambda i, j: (0, 0)),
                  pl.BlockSpec((d, tn), lambda i, j: (0, j))],
        out_specs=pl.BlockSpec((tm, tn), lambda i, j: (i, j)),
        out_shape=jax.ShapeDtypeStruct((m, n), F32),
        scratch_shapes=[pltpu.VMEM((tm, d), BF16)],
        compiler_params=_params("parallel", "arbitrary"),
        name="in_proj",
    )(x2, gain, w)


def _chunk_masks():
    ri = lax.broadcasted_iota(jnp.int32, (CHUNK, CHUNK), 0)
    ci = lax.broadcasted_iota(jnp.int32, (CHUNK, CHUNK), 1)
    return ri, ci


def _chunk_cumsum(x):
    rowi = lax.broadcasted_iota(jnp.int32, x.shape, 0)
    s = 1
    while s < CHUNK:
        x = x + jnp.where(rowi >= s, pltpu.roll(x, s, axis=0), 0.0)
        s *= 2
    return x


def _pick_lane(x, lane_idx):
    lane = lax.broadcasted_iota(jnp.int32, x.shape, 1)
    return jnp.sum(jnp.where(lane == lane_idx, x, 0.0), axis=1, keepdims=True)


def _col_to_row(col, eye):
    return jnp.sum(jnp.where(eye, col, 0.0), axis=0, keepdims=True)


def _inv_unit_lower(x, bd, eyef):
    xd = jnp.where(bd, x, 0.0)
    xo = jnp.where(bd, 0.0, x)
    x2 = _mm(xd, xd, HP)
    x4 = _mm(x2, x2, HP)
    x8 = _mm(x4, x4, HP)
    p = eyef - xd
    p = p + _mm(p, x2, HP)
    p = p + _mm(p, x4, HP)
    p = p + _mm(p, x8, HP)
    m = _mm(p, xo, HP)
    m2 = _mm(m, m, HP)
    q = eyef - m
    q = q + _mm(q, m2, HP)
    return _mm(q, p, HP)


def _gdn_kernel(q_ref, k_ref, v_ref, z_ref, sm_ref, cwq_ref, cwk_ref, cwv_ref, alog_ref, dtb_ref,
                nw_ref, o_ref, qg_s, kd_s, u_s, w_s, at_s, eg_s):
    h = pl.program_id(1)
    L = CHUNK
    ri, ci = _chunk_masks()
    tril = ri >= ci
    strict = ri > ci
    eye = ri == ci
    bd = (ri >> 4) == (ci >> 4)
    eyef = jnp.where(eye, 1.0, 0.0)
    alog = alog_ref[...]
    dtb = dtb_ref[...]
    nw = nw_ref[...]
    cwq, cwk, cwv = cwq_ref[...], cwk_ref[...], cwv_ref[...]

    def conv_silu(ref, w, n, r0):
        cur = ref[pl.ds(r0, L), :]
        p0 = pl.multiple_of(jnp.maximum(r0 - 8, 0), 8)
        prev = jnp.where(n > 0, ref[pl.ds(p0, 8), :], 0.0)
        win = jnp.concatenate([prev, cur], axis=0)
        acc = cur * w[GDN_CONV - 1:GDN_CONV, :]
        for s in range(1, GDN_CONV):
            acc = acc + pltpu.roll(win, s, axis=0)[8:, :] * w[GDN_CONV - 1 - s:GDN_CONV - s, :]
        return acc * _sigmoid(acc)

    def prep(n, carry):
        r0 = pl.multiple_of(n * L, L)
        q = conv_silu(q_ref, cwq, n, r0)
        k = conv_silu(k_ref, cwk, n, r0)
        v = conv_silu(v_ref, cwv, n, r0)
        q = q * lax.rsqrt(jnp.sum(q * q, axis=-1, keepdims=True) + 1e-6) * (D_HEAD_REC ** -0.5)
        k = k * lax.rsqrt(jnp.sum(k * k, axis=-1, keepdims=True) + 1e-6)
        x = sm_ref[pl.ds(r0, L), :]
        g = _chunk_cumsum(-jnp.exp(alog) * _softplus(x + dtb))
        gc = _pick_lane(g, SM_AA + h)
        beta = _pick_lane(_sigmoid(x), SM_AB + h)
        diff = gc - _col_to_row(gc, eye)
        decay = jnp.where(tril, jnp.exp(jnp.where(tril, diff, 0.0)), 0.0)
        kb = k * beta
        xm = jnp.where(strict, _mm_nt(kb, k, HP) * decay, 0.0)
        t = _bf(_inv_unit_lower(xm, bd, eyef))
        egc = jnp.exp(gc)
        gl = gc[L - 1:L, :]
        u_s[pl.ds(r0, L), :] = _mm(t, _bf(v * beta))
        w_s[pl.ds(r0, L), :] = _mm(t, _bf(kb * egc))
        at_s[pl.ds(r0, L), :] = _mm_nt(_bf(q), _bf(k)) * decay
        qg_s[pl.ds(r0, L), :] = q * egc
        kd_s[pl.ds(r0, L), :] = k * jnp.exp(gl - gc)
        eg_s[pl.ds(pl.multiple_of(n * 8, 8), 8), :] = jnp.broadcast_to(jnp.exp(gl), (8, LANES))
        return carry

    lax.fori_loop(0, SEQ // L, prep, 0)

    def step(n, state):
        r0 = pl.multiple_of(n * L, L)
        sb = _bf(state)
        eg = eg_s[pl.ds(pl.multiple_of(n * 8, 8), 1), :]
        vn = u_s[pl.ds(r0, L), :] - _mm(_bf(w_s[pl.ds(r0, L), :]), sb)
        vnb = _bf(vn)
        o = _mm(_bf(qg_s[pl.ds(r0, L), :]), sb) + _mm(_bf(at_s[pl.ds(r0, L), :]), vnb)
        state = state * eg + _mm_tn(_bf(kd_s[pl.ds(r0, L), :]), vnb)
        z = z_ref[pl.ds(r0, L), :]
        o_ref[pl.ds(r0, L), :] = _rms_rows(o, nw) * (z * _sigmoid(z))
        return state

    lax.fori_loop(0, SEQ // L, step, jnp.zeros((D_HEAD_REC, D_HEAD_REC), F32))


def _gdn(y3, conv_w, alog_row, dtb_row, norm_w):
    bsz = y3.shape[0]
    H = N_HEADS_REC

    def col(off):
        return pl.BlockSpec((None, SEQ, LANES), lambda b, h, off=off: (b, 0, off + h))

    def cw(off):
        return pl.BlockSpec((GDN_CONV, LANES), lambda b, h, off=off: (0, off + h))

    row = pl.BlockSpec((1, LANES), lambda b, h: (0, 0))
    seq_scr = pltpu.VMEM((SEQ, LANES), F32)
    return pl.pallas_call(
        _gdn_kernel,
        grid=(bsz, H),
        in_specs=[col(CB_A), col(CB_A + H), col(CB_A + 2 * H), col(CB_A + 3 * H),
                  pl.BlockSpec((None, SEQ, LANES), lambda b, h: (b, 0, CB_SMALL)),
                  cw(0), cw(H), cw(2 * H), row, row, row],
        out_specs=pl.BlockSpec((None, SEQ, LANES), lambda b, h: (b, 0, h)),
        out_shape=jax.ShapeDtypeStruct((bsz, SEQ, H * D_HEAD_REC), F32),
        scratch_shapes=[seq_scr, seq_scr, seq_scr, seq_scr, pltpu.VMEM((SEQ, CHUNK), F32),
                        pltpu.VMEM((SEQ // CHUNK * 8, LANES), F32)],
        compiler_params=_params("parallel", "parallel"),
        name="gdn",
    )(y3, y3, y3, y3, y3, conv_w, conv_w, conv_w, alog_row, dtb_row, norm_w)


def _mlstm_kernel(q_ref, k_ref, v_ref, og_ref, sm_ref, bi_ref, bf_ref, nw_ref, o_ref):
    h = pl.program_id(1)
    L = CHUNK
    ri, ci = _chunk_masks()
    tril = ri >= ci
    eye = ri == ci
    bi = bi_ref[...]
    bfr = bf_ref[...]
    nw = nw_ref[...]

    def step(n, carry):
        c_st, n_st, m_st = carry
        r0 = pl.multiple_of(n * L, L)
        q = q_ref[pl.ds(r0, L), :]
        k = k_ref[pl.ds(r0, L), :] * (D_HEAD_REC ** -0.5)
        v = v_ref[pl.ds(r0, L), :]
        qb, kb, vb = _bf(q), _bf(k), _bf(v)
        x = sm_ref[pl.ds(r0, L), :]
        bcum = _pick_lane(_chunk_cumsum(-_softplus(-(x + bfr))), SM_BF + h)
        it = _pick_lane(x + bi, SM_BI + h)
        dm = jnp.where(tril, bcum - _col_to_row(bcum, eye) + _col_to_row(it, eye), NEG)
        dmax = jnp.max(dm, axis=1, keepdims=True)
        qk = _mm_nt(qb, kb)
        a = bcum + m_st
        mt = jnp.maximum(a, dmax)
        sm = jnp.exp(dm - mt) * qk
        si = jnp.exp(a - mt)
        num = si * _mm(qb, _bf(c_st)) + _mm(_bf(sm), vb)
        den = si * jnp.sum(q * n_st, axis=1, keepdims=True) + jnp.sum(sm, axis=1, keepdims=True)
        hh = num / jnp.maximum(jnp.abs(den), jnp.exp(-mt))
        bl = bcum[L - 1:L, :]
        ds = bl - bcum + it
        m_new = jnp.maximum(bl + m_st, jnp.max(ds, axis=0, keepdims=True))
        wk = k * jnp.exp(ds - m_new)
        sc = jnp.exp(bl + m_st - m_new)
        c_st = sc * c_st + _mm_tn(_bf(wk), vb)
        n_st = sc * n_st + jnp.sum(wk, axis=0, keepdims=True)
        o_ref[pl.ds(r0, L), :] = _rms_rows(hh, nw) * _sigmoid(og_ref[pl.ds(r0, L), :])
        return c_st, n_st, m_new

    init = (jnp.zeros((D_HEAD_REC, D_HEAD_REC), F32), jnp.zeros((1, D_HEAD_REC), F32),
            jnp.zeros((1, 1), F32))
    lax.fori_loop(0, SEQ // L, step, init)


def _mlstm(y3, bi_row, bf_row, norm_w):
    bsz = y3.shape[0]
    H = N_HEADS_REC

    def col(off):
        return pl.BlockSpec((None, SEQ, LANES), lambda b, h, off=off: (b, 0, off + h))

    row = pl.BlockSpec((1, LANES), lambda b, h: (0, 0))
    return pl.pallas_call(
        _mlstm_kernel,
        grid=(bsz, H),
        in_specs=[col(CB_B), col(CB_B + H), col(CB_B + 2 * H), col(CB_B + 3 * H),
                  pl.BlockSpec((None, SEQ, LANES), lambda b, h: (b, 0, CB_SMALL)),
                  row, row, row],
        out_specs=pl.BlockSpec((None, SEQ, LANES), lambda b, h: (b, 0, h)),
        out_shape=jax.ShapeDtypeStruct((bsz, SEQ, H * D_HEAD_REC), F32),
        compiler_params=_params("parallel", "parallel"),
        name="mlstm",
    )(y3, y3, y3, y3, y3, bi_row, bf_row, norm_w)


def _nsa_prep_kernel(qt_ref, ks_ref, kw_ref, k2_ref, v2_ref, wkl_ref, wkh_ref, wvl_ref, wvh_ref,
                     pekl_ref, pekh_ref, pevl_ref, pevh_ref, qn_ref, kn0_ref, kn1_ref, kn2_ref,
                     qn_out, ks_out, kw_out, ck_out, cv_out):
    qg = qn_ref[...] * (NSA_DH ** -0.5)
    zq = jnp.zeros((NSA_DH, NSA_HPG * Q_BLOCK), F32)
    zc = jnp.zeros((N_SEG, NSA_DH), F32)
    for g in range(NSA_GROUPS):
        for c in range(N_QBLK):
            x = qt_ref[g, c]
            xn = x * lax.rsqrt(jnp.mean(x * x, axis=0, keepdims=True) + EPS) * qg
            parts = [xn, zq] if g == 0 else [zq, xn]
            qn_out[g, c] = _bf(jnp.concatenate(parts, axis=0))
        k2 = k2_ref[g]
        v2 = v2_ref[g]
        a = _mm(_bf(k2 + pekl_ref[...]), wkl_ref[...])
        bh = _mm(_bf(k2 + pekh_ref[...]), wkh_ref[...])
        ck = _rms_rows(a + pltpu.roll(bh, N_SEG - 1, axis=0), kn0_ref[...])
        parts = [ck, zc] if g == 0 else [zc, ck]
        ck_out[g] = _bf(jnp.concatenate(parts, axis=1))
        at = _mm_nt(wvl_ref[...], _bf(v2 + pevl_ref[...]))
        bt = _mm_nt(wvh_ref[...], _bf(v2 + pevh_ref[...]))
        cv_out[g] = _bf(at + pltpu.roll(bt, N_SEG - 1, axis=1))

    lane = lax.broadcasted_iota(jnp.int32, (Q_BLOCK, LANES), 1)
    lo = lane < NSA_DH

    def norm_keys(src, dst, gain):
        def body(i, carry):
            r0 = pl.multiple_of(i * Q_BLOCK, Q_BLOCK)
            x = src[pl.ds(r0, Q_BLOCK), :]
            xx = x * x
            s0 = jnp.sum(jnp.where(lo, xx, 0.0), axis=1, keepdims=True)
            s1 = jnp.sum(jnp.where(lo, 0.0, xx), axis=1, keepdims=True)
            ms = jnp.where(lo, s0, s1) * (1.0 / NSA_DH)
            dst[pl.ds(r0, Q_BLOCK), :] = _bf(x * lax.rsqrt(ms + EPS) * gain)
            return carry
        lax.fori_loop(0, N_QBLK, body, 0)

    norm_keys(ks_ref, ks_out, kn1_ref[...])
    norm_keys(kw_ref, kw_out, kn2_ref[...])


def _nsa_prep(y3, qt, k2, v2, wkl, wkh, wvl, wvh, pekl, pekh, pevl, pevh, qn, kn0, kn1, kn2):
    bsz = y3.shape[0]
    G = NSA_GROUPS
    W4 = NSA_HPG * Q_BLOCK

    def full(a):
        nd = a.ndim
        return pl.BlockSpec(a.shape, lambda b, nd=nd: (0,) * nd)

    def per_b(shape):
        nd = len(shape)
        return pl.BlockSpec((None,) + shape, lambda b, nd=nd: (b,) + (0,) * nd)

    consts = [wkl, wkh, wvl, wvh, pekl, pekh, pevl, pevh, qn, kn0, kn1, kn2]
    return pl.pallas_call(
        _nsa_prep_kernel,
        grid=(bsz,),
        in_specs=[per_b((G, N_QBLK, NSA_DH, W4)),
                  pl.BlockSpec((None, SEQ, LANES), lambda b: (b, 0, CB_CKS)),
                  pl.BlockSpec((None, SEQ, LANES), lambda b: (b, 0, CB_CKW)),
                  per_b((G, N_SEG, CMP_STRIDE * NSA_DH)), per_b((G, N_SEG, CMP_STRIDE * NSA_DH))]
                 + [full(a) for a in consts],
        out_specs=[per_b((G, N_QBLK, 2 * NSA_DH, W4)), per_b((SEQ, LANES)), per_b((SEQ, LANES)),
                   per_b((G, N_SEG, LANES)), per_b((G, NSA_DH, N_SEG))],
        out_shape=[jax.ShapeDtypeStruct((bsz, G, N_QBLK, 2 * NSA_DH, W4), BF16),
                   jax.ShapeDtypeStruct((bsz, SEQ, LANES), BF16),
                   jax.ShapeDtypeStruct((bsz, SEQ, LANES), BF16),
                   jax.ShapeDtypeStruct((bsz, G, N_SEG, LANES), BF16),
                   jax.ShapeDtypeStruct((bsz, G, NSA_DH, N_SEG), BF16)],
        compiler_params=_params("parallel"),
        name="nsa_prep",
    )(qt, y3, y3, k2, v2, *consts)


def _nsa_kernel(q_ref, ck_ref, cv_ref, ks_ref, vs_ref, kw_ref, vw_ref, bc_ref, bt_ref, b31_ref,
                cg_ref, o_ref, sel_s, m_s, l_s, acc_s, m_w, l_w, acc_w):
    c = pl.program_id(2)
    W4 = NSA_HPG * Q_BLOCK
    q = q_ref[...]
    kpos = lax.broadcasted_iota(jnp.int32, (Q_BLOCK, W4), 0)
    qpos = lax.broadcasted_iota(jnp.int32, (Q_BLOCK, W4), 1) & (Q_BLOCK - 1)

    s = _mm(ck_ref[...], q) + bc_ref[...]
    valid = (CMP_STRIDE * kpos + (CMP_BLOCK - 1)) <= (c * Q_BLOCK + qpos)
    s = jnp.where(valid, s, NEG)
    e = jnp.where(valid, jnp.exp(s - jnp.max(s, axis=0, keepdims=True)), 0.0)
    p = e / jnp.maximum(jnp.sum(e, axis=0, keepdims=True), 1e-30)
    o_cmp = _mm(cv_ref[...], _bf(p))

    psum = (p[:, 0:Q_BLOCK] + p[:, Q_BLOCK:2 * Q_BLOCK] + p[:, 2 * Q_BLOCK:3 * Q_BLOCK]
            + p[:, 3 * Q_BLOCK:4 * Q_BLOCK])
    jj = lax.broadcasted_iota(jnp.int32, (N_SEL, N_SEG), 0)
    nn = lax.broadcasted_iota(jnp.int32, (N_SEL, N_SEG), 1)
    ratio = SEL_BLOCK // CMP_STRIDE
    ov = jnp.where((nn >= ratio * jj - 1) & (nn <= ratio * jj + ratio - 1) & (nn < N_SEG - 1), 1.0, 0.0)
    imp = _mm(ov, psum, HP)
    jb = lax.broadcasted_iota(jnp.int32, (N_SEL, Q_BLOCK), 0)
    tq = c * Q_BLOCK + lax.broadcasted_iota(jnp.int32, (N_SEL, Q_BLOCK), 1)
    cur = tq >> 6
    imp = jnp.where((jb == 0) | (jb == cur), FORCE_SCORE, jnp.where(jb <= cur, imp, -1.0))
    rank = jnp.zeros((N_SEL, Q_BLOCK), F32)
    for jp in range(N_SEL):
        rowv = imp[jp:jp + 1, :]
        rank = rank + jnp.where(rowv > imp, 1.0, 0.0)
        if jp < N_SEL - 1:
            rank = rank + jnp.where(jb > jp, jnp.where(rowv == imp, 1.0, 0.0), 0.0)
    sel_s[...] = jnp.where(rank < SEL_TOPN, 1.0, 0.0)

    b31 = b31_ref[...]

    def attend(kt, vt, bias, valid, m_ref, l_ref, acc_ref):
        sc = jnp.where(valid, _mm(kt, q) + bias, NEG)
        m_old = m_ref[...]
        m_new = jnp.maximum(m_old, jnp.max(sc, axis=0, keepdims=True))
        alpha = jnp.exp(m_old - m_new)
        pe = jnp.where(valid, jnp.exp(sc - m_new), 0.0)
        m_ref[...] = m_new
        l_ref[...] = alpha * l_ref[...] + jnp.sum(pe, axis=0, keepdims=True)
        acc_ref[...] = alpha * acc_ref[...] + _mm(vt, _bf(pe))

    def reset(m_ref, l_ref, acc_ref):
        m_ref[...] = jnp.full(m_ref.shape, NEG, F32)
        l_ref[...] = jnp.zeros(l_ref.shape, F32)
        acc_ref[...] = jnp.zeros(acc_ref.shape, F32)

    reset(m_s, l_s, acc_s)
    half = SEL_BLOCK

    def sel_mask(jt):
        r0 = jnp.broadcast_to(sel_s[pl.ds(2 * jt, 1), :], (half, Q_BLOCK))
        r1 = jnp.broadcast_to(sel_s[pl.ds(2 * jt + 1, 1), :], (half, Q_BLOCK))
        m1 = jnp.concatenate([r0, r1], axis=0)
        return jnp.concatenate([m1] * NSA_HPG, axis=1) > 0.5

    def far_tile(jt, carry):
        attend(ks_ref[jt], vs_ref[jt], b31, sel_mask(jt), m_s, l_s, acc_s)
        return carry

    lax.fori_loop(0, jnp.maximum(c - 1, 0), far_tile, 0)

    @pl.when(c >= 1)
    def _():
        attend(ks_ref[c - 1], vs_ref[c - 1], bt_ref[1], sel_mask(c - 1), m_s, l_s, acc_s)

    attend(ks_ref[c], vs_ref[c], bt_ref[0], sel_mask(c) & (kpos <= qpos), m_s, l_s, acc_s)
    o_sel = acc_s[...] / jnp.maximum(l_s[...], 1e-30)

    reset(m_w, l_w, acc_w)
    n_wt = WINDOW // Q_BLOCK
    for d in range(n_wt, -1, -1):
        if d == n_wt:
            valid = kpos > qpos
        elif d == 0:
            valid = kpos <= qpos
        else:
            valid = kpos >= 0
        bias = bt_ref[d] if d < 2 else b31

        @pl.when(c >= d)
        def _(d=d, valid=valid, bias=bias):
            attend(kw_ref[c - d], vw_ref[c - d], bias, valid, m_w, l_w, acc_w)

    o_win = acc_w[...] / jnp.maximum(l_w[...], 1e-30)

    gates = _sigmoid(cg_ref[...])

    def gate_row(r):
        return jnp.concatenate([gates[k * 3 + r:k * 3 + r + 1, :] for k in range(NSA_HPG)], axis=1)

    o_ref[...] = gate_row(0) * o_cmp + gate_row(1) * o_sel + gate_row(2) * o_win


def _nsa_attend(qn, ck, cv, ksn, vst, kwn, vwt, bias_c, bias_t, bias_far, cgt):
    bsz = qn.shape[0]
    G = NSA_GROUPS
    W4 = NSA_HPG * Q_BLOCK
    row_scr = pltpu.VMEM((1, W4), F32)
    acc_scr = pltpu.VMEM((NSA_DH, W4), F32)
    return pl.pallas_call(
        _nsa_kernel,
        grid=(bsz, G, N_QBLK),
        in_specs=[
            pl.BlockSpec((None, None, None, 2 * NSA_DH, W4), lambda b, g, c: (b, g, c, 0, 0)),
            pl.BlockSpec((None, None, N_SEG, LANES), lambda b, g, c: (b, g, 0, 0)),
            pl.BlockSpec((None, None, NSA_DH, N_SEG), lambda b, g, c: (b, g, 0, 0)),
            pl.BlockSpec((None, N_QBLK, Q_BLOCK, LANES), lambda b, g, c: (b, 0, 0, 0)),
            pl.BlockSpec((None, None, N_QBLK, NSA_DH, Q_BLOCK), lambda b, g, c: (b, g, 0, 0, 0)),
            pl.BlockSpec((None, N_QBLK, Q_BLOCK, LANES), lambda b, g, c: (b, 0, 0, 0)),
            pl.BlockSpec((None, None, N_QBLK, NSA_DH, Q_BLOCK), lambda b, g, c: (b, g, 0, 0, 0)),
            pl.BlockSpec((None, None, N_SEG, W4), lambda b, g, c: (g, c, 0, 0)),
            pl.BlockSpec((None, 2, Q_BLOCK, W4), lambda b, g, c: (g, 0, 0, 0)),
            pl.BlockSpec((None, 1, W4), lambda b, g, c: (g, 0, 0)),
            pl.BlockSpec((None, None, NSA_HPG * 3, Q_BLOCK), lambda b, g, c: (b, g, 0, c)),
        ],
        out_specs=pl.BlockSpec((None, None, None, NSA_DH, W4), lambda b, g, c: (b, g, c, 0, 0)),
        out_shape=jax.ShapeDtypeStruct((bsz, G, N_QBLK, NSA_DH, W4), F32),
        scratch_shapes=[pltpu.VMEM((N_SEL, Q_BLOCK), F32), row_scr, row_scr, acc_scr,
                        row_scr, row_scr, acc_scr],
        compiler_params=_params("parallel", "parallel", "arbitrary"),
        name="nsa_attend",
    )(qn, ck, cv, ksn, vst, kwn, vwt, bias_c, bias_t, bias_far, cgt)


def _bucket(dist):
    max_exact = REL_BUCKETS // 2
    d = jnp.maximum(dist, 0)
    df = jnp.maximum(d, 1).astype(F32)
    large = max_exact + (jnp.log(df / max_exact) / math.log(REL_MAX_DIST / max_exact)
                         * (REL_BUCKETS - max_exact)).astype(jnp.int32)
    large = jnp.minimum(large, REL_BUCKETS - 1)
    return jnp.where(d < max_exact, d, large)


def _bias_tables(rel_bias):
    G, K = NSA_GROUPS, NSA_HPG
    t = np.arange(SEQ)
    cmp_end = np.arange(N_SEG) * CMP_STRIDE + CMP_BLOCK - 1
    dist_c = jnp.asarray((t[:, None] - cmp_end[None]).astype(np.int32))
    bc = rel_bias[_bucket(dist_c)].reshape(N_QBLK, Q_BLOCK, N_SEG, G, K)
    bc = bc.transpose(3, 0, 2, 4, 1).reshape(G, N_QBLK, N_SEG, K * Q_BLOCK)
    i = np.arange(Q_BLOCK)
    dist_t = np.arange(2)[:, None, None] * Q_BLOCK + i[None, None, :] - i[None, :, None]
    bt = rel_bias[_bucket(jnp.asarray(dist_t.astype(np.int32)))].reshape(2, Q_BLOCK, Q_BLOCK, G, K)
    bt = bt.transpose(3, 0, 1, 4, 2).reshape(G, 2, Q_BLOCK, K * Q_BLOCK)
    far = jnp.repeat(rel_bias[REL_BUCKETS - 1].reshape(G, K), Q_BLOCK, axis=1).reshape(G, 1, K * Q_BLOCK)
    return bc, bt, far


def _nsa(y3, small, q_norm, k_norm, cmp_pe, w_cmp, tables):
    bsz = y3.shape[0]
    G, K, DH = NSA_GROUPS, NSA_HPG, NSA_DH

    def cols(cb, n=1):
        return y3[:, :, cb * LANES:(cb + n) * LANES]

    qt = cols(CB_CQ, 4).reshape(bsz, N_QBLK, Q_BLOCK, G, K, DH)
    qt = qt.transpose(0, 3, 1, 5, 4, 2).reshape(bsz, G, N_QBLK, DH, K * Q_BLOCK)

    def segs(x):
        return x.reshape(bsz, N_SEG, CMP_STRIDE, G, DH).transpose(0, 3, 1, 2, 4).reshape(
            bsz, G, N_SEG, CMP_STRIDE * DH)

    def tiles_t(x):
        return _bf(x.reshape(bsz, N_QBLK, Q_BLOCK, G, DH).transpose(0, 3, 1, 4, 2))

    wk = w_cmp[0].reshape(2, CMP_STRIDE * DH, DH)
    wv = w_cmp[1].reshape(2, CMP_STRIDE * DH, DH)
    pe = cmp_pe.reshape(2, 2, 1, CMP_STRIDE * DH)
    qn, ksn, kwn, ck, cv = _nsa_prep(
        y3, qt, segs(cols(CB_CKC)), segs(cols(CB_CVC)),
        _bf(wk[0]), _bf(wk[1]), _bf(wv[0].T), _bf(wv[1].T),
        pe[0, 0], pe[0, 1], pe[1, 0], pe[1, 1],
        q_norm.reshape(DH, 1), k_norm[0].reshape(1, DH),
        jnp.tile(k_norm[1], G).reshape(1, LANES), jnp.tile(k_norm[2], G).reshape(1, LANES))
    cgt = small[:, :, SM_CG:SM_CG + NSA_HEADS * 3].reshape(bsz, SEQ, G, K * 3).transpose(0, 2, 3, 1)
    bc, bt, far = tables
    ot = _nsa_attend(qn, ck, cv, ksn.reshape(bsz, N_QBLK, Q_BLOCK, LANES), tiles_t(cols(CB_CVS)),
                     kwn.reshape(bsz, N_QBLK, Q_BLOCK, LANES), tiles_t(cols(CB_CVW)), bc, bt, far, cgt)
    ot = ot.reshape(bsz, G, N_QBLK, DH, K, Q_BLOCK).transpose(0, 2, 5, 1, 4, 3)
    return ot.reshape(bsz, SEQ, NSA_HEADS * DH)


def _merge_kernel(h_ref, ya_ref, yb_ref, yc_ref, ga_ref, gb_ref, gc_ref, wb_ref, wo_ref, o_ref):
    acc = None
    for n, (y_ref, g_ref) in enumerate(((ya_ref, ga_ref), (yb_ref, gb_ref), (yc_ref, gc_ref))):
        t = _sigmoid(g_ref[...]) * _mm(_bf(y_ref[...]), wb_ref[n])
        acc = t if acc is None else acc + t
    o_ref[...] = h_ref[...] + _mm(_bf(acc), wo_ref[...])


def _merge(h2, ya, yb, yc, y2, wb, wo):
    m, d = h2.shape
    tm = 512
    row = lambda w: pl.BlockSpec((tm, w), lambda i: (i, 0))
    gate = lambda n: pl.BlockSpec((tm, d), lambda i, n=n: (i, n))
    return pl.pallas_call(
        _merge_kernel,
        grid=(m // tm,),
        in_specs=[row(d), row(BRANCH_W), row(BRANCH_W), row(BRANCH_W), gate(0), gate(1), gate(2),
                  pl.BlockSpec((N_BRANCH, BRANCH_W, d), lambda i: (0, 0, 0)),
                  pl.BlockSpec((d, d), lambda i: (0, 0))],
        out_specs=row(d),
        out_shape=jax.ShapeDtypeStruct((m, d), F32),
        compiler_params=_params("parallel"),
        name="merge",
    )(h2, ya, yb, yc, y2, y2, y2, wb, wo)


def _ffn_kernel(h_ref, g_ref, wg_ref, wu_ref, wo_ref, o_ref, u_ref, acc_ref):
    f = pl.program_id(1)

    @pl.when(f == 0)
    def _():
        x = h_ref[...]
        u_ref[...] = _bf(_rms_rows(x, g_ref[...]))
        acc_ref[...] = x

    u = u_ref[...]
    gt = _mm(u, wg_ref[...])
    up = _mm(u, wu_ref[...])
    acc_ref[...] += _mm(_bf(gt * _sigmoid(gt) * up), wo_ref[...])

    @pl.when(f == pl.num_programs(1) - 1)
    def _():
        o_ref[...] = acc_ref[...]


def _ffn(h2, gain, w_in, w_out):
    m, d = h2.shape
    tm, nf = 512, 2
    tf = D_FF // nf
    return pl.pallas_call(
        _ffn_kernel,
        grid=(m // tm, nf),
        in_specs=[pl.BlockSpec((tm, d), lambda i, f: (i, 0)),
                  pl.BlockSpec((1, d), lambda i, f: (0, 0)),
                  pl.BlockSpec((d, tf), lambda i, f: (0, f)),
                  pl.BlockSpec((d, tf), lambda i, f: (0, f + nf)),
                  pl.BlockSpec((tf, d), lambda i, f: (f, 0))],
        out_specs=pl.BlockSpec((tm, d), lambda i, f: (i, 0)),
        out_shape=jax.ShapeDtypeStruct((m, d), F32),
        scratch_shapes=[pltpu.VMEM((tm, d), BF16), pltpu.VMEM((tm, d), F32)],
        compiler_params=_params("parallel", "arbitrary"),
        name="ffn",
    )(h2, gain, w_in, w_in, w_out)


def _ple_kernel(h_ref, g_ref, p_ref, wg_ref, wp_ref, o_ref):
    x = h_ref[...]
    gate = _sigmoid(_mm(_bf(_rms_rows(x, g_ref[...])), wg_ref[...]))
    o_ref[...] = x + gate * _mm(_bf(p_ref[...]), wp_ref[...])


def _ple(h2, gain, p2, wg, wp):
    m, d = h2.shape
    tm = 512
    return pl.pallas_call(
        _ple_kernel,
        grid=(m // tm,),
        in_specs=[pl.BlockSpec((tm, d), lambda i: (i, 0)),
                  pl.BlockSpec((1, d), lambda i: (0, 0)),
                  pl.BlockSpec((tm, PLE_DIM), lambda i: (i, 0)),
                  pl.BlockSpec((d, d), lambda i: (0, 0)),
                  pl.BlockSpec((PLE_DIM, d), lambda i: (0, 0))],
        out_specs=pl.BlockSpec((tm, d), lambda i: (i, 0)),
        out_shape=jax.ShapeDtypeStruct((m, d), F32),
        compiler_params=_params("parallel"),
        name="ple",
    )(h2, gain, p2, wg, wp)


def _reorder_w_in(w):
    hd = N_HEADS_REC * D_HEAD_REC
    sizes = [hd, hd, hd, hd, N_HEADS_REC, N_HEADS_REC, hd, hd, hd, hd, N_HEADS_REC, N_HEADS_REC,
             NSA_HEADS * NSA_DH] + [NSA_GROUPS * NSA_DH] * 6 + [NSA_HEADS * 3, N_BRANCH * D_MODEL]
    offs = np.concatenate([[0], np.cumsum(sizes)])
    seg = [w[:, offs[i]:offs[i + 1]] for i in range(len(sizes))]
    (aq, ak, av, az, aa, ab, bq, bk, bv, bo, bi, bf, cq, ckc, cvc, cks, cvs, ckw, cvw, cg, mg) = seg
    small = jnp.concatenate([aa, ab, bi, bf, cg], axis=1)
    main = jnp.concatenate([mg, aq, ak, av, az, bq, bk, bv, bo, cq, ckc, cvc, cks, cvs, ckw, cvw, small],
                           axis=1)
    return _bf(jnp.pad(main, ((0, 0), (0, N_COLS - main.shape[1]))))


def _lane_row(vals, offset):
    return jnp.zeros((1, LANES), F32).at[0, offset:offset + vals.shape[0]].set(vals)


def kernel(x, p, rel_bias, norm_mix, w_in, conv_w, gdn_a_log, gdn_dt_bias, gdn_norm, mlstm_b_i, mlstm_b_f,
           mlstm_norm, nsa_q_norm, nsa_k_norm, nsa_cmp_pe, nsa_w_cmp, w_branch, w_out, norm_ffn, w_ffn_in,
           w_ffn_out, norm_ple, w_ple_gate, w_ple_proj):
    bsz, seq, d = x.shape
    assert seq == SEQ and d == D_MODEL
    depth = w_in.shape[0]
    m = bsz * seq
    tables = _bias_tables(rel_bias)
    h2 = x.reshape(m, d)
    for l in range(depth):
        y2 = _in_proj(h2, norm_mix[l].reshape(1, d), _reorder_w_in(w_in[l]))
        y3 = y2.reshape(bsz, seq, N_COLS)
        small = y3[:, :, CB_SMALL * LANES:(CB_SMALL + 1) * LANES]
        ya = _gdn(y3, conv_w[l], _lane_row(gdn_a_log[l], SM_AA), _lane_row(gdn_dt_bias[l], SM_AA),
                  gdn_norm[l].reshape(1, D_HEAD_REC))
        yb = _mlstm(y3, _lane_row(mlstm_b_i[l], SM_BI), _lane_row(mlstm_b_f[l], SM_BF),
                    mlstm_norm[l].reshape(1, D_HEAD_REC))
        yc = _nsa(y3, small, nsa_q_norm[l], nsa_k_norm[l], nsa_cmp_pe[l], nsa_w_cmp[l], tables)
        h2 = _merge(h2, ya.reshape(m, BRANCH_W), yb.reshape(m, BRANCH_W), yc.reshape(m, BRANCH_W), y2,
                    _bf(w_branch[l]), _bf(w_out[l]))
        h2 = _ffn(h2, norm_ffn[l].reshape(1, d), _bf(w_ffn_in[l]), _bf(w_ffn_out[l]))
        h2 = _ple(h2, norm_ple[l].reshape(1, d), p[l].reshape(m, PLE_DIM), _bf(w_ple_gate[l]),
                  _bf(w_ple_proj[l]))
    return h2.reshape(bsz, seq, d)
```

```python
import functools
import math

import numpy as np
import jax
import jax.numpy as jnp
from jax import lax
from jax.experimental import pallas as pl
from jax.experimental.pallas import tpu as pltpu

D_MODEL = 1024
SEQ = 2048
N_HEADS_REC = 4
D_HEAD_REC = 128
CHUNK = 64
GDN_HPS = 2
GDN_CONV = 4
NSA_HEADS = 8
NSA_GROUPS = 2
NSA_HPG = NSA_HEADS // NSA_GROUPS
NSA_DH = 64
CMP_BLOCK = 32
CMP_STRIDE = 16
SEL_BLOCK = 64
SEL_TOPN = 4
WINDOW = 512
Q_BLOCK = 128
N_QBLK = SEQ // Q_BLOCK
N_SEL = SEQ // SEL_BLOCK
N_SEG = SEQ // CMP_STRIDE
REL_BUCKETS = 32
REL_MAX_DIST = 128
N_BRANCH = 3
BRANCH_W = 512
D_FF = 2816
PLE_DIM = 256
EPS = 1e-6
NEG = -1e30
FORCE_SCORE = 1e4

LANES = 128
VMEM_LIMIT = 48 * 1024 * 1024

F32 = jnp.float32
BF16 = jnp.bfloat16
HP = lax.Precision.HIGHEST

CB_MG = 0
CB_A = 24
CB_B = 40
CB_CQ = 56
CB_CKC, CB_CVC, CB_CKS, CB_CVS, CB_CKW, CB_CVW = 60, 61, 62, 63, 64, 65
CB_SMALL = 66
N_COLS = 68 * LANES
SM_AA, SM_AB, SM_BI, SM_BF, SM_CG = 0, 4, 8, 12, 16


def _mm(a, b, precision=None):
    return lax.dot_general(a, b, (((1,), (0,)), ((), ())), precision=precision,
                           preferred_element_type=F32)


def _mm_nt(a, b, precision=None):
    return lax.dot_general(a, b, (((1,), (1,)), ((), ())), precision=precision,
                           preferred_element_type=F32)


def _mm_tn(a, b, precision=None):
    return lax.dot_general(a, b, (((0,), (0,)), ((), ())), precision=precision,
                           preferred_element_type=F32)


def _bf(x):
    return x.astype(BF16)


def _sigmoid(x):
    return 1.0 / (1.0 + jnp.exp(-x))


def _softplus(x):
    return jnp.maximum(x, 0.0) + jnp.log1p(jnp.exp(-jnp.abs(x)))


def _rms_rows(x, g):
    return x * lax.rsqrt(jnp.mean(x * x, axis=-1, keepdims=True) + EPS) * g


def _params(*sem):
    return pltpu.CompilerParams(dimension_semantics=sem, vmem_limit_bytes=VMEM_LIMIT)


def _proj_kernel(x_ref, g_ref, w_ref, o_ref, u_ref):
    @pl.when(pl.program_id(1) == 0)
    def _():
        u_ref[...] = _bf(_rms_rows(x_ref[...], g_ref[...]))

    o_ref[...] = _mm(u_ref[...], w_ref[...])


def _in_proj(x2, gain, w):
    m, d = x2.shape
    n = w.shape[1]
    tm, tn = 1024, 512
    return pl.pallas_call(
        _proj_kernel,
        grid=(m // tm, n // tn),
        in_specs=[pl.BlockSpec((tm, d), lambda i, j: (i, 0)),
                  pl.BlockSpec((1, d), lambda i, j: (0, 0)),
                  pl.BlockSpec((d, tn), lambda i, j: (0, j))],
        out_specs=pl.BlockSpec((tm, tn), lambda i, j: (i, j)),
        out_shape=jax.ShapeDtypeStruct((m, n), F32),
        scratch_shapes=[pltpu.VMEM((tm, d), BF16)],
        compiler_params=_params("parallel", "arbitrary"),
        name="in_proj",
    )(x2, gain, w)


def _chunk_masks():
    ri = lax.broadcasted_iota(jnp.int32, (CHUNK, CHUNK), 0)
    ci = lax.broadcasted_iota(jnp.int32, (CHUNK, CHUNK), 1)
    return ri, ci


def _chunk_cumsum(x):
    rowi = lax.broadcasted_iota(jnp.int32, x.shape, 0)
    s = 1
    while s < CHUNK:
        x = x + jnp.where(rowi >= s, pltpu.roll(x, s, axis=0), 0.0)
        s *= 2
    return x


def _pick_lane(x, lane_idx):
    lane = lax.broadcasted_iota(jnp.int32, x.shape, 1)
    return jnp.sum(jnp.where(lane == lane_idx, x, 0.0), axis=1, keepdims=True)


def _col_to_row(col, eye):
    return jnp.sum(jnp.where(eye, col, 0.0), axis=0, keepdims=True)


def _bd_pair(pk):
    lo = lax.broadcasted_iota(jnp.int32, pk.shape, 1) < CHUNK
    zero = jnp.zeros_like(pk)
    return jnp.concatenate([jnp.where(lo, pk, zero), jnp.where(lo, zero, pk)], axis=0)


def _bd_wide(x):
    zero = jnp.zeros((x.shape[0], LANES), x.dtype)
    return jnp.concatenate([jnp.concatenate([x[:, :LANES], zero], axis=1),
                            jnp.concatenate([zero, x[:, LANES:]], axis=1)], axis=0)


def _inv_unit_lower(x, bd, eyef):
    xd = jnp.where(bd, x, 0.0)
    xo = jnp.where(bd, 0.0, x)
    mm = lambda a, b: _mm(_bf(a), _bd_pair(_bf(b)))
    x2 = mm(xd, xd)
    x4 = mm(x2, x2)
    x8 = mm(x4, x4)
    p = eyef - xd
    p = p + mm(p, x2)
    p = p + mm(p, x4)
    p = p + mm(p, x8)
    m = mm(p, xo)
    m2 = mm(m, m)
    q = eyef - m
    q = q + mm(q, m2)
    return mm(q, p)


def _gdn_kernel(q_ref, k_ref, v_ref, z_ref, sm_ref, cwq_ref, cwk_ref, cwv_ref, alog_ref, dtb_ref,
                nw_ref, o_ref, qg_s, kd_s, u_s, w_s, at_s, eg_s):
    L = CHUNK
    W = 2 * LANES
    ri = lax.broadcasted_iota(jnp.int32, (L, 2 * L), 0)
    cp = lax.broadcasted_iota(jnp.int32, (L, 2 * L), 1)
    ci = cp & (L - 1)
    first = cp < L
    tril = ri >= ci
    strict = ri > ci
    eye = ri == ci
    bd = (ri >> 4) == (ci >> 4)
    eyef = jnp.where(eye, 1.0, 0.0)
    alog = alog_ref[...]
    dtb = dtb_ref[...]
    nw = nw_ref[...]
    cwq, cwk, cwv = cwq_ref[...], cwk_ref[...], cwv_ref[...]
    ha = pl.program_id(1) * 2

    def conv_silu(ref, w, n, r0):
        cur = ref[pl.ds(r0, L), :]
        p0 = pl.multiple_of(jnp.maximum(r0 - 8, 0), 8)
        prev = jnp.where(n > 0, ref[pl.ds(p0, 8), :], 0.0)
        win = jnp.concatenate([prev, cur], axis=0)
        acc = cur * w[GDN_CONV - 1:GDN_CONV, :]
        for s in range(1, GDN_CONV):
            acc = acc + pltpu.roll(win, s, axis=0)[8:, :] * w[GDN_CONV - 1 - s:GDN_CONV - s, :]
        return acc * _sigmoid(acc)

    def wide(col_a, col_b, rows=L):
        return jnp.concatenate([jnp.broadcast_to(col_a, (rows, LANES)),
                                jnp.broadcast_to(col_b, (rows, LANES))], axis=1)

    def l2n(x):
        xx = x * x
        return x * wide(lax.rsqrt(jnp.sum(xx[:, :LANES], axis=-1, keepdims=True) + 1e-6),
                        lax.rsqrt(jnp.sum(xx[:, LANES:], axis=-1, keepdims=True) + 1e-6))

    def prep(n, carry):
        r0 = pl.multiple_of(n * L, L)
        x = sm_ref[pl.ds(r0, L), :]
        g = _chunk_cumsum(-jnp.exp(alog) * _softplus(x + dtb))
        sx = _sigmoid(x)
        q = l2n(conv_silu(q_ref, cwq, n, r0)) * (D_HEAD_REC ** -0.5)
        k = l2n(conv_silu(k_ref, cwk, n, r0))
        v = conv_silu(v_ref, cwv, n, r0)
        gc_a, gc_b = _pick_lane(g, SM_AA + ha), _pick_lane(g, SM_AA + ha + 1)
        beta = wide(_pick_lane(sx, SM_AB + ha), _pick_lane(sx, SM_AB + ha + 1))
        gc_pk = jnp.where(first, gc_a, gc_b)
        diff = gc_pk - jnp.sum(jnp.where(eye, gc_pk, 0.0), axis=0, keepdims=True)
        decay = jnp.where(tril, jnp.exp(jnp.where(tril, diff, 0.0)), 0.0)
        kb = k * beta
        kq = _mm_nt(_bf(jnp.concatenate([kb, q], axis=0)), _bd_wide(_bf(k)))
        xm = jnp.where(strict, kq[:L] * decay, 0.0)
        t = _bf(_inv_unit_lower(xm, bd, eyef))
        egc = wide(jnp.exp(gc_a), jnp.exp(gc_b))
        gl_a, gl_b = gc_a[L - 1:L, :], gc_b[L - 1:L, :]
        rhs = jnp.concatenate([_bd_wide(_bf(v * beta)), _bd_wide(_bf(kb * egc))], axis=1)
        uw = _mm(t, rhs)
        u_s[pl.ds(r0, L), :] = uw[:, :W]
        w_s[pl.ds(r0, L), :] = _bf(uw[:, W:])
        at_s[pl.ds(r0, L), :] = _bf(kq[L:] * decay)
        qg_s[pl.ds(r0, L), :] = _bf(q * egc)
        kd_s[pl.ds(r0, L), :] = _bf(k * wide(jnp.exp(gl_a - gc_a), jnp.exp(gl_b - gc_b)))
        eg_s[pl.ds(pl.multiple_of(n * 8, 8), 8), :] = wide(jnp.exp(gl_a), jnp.exp(gl_b), 8)
        return carry

    lax.fori_loop(0, SEQ // L, prep, 0, unroll=2)

    zs = jnp.zeros((D_HEAD_REC, D_HEAD_REC), BF16)

    def step(n, states):
        sa, sb = states
        r0 = pl.multiple_of(n * L, L)
        sbd = jnp.concatenate([jnp.concatenate([_bf(sa), zs], axis=1),
                               jnp.concatenate([zs, _bf(sb)], axis=1)], axis=0)
        eg = eg_s[pl.ds(pl.multiple_of(n * 8, 8), 1), :]
        r = _mm(jnp.concatenate([w_s[pl.ds(r0, L), :], qg_s[pl.ds(r0, L), :]], axis=0), sbd)
        vnb = _bf(u_s[pl.ds(r0, L), :] - r[:L])
        o = r[L:] + _mm(at_s[pl.ds(r0, L), :], _bd_wide(vnb))
        upd = _mm_tn(kd_s[pl.ds(r0, L), :], vnb)
        sa = sa * eg[:, :LANES] + upd[:LANES, :LANES]
        sb = sb * eg[:, LANES:] + upd[LANES:, LANES:]
        z = z_ref[pl.ds(r0, L), :]
        on = jnp.concatenate([_rms_rows(o[:, :LANES], nw), _rms_rows(o[:, LANES:], nw)], axis=1)
        o_ref[pl.ds(r0, L), :] = on * (z * _sigmoid(z))
        return sa, sb

    zero = jnp.zeros((D_HEAD_REC, D_HEAD_REC), F32)
    lax.fori_loop(0, SEQ // L, step, (zero, zero))


def _gdn(y3, conv_w, alog_row, dtb_row, norm_w):
    bsz = y3.shape[0]
    H = N_HEADS_REC
    W = 2 * LANES

    def col(off):
        return pl.BlockSpec((None, SEQ, W), lambda b, h, off=off: (b, 0, off // 2 + h))

    def cw(off):
        return pl.BlockSpec((GDN_CONV, W), lambda b, h, off=off: (0, off // 2 + h))

    row = pl.BlockSpec((1, LANES), lambda b, h: (0, 0))
    return pl.pallas_call(
        _gdn_kernel,
        grid=(bsz, H // 2),
        in_specs=[col(CB_A), col(CB_A + H), col(CB_A + 2 * H), col(CB_A + 3 * H),
                  pl.BlockSpec((None, SEQ, LANES), lambda b, h: (b, 0, CB_SMALL)),
                  cw(0), cw(H), cw(2 * H), row, row, row],
        out_specs=pl.BlockSpec((None, SEQ, W), lambda b, h: (b, 0, h)),
        out_shape=jax.ShapeDtypeStruct((bsz, SEQ, H * D_HEAD_REC), F32),
        scratch_shapes=[pltpu.VMEM((SEQ, W), BF16), pltpu.VMEM((SEQ, W), BF16),
                        pltpu.VMEM((SEQ, W), F32), pltpu.VMEM((SEQ, W), BF16),
                        pltpu.VMEM((SEQ, 2 * CHUNK), BF16),
                        pltpu.VMEM((SEQ // CHUNK * 8, W), F32)],
        compiler_params=_params("parallel", "parallel"),
        name="gdn",
    )(y3, y3, y3, y3, y3, conv_w, conv_w, conv_w, alog_row, dtb_row, norm_w)


def _mlstm_kernel(q_ref, k_ref, v_ref, og_ref, sm_ref, bi_ref, bf_ref, nw_ref, o_ref):
    h = pl.program_id(1)
    L = CHUNK
    ri, ci = _chunk_masks()
    tril = ri >= ci
    eye = ri == ci
    bi = bi_ref[...]
    bfr = bf_ref[...]
    nw = nw_ref[...]

    def step(n, carry):
        c_st, n_st, m_st = carry
        r0 = pl.multiple_of(n * L, L)
        q = q_ref[pl.ds(r0, L), :]
        k = k_ref[pl.ds(r0, L), :] * (D_HEAD_REC ** -0.5)
        v = v_ref[pl.ds(r0, L), :]
        qb, kb, vb = _bf(q), _bf(k), _bf(v)
        x = sm_ref[pl.ds(r0, L), :]
        bcum = _pick_lane(_chunk_cumsum(-_softplus(-(x + bfr))), SM_BF + h)
        it = _pick_lane(x + bi, SM_BI + h)
        dm = jnp.where(tril, bcum - _col_to_row(bcum, eye) + _col_to_row(it, eye), NEG)
        dmax = jnp.max(dm, axis=1, keepdims=True)
        qk = _mm_nt(qb, kb)
        a = bcum + m_st
        mt = jnp.maximum(a, dmax)
        sm = jnp.exp(dm - mt) * qk
        si = jnp.exp(a - mt)
        num = si * _mm(qb, _bf(c_st)) + _mm(_bf(sm), vb)
        den = si * jnp.sum(q * n_st, axis=1, keepdims=True) + jnp.sum(sm, axis=1, keepdims=True)
        hh = num / jnp.maximum(jnp.abs(den), jnp.exp(-mt))
        bl = bcum[L - 1:L, :]
        ds = bl - bcum + it
        m_new = jnp.maximum(bl + m_st, jnp.max(ds, axis=0, keepdims=True))
        wk = k * jnp.exp(ds - m_new)
        sc = jnp.exp(bl + m_st - m_new)
        c_st = sc * c_st + _mm_tn(_bf(wk), vb)
        n_st = sc * n_st + jnp.sum(wk, axis=0, keepdims=True)
        o_ref[pl.ds(r0, L), :] = _rms_rows(hh, nw) * _sigmoid(og_ref[pl.ds(r0, L), :])
        return c_st, n_st, m_new

    init = (jnp.zeros((D_HEAD_REC, D_HEAD_REC), F32), jnp.zeros((1, D_HEAD_REC), F32),
            jnp.zeros((1, 1), F32))
    lax.fori_loop(0, SEQ // L, step, init)


def _mlstm(y3, bi_row, bf_row, norm_w):
    bsz = y3.shape[0]
    H = N_HEADS_REC

    def col(off):
        return pl.BlockSpec((None, SEQ, LANES), lambda b, h, off=off: (b, 0, off + h))

    row = pl.BlockSpec((1, LANES), lambda b, h: (0, 0))
    return pl.pallas_call(
        _mlstm_kernel,
        grid=(bsz, H),
        in_specs=[col(CB_B), col(CB_B + H), col(CB_B + 2 * H), col(CB_B + 3 * H),
                  pl.BlockSpec((None, SEQ, LANES), lambda b, h: (b, 0, CB_SMALL)),
                  row, row, row],
        out_specs=pl.BlockSpec((None, SEQ, LANES), lambda b, h: (b, 0, h)),
        out_shape=jax.ShapeDtypeStruct((bsz, SEQ, H * D_HEAD_REC), F32),
        compiler_params=_params("parallel", "parallel"),
        name="mlstm",
    )(y3, y3, y3, y3, y3, bi_row, bf_row, norm_w)


def _nsa_prep_kernel(qt_ref, ks_ref, kw_ref, k2_ref, v2_ref, wkl_ref, wkh_ref, wvl_ref, wvh_ref,
                     pekl_ref, pekh_ref, pevl_ref, pevh_ref, qn_ref, kn0_ref, kn1_ref, kn2_ref,
                     qn_out, ks_out, kw_out, ck_out, cv_out):
    qg = qn_ref[...] * (NSA_DH ** -0.5)
    zq = jnp.zeros((NSA_DH, NSA_HPG * Q_BLOCK), F32)
    zc = jnp.zeros((N_SEG, NSA_DH), F32)
    for g in range(NSA_GROUPS):
        for c in range(N_QBLK):
            x = qt_ref[g, c]
            xn = x * lax.rsqrt(jnp.mean(x * x, axis=0, keepdims=True) + EPS) * qg
            parts = [xn, zq] if g == 0 else [zq, xn]
            qn_out[g, c] = _bf(jnp.concatenate(parts, axis=0))
        k2 = k2_ref[g]
        v2 = v2_ref[g]
        a = _mm(_bf(k2 + pekl_ref[...]), wkl_ref[...])
        bh = _mm(_bf(k2 + pekh_ref[...]), wkh_ref[...])
        ck = _rms_rows(a + pltpu.roll(bh, N_SEG - 1, axis=0), kn0_ref[...])
        parts = [ck, zc] if g == 0 else [zc, ck]
        ck_out[g] = _bf(jnp.concatenate(parts, axis=1))
        at = _mm_nt(wvl_ref[...], _bf(v2 + pevl_ref[...]))
        bt = _mm_nt(wvh_ref[...], _bf(v2 + pevh_ref[...]))
        cv_out[g] = _bf(at + pltpu.roll(bt, N_SEG - 1, axis=1))

    lane = lax.broadcasted_iota(jnp.int32, (Q_BLOCK, LANES), 1)
    lo = lane < NSA_DH

    def norm_keys(src, dst, gain):
        def body(i, carry):
            r0 = pl.multiple_of(i * Q_BLOCK, Q_BLOCK)
            x = src[pl.ds(r0, Q_BLOCK), :]
            xx = x * x
            s0 = jnp.sum(jnp.where(lo, xx, 0.0), axis=1, keepdims=True)
            s1 = jnp.sum(jnp.where(lo, 0.0, xx), axis=1, keepdims=True)
            ms = jnp.where(lo, s0, s1) * (1.0 / NSA_DH)
            dst[pl.ds(r0, Q_BLOCK), :] = _bf(x * lax.rsqrt(ms + EPS) * gain)
            return carry
        lax.fori_loop(0, N_QBLK, body, 0)

    norm_keys(ks_ref, ks_out, kn1_ref[...])
    norm_keys(kw_ref, kw_out, kn2_ref[...])


def _nsa_prep(y3, qt, k2, v2, wkl, wkh, wvl, wvh, pekl, pekh, pevl, pevh, qn, kn0, kn1, kn2):
    bsz = y3.shape[0]
    G = NSA_GROUPS
    W4 = NSA_HPG * Q_BLOCK

    def full(a):
        nd = a.ndim
        return pl.BlockSpec(a.shape, lambda b, nd=nd: (0,) * nd)

    def per_b(shape):
        nd = len(shape)
        return pl.BlockSpec((None,) + shape, lambda b, nd=nd: (b,) + (0,) * nd)

    consts = [wkl, wkh, wvl, wvh, pekl, pekh, pevl, pevh, qn, kn0, kn1, kn2]
    return pl.pallas_call(
        _nsa_prep_kernel,
        grid=(bsz,),
        in_specs=[per_b((G, N_QBLK, NSA_DH, W4)),
                  pl.BlockSpec((None, SEQ, LANES), lambda b: (b, 0, CB_CKS)),
                  pl.BlockSpec((None, SEQ, LANES), lambda b: (b, 0, CB_CKW)),
                  per_b((G, N_SEG, CMP_STRIDE * NSA_DH)), per_b((G, N_SEG, CMP_STRIDE * NSA_DH))]
                 + [full(a) for a in consts],
        out_specs=[per_b((G, N_QBLK, 2 * NSA_DH, W4)), per_b((SEQ, LANES)), per_b((SEQ, LANES)),
                   per_b((G, N_SEG, LANES)), per_b((G, NSA_DH, N_SEG))],
        out_shape=[jax.ShapeDtypeStruct((bsz, G, N_QBLK, 2 * NSA_DH, W4), BF16),
                   jax.ShapeDtypeStruct((bsz, SEQ, LANES), BF16),
                   jax.ShapeDtypeStruct((bsz, SEQ, LANES), BF16),
                   jax.ShapeDtypeStruct((bsz, G, N_SEG, LANES), BF16),
                   jax.ShapeDtypeStruct((bsz, G, NSA_DH, N_SEG), BF16)],
        compiler_params=_params("parallel"),
        name="nsa_prep",
    )(qt, y3, y3, k2, v2, *consts)


def _nsa_kernel(q_ref, ck_ref, cv_ref, ks_ref, vs_ref, kw_ref, vw_ref, bc_ref, bt_ref, b31_ref,
                cg_ref, o_ref, sel_s, m_s, l_s, acc_s, m_w, l_w, acc_w):
    c = pl.program_id(2)
    W4 = NSA_HPG * Q_BLOCK
    q = q_ref[...]
    kpos = lax.broadcasted_iota(jnp.int32, (Q_BLOCK, W4), 0)
    qpos = lax.broadcasted_iota(jnp.int32, (Q_BLOCK, W4), 1) & (Q_BLOCK - 1)

    s = _mm(ck_ref[...], q) + bc_ref[...]
    valid = (CMP_STRIDE * kpos + (CMP_BLOCK - 1)) <= (c * Q_BLOCK + qpos)
    s = jnp.where(valid, s, NEG)
    e = jnp.where(valid, jnp.exp(s - jnp.max(s, axis=0, keepdims=True)), 0.0)
    p = e / jnp.maximum(jnp.sum(e, axis=0, keepdims=True), 1e-30)
    o_cmp = _mm(cv_ref[...], _bf(p))

    psum = (p[:, 0:Q_BLOCK] + p[:, Q_BLOCK:2 * Q_BLOCK] + p[:, 2 * Q_BLOCK:3 * Q_BLOCK]
            + p[:, 3 * Q_BLOCK:4 * Q_BLOCK])
    jj = lax.broadcasted_iota(jnp.int32, (N_SEL, N_SEG), 0)
    nn = lax.broadcasted_iota(jnp.int32, (N_SEL, N_SEG), 1)
    ratio = SEL_BLOCK // CMP_STRIDE
    ov = jnp.where((nn >= ratio * jj - 1) & (nn <= ratio * jj + ratio - 1) & (nn < N_SEG - 1), 1.0, 0.0)
    imp = _mm(ov, psum, HP)
    jb = lax.broadcasted_iota(jnp.int32, (N_SEL, Q_BLOCK), 0)
    tq = c * Q_BLOCK + lax.broadcasted_iota(jnp.int32, (N_SEL, Q_BLOCK), 1)
    cur = tq >> 6
    imp = jnp.where((jb == 0) | (jb == cur), FORCE_SCORE, jnp.where(jb <= cur, imp, -1.0))
    rank = jnp.zeros((N_SEL, Q_BLOCK), F32)
    for jp in range(N_SEL):
        rowv = imp[jp:jp + 1, :]
        rank = rank + jnp.where(rowv > imp, 1.0, 0.0)
        if jp < N_SEL - 1:
            rank = rank + jnp.where(jb > jp, jnp.where(rowv == imp, 1.0, 0.0), 0.0)
    sel_s[...] = jnp.where(rank < SEL_TOPN, 1.0, 0.0)

    b31 = b31_ref[...]

    def attend(kt, vt, bias, valid, m_ref, l_ref, acc_ref):
        sc = jnp.where(valid, _mm(kt, q) + bias, NEG)
        m_old = m_ref[...]
        m_new = jnp.maximum(m_old, jnp.max(sc, axis=0, keepdims=True))
        alpha = jnp.exp(m_old - m_new)
        pe = jnp.where(valid, jnp.exp(sc - m_new), 0.0)
        m_ref[...] = m_new
        l_ref[...] = alpha * l_ref[...] + jnp.sum(pe, axis=0, keepdims=True)
        acc_ref[...] = alpha * acc_ref[...] + _mm(vt, _bf(pe))

    def reset(m_ref, l_ref, acc_ref):
        m_ref[...] = jnp.full(m_ref.shape, NEG, F32)
        l_ref[...] = jnp.zeros(l_ref.shape, F32)
        acc_ref[...] = jnp.zeros(acc_ref.shape, F32)

    reset(m_s, l_s, acc_s)
    half = SEL_BLOCK

    def sel_mask(jt):
        r0 = jnp.broadcast_to(sel_s[pl.ds(2 * jt, 1), :], (half, Q_BLOCK))
        r1 = jnp.broadcast_to(sel_s[pl.ds(2 * jt + 1, 1), :], (half, Q_BLOCK))
        m1 = jnp.concatenate([r0, r1], axis=0)
        return jnp.concatenate([m1] * NSA_HPG, axis=1) > 0.5

    def far_tile(jt, carry):
        attend(ks_ref[jt], vs_ref[jt], b31, sel_mask(jt), m_s, l_s, acc_s)
        return carry

    lax.fori_loop(0, jnp.maximum(c - 1, 0), far_tile, 0)

    @pl.when(c >= 1)
    def _():
        attend(ks_ref[c - 1], vs_ref[c - 1], bt_ref[1], sel_mask(c - 1), m_s, l_s, acc_s)

    attend(ks_ref[c], vs_ref[c], bt_ref[0], sel_mask(c) & (kpos <= qpos), m_s, l_s, acc_s)
    o_sel = acc_s[...] / jnp.maximum(l_s[...], 1e-30)

    reset(m_w, l_w, acc_w)
    n_wt = WINDOW // Q_BLOCK
    for d in range(n_wt, -1, -1):
        if d == n_wt:
            valid = kpos > qpos
        elif d == 0:
            valid = kpos <= qpos
        else:
            valid = kpos >= 0
        bias = bt_ref[d] if d < 2 else b31

        @pl.when(c >= d)
        def _(d=d, valid=valid, bias=bias):
            attend(kw_ref[c - d], vw_ref[c - d], bias, valid, m_w, l_w, acc_w)

    o_win = acc_w[...] / jnp.maximum(l_w[...], 1e-30)

    gates = _sigmoid(cg_ref[...])

    def gate_row(r):
        return jnp.concatenate([gates[k * 3 + r:k * 3 + r + 1, :] for k in range(NSA_HPG)], axis=1)

    o_ref[...] = gate_row(0) * o_cmp + gate_row(1) * o_sel + gate_row(2) * o_win


def _nsa_attend(qn, ck, cv, ksn, vst, kwn, vwt, bias_c, bias_t, bias_far, cgt):
    bsz = qn.shape[0]
    G = NSA_GROUPS
    W4 = NSA_HPG * Q_BLOCK
    row_scr = pltpu.VMEM((1, W4), F32)
    acc_scr = pltpu.VMEM((NSA_DH, W4), F32)
    return pl.pallas_call(
        _nsa_kernel,
        grid=(bsz, G, N_QBLK),
        in_specs=[
            pl.BlockSpec((None, None, None, 2 * NSA_DH, W4), lambda b, g, c: (b, g, c, 0, 0)),
            pl.BlockSpec((None, None, N_SEG, LANES), lambda b, g, c: (b, g, 0, 0)),
            pl.BlockSpec((None, None, NSA_DH, N_SEG), lambda b, g, c: (b, g, 0, 0)),
            pl.BlockSpec((None, N_QBLK, Q_BLOCK, LANES), lambda b, g, c: (b, 0, 0, 0)),
            pl.BlockSpec((None, None, N_QBLK, NSA_DH, Q_BLOCK), lambda b, g, c: (b, g, 0, 0, 0)),
            pl.BlockSpec((None, N_QBLK, Q_BLOCK, LANES), lambda b, g, c: (b, 0, 0, 0)),
            pl.BlockSpec((None, None, N_QBLK, NSA_DH, Q_BLOCK), lambda b, g, c: (b, g, 0, 0, 0)),
            pl.BlockSpec((None, None, N_SEG, W4), lambda b, g, c: (g, c, 0, 0)),
            pl.BlockSpec((None, 2, Q_BLOCK, W4), lambda b, g, c: (g, 0, 0, 0)),
            pl.BlockSpec((None, 1, W4), lambda b, g, c: (g, 0, 0)),
            pl.BlockSpec((None, None, NSA_HPG * 3, Q_BLOCK), lambda b, g, c: (b, g, 0, c)),
        ],
        out_specs=pl.BlockSpec((None, None, None, NSA_DH, W4), lambda b, g, c: (b, g, c, 0, 0)),
        out_shape=jax.ShapeDtypeStruct((bsz, G, N_QBLK, NSA_DH, W4), F32),
        scratch_shapes=[pltpu.VMEM((N_SEL, Q_BLOCK), F32), row_scr, row_scr, acc_scr,
                        row_scr, row_scr, acc_scr],
        compiler_params=_params("parallel", "parallel", "arbitrary"),
        name="nsa_attend",
    )(qn, ck, cv, ksn, vst, kwn, vwt, bias_c, bias_t, bias_far, cgt)


def _bucket(dist):
    max_exact = REL_BUCKETS // 2
    d = jnp.maximum(dist, 0)
    df = jnp.maximum(d, 1).astype(F32)
    large = max_exact + (jnp.log(df / max_exact) / math.log(REL_MAX_DIST / max_exact)
                         * (REL_BUCKETS - max_exact)).astype(jnp.int32)
    large = jnp.minimum(large, REL_BUCKETS - 1)
    return jnp.where(d < max_exact, d, large)


def _bias_tables(rel_bias):
    G, K = NSA_GROUPS, NSA_HPG
    t = np.arange(SEQ)
    cmp_end = np.arange(N_SEG) * CMP_STRIDE + CMP_BLOCK - 1
    dist_c = jnp.asarray((t[:, None] - cmp_end[None]).astype(np.int32))
    bc = rel_bias[_bucket(dist_c)].reshape(N_QBLK, Q_BLOCK, N_SEG, G, K)
    bc = bc.transpose(3, 0, 2, 4, 1).reshape(G, N_QBLK, N_SEG, K * Q_BLOCK)
    i = np.arange(Q_BLOCK)
    dist_t = np.arange(2)[:, None, None] * Q_BLOCK + i[None, None, :] - i[None, :, None]
    bt = rel_bias[_bucket(jnp.asarray(dist_t.astype(np.int32)))].reshape(2, Q_BLOCK, Q_BLOCK, G, K)
    bt = bt.transpose(3, 0, 1, 4, 2).reshape(G, 2, Q_BLOCK, K * Q_BLOCK)
    far = jnp.repeat(rel_bias[REL_BUCKETS - 1].reshape(G, K), Q_BLOCK, axis=1).reshape(G, 1, K * Q_BLOCK)
    return bc, bt, far


def _nsa(y3, small, q_norm, k_norm, cmp_pe, w_cmp, tables):
    bsz = y3.shape[0]
    G, K, DH = NSA_GROUPS, NSA_HPG, NSA_DH

    def cols(cb, n=1):
        return y3[:, :, cb * LANES:(cb + n) * LANES]

    qt = cols(CB_CQ, 4).reshape(bsz, N_QBLK, Q_BLOCK, G, K, DH)
    qt = qt.transpose(0, 3, 1, 5, 4, 2).reshape(bsz, G, N_QBLK, DH, K * Q_BLOCK)

    def segs(x):
        return x.reshape(bsz, N_SEG, CMP_STRIDE, G, DH).transpose(0, 3, 1, 2, 4).reshape(
            bsz, G, N_SEG, CMP_STRIDE * DH)

    def tiles_t(x):
        return _bf(x.reshape(bsz, N_QBLK, Q_BLOCK, G, DH).transpose(0, 3, 1, 4, 2))

    wk = w_cmp[0].reshape(2, CMP_STRIDE * DH, DH)
    wv = w_cmp[1].reshape(2, CMP_STRIDE * DH, DH)
    pe = cmp_pe.reshape(2, 2, 1, CMP_STRIDE * DH)
    qn, ksn, kwn, ck, cv = _nsa_prep(
        y3, qt, segs(cols(CB_CKC)), segs(cols(CB_CVC)),
        _bf(wk[0]), _bf(wk[1]), _bf(wv[0].T), _bf(wv[1].T),
        pe[0, 0], pe[0, 1], pe[1, 0], pe[1, 1],
        q_norm.reshape(DH, 1), k_norm[0].reshape(1, DH),
        jnp.tile(k_norm[1], G).reshape(1, LANES), jnp.tile(k_norm[2], G).reshape(1, LANES))
    cgt = small[:, :, SM_CG:SM_CG + NSA_HEADS * 3].reshape(bsz, SEQ, G, K * 3).transpose(0, 2, 3, 1)
    bc, bt, far = tables
    ot = _nsa_attend(qn, ck, cv, ksn.reshape(bsz, N_QBLK, Q_BLOCK, LANES), tiles_t(cols(CB_CVS)),
                     kwn.reshape(bsz, N_QBLK, Q_BLOCK, LANES), tiles_t(cols(CB_CVW)), bc, bt, far, cgt)
    ot = ot.reshape(bsz, G, N_QBLK, DH, K, Q_BLOCK).transpose(0, 2, 5, 1, 4, 3)
    return ot.reshape(bsz, SEQ, NSA_HEADS * DH)


def _merge_kernel(h_ref, ya_ref, yb_ref, yc_ref, ga_ref, gb_ref, gc_ref, wb_ref, wo_ref, o_ref):
    acc = None
    for n, (y_ref, g_ref) in enumerate(((ya_ref, ga_ref), (yb_ref, gb_ref), (yc_ref, gc_ref))):
        t = _sigmoid(g_ref[...]) * _mm(_bf(y_ref[...]), wb_ref[n])
        acc = t if acc is None else acc + t
    o_ref[...] = h_ref[...] + _mm(_bf(acc), wo_ref[...])


def _merge(h2, ya, yb, yc, y2, wb, wo):
    m, d = h2.shape
    tm = 512
    row = lambda w: pl.BlockSpec((tm, w), lambda i: (i, 0))
    gate = lambda n: pl.BlockSpec((tm, d), lambda i, n=n: (i, n))
    return pl.pallas_call(
        _merge_kernel,
        grid=(m // tm,),
        in_specs=[row(d), row(BRANCH_W), row(BRANCH_W), row(BRANCH_W), gate(0), gate(1), gate(2),
                  pl.BlockSpec((N_BRANCH, BRANCH_W, d), lambda i: (0, 0, 0)),
                  pl.BlockSpec((d, d), lambda i: (0, 0))],
        out_specs=row(d),
        out_shape=jax.ShapeDtypeStruct((m, d), F32),
        compiler_params=_params("parallel"),
        name="merge",
    )(h2, ya, yb, yc, y2, y2, y2, wb, wo)


def _ffn_kernel(h_ref, g_ref, wg_ref, wu_ref, wo_ref, o_ref, u_ref, acc_ref):
    f = pl.program_id(1)

    @pl.when(f == 0)
    def _():
        x = h_ref[...]
        u_ref[...] = _bf(_rms_rows(x, g_ref[...]))
        acc_ref[...] = x

    u = u_ref[...]
    gt = _mm(u, wg_ref[...])
    up = _mm(u, wu_ref[...])
    acc_ref[...] += _mm(_bf(gt * _sigmoid(gt) * up), wo_ref[...])

    @pl.when(f == pl.num_programs(1) - 1)
    def _():
        o_ref[...] = acc_ref[...]


def _ffn(h2, gain, w_in, w_out):
    m, d = h2.shape
    tm, nf = 512, 2
    tf = D_FF // nf
    return pl.pallas_call(
        _ffn_kernel,
        grid=(m // tm, nf),
        in_specs=[pl.BlockSpec((tm, d), lambda i, f: (i, 0)),
                  pl.BlockSpec((1, d), lambda i, f: (0, 0)),
                  pl.BlockSpec((d, tf), lambda i, f: (0, f)),
                  pl.BlockSpec((d, tf), lambda i, f: (0, f + nf)),
                  pl.BlockSpec((tf, d), lambda i, f: (f, 0))],
        out_specs=pl.BlockSpec((tm, d), lambda i, f: (i, 0)),
        out_shape=jax.ShapeDtypeStruct((m, d), F32),
        scratch_shapes=[pltpu.VMEM((tm, d), BF16), pltpu.VMEM((tm, d), F32)],
        compiler_params=_params("parallel", "arbitrary"),
        name="ffn",
    )(h2, gain, w_in, w_in, w_out)


def _ple_kernel(h_ref, g_ref, p_ref, wg_ref, wp_ref, o_ref):
    x = h_ref[...]
    gate = _sigmoid(_mm(_bf(_rms_rows(x, g_ref[...])), wg_ref[...]))
    o_ref[...] = x + gate * _mm(_bf(p_ref[...]), wp_ref[...])


def _ple(h2, gain, p2, wg, wp):
    m, d = h2.shape
    tm = 512
    return pl.pallas_call(
        _ple_kernel,
        grid=(m // tm,),
        in_specs=[pl.BlockSpec((tm, d), lambda i: (i, 0)),
                  pl.BlockSpec((1, d), lambda i: (0, 0)),
                  pl.BlockSpec((tm, PLE_DIM), lambda i: (i, 0)),
                  pl.BlockSpec((d, d), lambda i: (0, 0)),
                  pl.BlockSpec((PLE_DIM, d), lambda i: (0, 0))],
        out_specs=pl.BlockSpec((tm, d), lambda i: (i, 0)),
        out_shape=jax.ShapeDtypeStruct((m, d), F32),
        compiler_params=_params("parallel"),
        name="ple",
    )(h2, gain, p2, wg, wp)


def _reorder_w_in(w):
    hd = N_HEADS_REC * D_HEAD_REC
    sizes = [hd, hd, hd, hd, N_HEADS_REC, N_HEADS_REC, hd, hd, hd, hd, N_HEADS_REC, N_HEADS_REC,
             NSA_HEADS * NSA_DH] + [NSA_GROUPS * NSA_DH] * 6 + [NSA_HEADS * 3, N_BRANCH * D_MODEL]
    offs = np.concatenate([[0], np.cumsum(sizes)])
    seg = [w[:, offs[i]:offs[i + 1]] for i in range(len(sizes))]
    (aq, ak, av, az, aa, ab, bq, bk, bv, bo, bi, bf, cq, ckc, cvc, cks, cvs, ckw, cvw, cg, mg) = seg
    small = jnp.concatenate([aa, ab, bi, bf, cg], axis=1)
    main = jnp.concatenate([mg, aq, ak, av, az, bq, bk, bv, bo, cq, ckc, cvc, cks, cvs, ckw, cvw, small],
                           axis=1)
    return _bf(jnp.pad(main, ((0, 0), (0, N_COLS - main.shape[1]))))


def _lane_row(vals, offset):
    return jnp.zeros((1, LANES), F32).at[0, offset:offset + vals.shape[0]].set(vals)


def kernel(x, p, rel_bias, norm_mix, w_in, conv_w, gdn_a_log, gdn_dt_bias, gdn_norm, mlstm_b_i, mlstm_b_f,
           mlstm_norm, nsa_q_norm, nsa_k_norm, nsa_cmp_pe, nsa_w_cmp, w_branch, w_out, norm_ffn, w_ffn_in,
           w_ffn_out, norm_ple, w_ple_gate, w_ple_proj):
    bsz, seq, d = x.shape
    assert seq == SEQ and d == D_MODEL
    depth = w_in.shape[0]
    m = bsz * seq
    tables = _bias_tables(rel_bias)
    h2 = x.reshape(m, d)
    for l in range(depth):
        y2 = _in_proj(h2, norm_mix[l].reshape(1, d), _reorder_w_in(w_in[l]))
        y3 = y2.reshape(bsz, seq, N_COLS)
        small = y3[:, :, CB_SMALL * LANES:(CB_SMALL + 1) * LANES]
        ya = _gdn(y3, conv_w[l], _lane_row(gdn_a_log[l], SM_AA), _lane_row(gdn_dt_bias[l], SM_AA),
                  gdn_norm[l].reshape(1, D_HEAD_REC))
        yb = _mlstm(y3, _lane_row(mlstm_b_i[l], SM_BI), _lane_row(mlstm_b_f[l], SM_BF),
                    mlstm_norm[l].reshape(1, D_HEAD_REC))
        yc = _nsa(y3, small, nsa_q_norm[l], nsa_k_norm[l], nsa_cmp_pe[l], nsa_w_cmp[l], tables)
        h2 = _merge(h2, ya.reshape(m, BRANCH_W), yb.reshape(m, BRANCH_W), yc.reshape(m, BRANCH_W), y2,
                    _bf(w_branch[l]), _bf(w_out[l]))
        h2 = _ffn(h2, norm_ffn[l].reshape(1, d), _bf(w_ffn_in[l]), _bf(w_ffn_out[l]))
        h2 = _ple(h2, norm_ple[l].reshape(1, d), p[l].reshape(m, PLE_DIM), _bf(w_ple_gate[l]),
                  _bf(w_ple_proj[l]))
    return h2.reshape(bsz, seq, d)
```

```python
import functools
import math

import numpy as np
import jax
import jax.numpy as jnp
from jax import lax
from jax.experimental import pallas as pl
from jax.experimental.pallas import tpu as pltpu

D_MODEL = 1024
SEQ = 2048
N_HEADS_REC = 4
D_HEAD_REC = 128
CHUNK = 64
GDN_HPS = 2
GDN_CONV = 4
NSA_HEADS = 8
NSA_GROUPS = 2
NSA_HPG = NSA_HEADS // NSA_GROUPS
NSA_DH = 64
CMP_BLOCK = 32
CMP_STRIDE = 16
SEL_BLOCK = 64
SEL_TOPN = 4
WINDOW = 512
Q_BLOCK = 128
N_QBLK = SEQ // Q_BLOCK
N_SEL = SEQ // SEL_BLOCK
N_SEG = SEQ // CMP_STRIDE
SEL_PAD = Q_BLOCK
FAR_TILE = 4 * Q_BLOCK
CMP_NEAR = 24
REL_BUCKETS = 32
REL_MAX_DIST = 128
N_BRANCH = 3
BRANCH_W = 512
D_FF = 2816
PLE_DIM = 256
EPS = 1e-6
NEG = -1e30
FORCE_SCORE = 1e4

LANES = 128
VMEM_LIMIT = 48 * 1024 * 1024

F32 = jnp.float32
BF16 = jnp.bfloat16
HP = lax.Precision.HIGHEST

CB_MG = 0
CB_A = 24
CB_B = 40
CB_CQ = 56
CB_CKC, CB_CVC, CB_CKS, CB_CVS, CB_CKW, CB_CVW = 60, 61, 62, 63, 64, 65
CB_SMALL = 66
N_COLS = 68 * LANES
SM_AA, SM_AB, SM_BI, SM_BF, SM_CG = 0, 4, 8, 12, 16


def _mm(a, b, precision=None):
    return lax.dot_general(a, b, (((1,), (0,)), ((), ())), precision=precision,
                           preferred_element_type=F32)


def _mm_nt(a, b, precision=None):
    return lax.dot_general(a, b, (((1,), (1,)), ((), ())), precision=precision,
                           preferred_element_type=F32)


def _mm_tn(a, b, precision=None):
    return lax.dot_general(a, b, (((0,), (0,)), ((), ())), precision=precision,
                           preferred_element_type=F32)


def _bf(x):
    return x.astype(BF16)


def _sigmoid(x):
    return 1.0 / (1.0 + jnp.exp(-x))


def _softplus(x):
    return jnp.maximum(x, 0.0) + jnp.log1p(jnp.exp(-jnp.abs(x)))


def _rms_rows(x, g):
    return x * lax.rsqrt(jnp.mean(x * x, axis=-1, keepdims=True) + EPS) * g


def _params(*sem):
    return pltpu.CompilerParams(dimension_semantics=sem, vmem_limit_bytes=VMEM_LIMIT)


def _proj_kernel(x_ref, g_ref, w_ref, o_ref, u_ref):
    @pl.when(pl.program_id(1) == 0)
    def _():
        u_ref[...] = _bf(_rms_rows(x_ref[...], g_ref[...]))

    o_ref[...] = _mm(u_ref[...], w_ref[...])


def _in_proj(x2, gain, w):
    m, d = x2.shape
    n = w.shape[1]
    tm, tn = 1024, 512
    return pl.pallas_call(
        _proj_kernel,
        grid=(m // tm, n // tn),
        in_specs=[pl.BlockSpec((tm, d), lambda i, j: (i, 0)),
                  pl.BlockSpec((1, d), lambda i, j: (0, 0)),
                  pl.BlockSpec((d, tn), lambda i, j: (0, j))],
        out_specs=pl.BlockSpec((tm, tn), lambda i, j: (i, j)),
        out_shape=jax.ShapeDtypeStruct((m, n), F32),
        scratch_shapes=[pltpu.VMEM((tm, d), BF16)],
        compiler_params=_params("parallel", "arbitrary"),
        name="in_proj",
    )(x2, gain, w)


def _chunk_masks():
    ri = lax.broadcasted_iota(jnp.int32, (CHUNK, CHUNK), 0)
    ci = lax.broadcasted_iota(jnp.int32, (CHUNK, CHUNK), 1)
    return ri, ci


def _chunk_cumsum(x):
    rowi = lax.broadcasted_iota(jnp.int32, x.shape, 0)
    s = 1
    while s < CHUNK:
        x = x + jnp.where(rowi >= s, pltpu.roll(x, s, axis=0), 0.0)
        s *= 2
    return x


def _pick_lane(x, lane_idx):
    lane = lax.broadcasted_iota(jnp.int32, x.shape, 1)
    return jnp.sum(jnp.where(lane == lane_idx, x, 0.0), axis=1, keepdims=True)


def _col_to_row(col, eye):
    return jnp.sum(jnp.where(eye, col, 0.0), axis=0, keepdims=True)


def _bd_pair(pk):
    lo = lax.broadcasted_iota(jnp.int32, pk.shape, 1) < CHUNK
    zero = jnp.zeros_like(pk)
    return jnp.concatenate([jnp.where(lo, pk, zero), jnp.where(lo, zero, pk)], axis=0)


def _bd_wide(x):
    zero = jnp.zeros((x.shape[0], LANES), x.dtype)
    return jnp.concatenate([jnp.concatenate([x[:, :LANES], zero], axis=1),
                            jnp.concatenate([zero, x[:, LANES:]], axis=1)], axis=0)


def _inv_unit_lower(x, bd, eyef):
    xd = jnp.where(bd, x, 0.0)
    xo = jnp.where(bd, 0.0, x)
    mm = lambda a, b: _mm(_bf(a), _bd_pair(_bf(b)))
    x2 = mm(xd, xd)
    x4 = mm(x2, x2)
    x8 = mm(x4, x4)
    p = eyef - xd
    p = p + mm(p, x2)
    p = p + mm(p, x4)
    p = p + mm(p, x8)
    m = mm(p, xo)
    m2 = mm(m, m)
    q = eyef - m
    q = q + mm(q, m2)
    return mm(q, p)


def _gdn_kernel(q_ref, k_ref, v_ref, z_ref, sm_ref, cwq_ref, cwk_ref, cwv_ref, alog_ref, dtb_ref,
                nw_ref, o_ref, qg_s, kd_s, u_s, w_s, at_s, eg_s):
    L = CHUNK
    W = 2 * LANES
    ri = lax.broadcasted_iota(jnp.int32, (L, 2 * L), 0)
    cp = lax.broadcasted_iota(jnp.int32, (L, 2 * L), 1)
    ci = cp & (L - 1)
    first = cp < L
    tril = ri >= ci
    strict = ri > ci
    eye = ri == ci
    bd = (ri >> 4) == (ci >> 4)
    eyef = jnp.where(eye, 1.0, 0.0)
    alog = alog_ref[...]
    dtb = dtb_ref[...]
    nw = nw_ref[...]
    cwq, cwk, cwv = cwq_ref[...], cwk_ref[...], cwv_ref[...]
    ha = pl.program_id(1) * 2

    def conv_silu(ref, w, n, r0):
        cur = ref[pl.ds(r0, L), :]
        p0 = pl.multiple_of(jnp.maximum(r0 - 8, 0), 8)
        prev = jnp.where(n > 0, ref[pl.ds(p0, 8), :], 0.0)
        win = jnp.concatenate([prev, cur], axis=0)
        acc = cur * w[GDN_CONV - 1:GDN_CONV, :]
        for s in range(1, GDN_CONV):
            acc = acc + pltpu.roll(win, s, axis=0)[8:, :] * w[GDN_CONV - 1 - s:GDN_CONV - s, :]
        return acc * _sigmoid(acc)

    def wide(col_a, col_b, rows=L):
        return jnp.concatenate([jnp.broadcast_to(col_a, (rows, LANES)),
                                jnp.broadcast_to(col_b, (rows, LANES))], axis=1)

    def l2n(x):
        xx = x * x
        return x * wide(lax.rsqrt(jnp.sum(xx[:, :LANES], axis=-1, keepdims=True) + 1e-6),
                        lax.rsqrt(jnp.sum(xx[:, LANES:], axis=-1, keepdims=True) + 1e-6))

    def prep(n, carry):
        r0 = pl.multiple_of(n * L, L)
        x = sm_ref[pl.ds(r0, L), :]
        g = _chunk_cumsum(-jnp.exp(alog) * _softplus(x + dtb))
        sx = _sigmoid(x)
        q = l2n(conv_silu(q_ref, cwq, n, r0)) * (D_HEAD_REC ** -0.5)
        k = l2n(conv_silu(k_ref, cwk, n, r0))
        v = conv_silu(v_ref, cwv, n, r0)
        gc_a, gc_b = _pick_lane(g, SM_AA + ha), _pick_lane(g, SM_AA + ha + 1)
        beta = wide(_pick_lane(sx, SM_AB + ha), _pick_lane(sx, SM_AB + ha + 1))
        gc_pk = jnp.where(first, gc_a, gc_b)
        diff = gc_pk - jnp.sum(jnp.where(eye, gc_pk, 0.0), axis=0, keepdims=True)
        decay = jnp.where(tril, jnp.exp(jnp.where(tril, diff, 0.0)), 0.0)
        kb = k * beta
        kq = _mm_nt(_bf(jnp.concatenate([kb, q], axis=0)), _bd_wide(_bf(k)))
        xm = jnp.where(strict, kq[:L] * decay, 0.0)
        t = _bf(_inv_unit_lower(xm, bd, eyef))
        egc = wide(jnp.exp(gc_a), jnp.exp(gc_b))
        gl_a, gl_b = gc_a[L - 1:L, :], gc_b[L - 1:L, :]
        rhs = jnp.concatenate([_bd_wide(_bf(v * beta)), _bd_wide(_bf(kb * egc))], axis=1)
        uw = _mm(t, rhs)
        u_s[pl.ds(r0, L), :] = uw[:, :W]
        w_s[pl.ds(r0, L), :] = _bf(uw[:, W:])
        at_s[pl.ds(r0, L), :] = _bf(kq[L:] * decay)
        qg_s[pl.ds(r0, L), :] = _bf(q * egc)
        kd_s[pl.ds(r0, L), :] = _bf(k * wide(jnp.exp(gl_a - gc_a), jnp.exp(gl_b - gc_b)))
        eg_s[pl.ds(pl.multiple_of(n * 8, 8), 8), :] = wide(jnp.exp(gl_a), jnp.exp(gl_b), 8)
        return carry

    lax.fori_loop(0, SEQ // L, prep, 0, unroll=2)

    zs = jnp.zeros((D_HEAD_REC, D_HEAD_REC), BF16)

    def step(n, states):
        sa, sb = states
        r0 = pl.multiple_of(n * L, L)
        sbd = jnp.concatenate([jnp.concatenate([_bf(sa), zs], axis=1),
                               jnp.concatenate([zs, _bf(sb)], axis=1)], axis=0)
        eg = eg_s[pl.ds(pl.multiple_of(n * 8, 8), 1), :]
        r = _mm(jnp.concatenate([w_s[pl.ds(r0, L), :], qg_s[pl.ds(r0, L), :]], axis=0), sbd)
        vnb = _bf(u_s[pl.ds(r0, L), :] - r[:L])
        o = r[L:] + _mm(at_s[pl.ds(r0, L), :], _bd_wide(vnb))
        upd = _mm_tn(kd_s[pl.ds(r0, L), :], vnb)
        sa = sa * eg[:, :LANES] + upd[:LANES, :LANES]
        sb = sb * eg[:, LANES:] + upd[LANES:, LANES:]
        z = z_ref[pl.ds(r0, L), :]
        on = jnp.concatenate([_rms_rows(o[:, :LANES], nw), _rms_rows(o[:, LANES:], nw)], axis=1)
        o_ref[pl.ds(r0, L), :] = on * (z * _sigmoid(z))
        return sa, sb

    zero = jnp.zeros((D_HEAD_REC, D_HEAD_REC), F32)
    lax.fori_loop(0, SEQ // L, step, (zero, zero))


def _gdn(y3, conv_w, alog_row, dtb_row, norm_w):
    bsz = y3.shape[0]
    H = N_HEADS_REC
    W = 2 * LANES

    def col(off):
        return pl.BlockSpec((None, SEQ, W), lambda b, h, off=off: (b, 0, off // 2 + h))

    def cw(off):
        return pl.BlockSpec((GDN_CONV, W), lambda b, h, off=off: (0, off // 2 + h))

    row = pl.BlockSpec((1, LANES), lambda b, h: (0, 0))
    return pl.pallas_call(
        _gdn_kernel,
        grid=(bsz, H // 2),
        in_specs=[col(CB_A), col(CB_A + H), col(CB_A + 2 * H), col(CB_A + 3 * H),
                  pl.BlockSpec((None, SEQ, LANES), lambda b, h: (b, 0, CB_SMALL)),
                  cw(0), cw(H), cw(2 * H), row, row, row],
        out_specs=pl.BlockSpec((None, SEQ, W), lambda b, h: (b, 0, h)),
        out_shape=jax.ShapeDtypeStruct((bsz, SEQ, H * D_HEAD_REC), F32),
        scratch_shapes=[pltpu.VMEM((SEQ, W), BF16), pltpu.VMEM((SEQ, W), BF16),
                        pltpu.VMEM((SEQ, W), F32), pltpu.VMEM((SEQ, W), BF16),
                        pltpu.VMEM((SEQ, 2 * CHUNK), BF16),
                        pltpu.VMEM((SEQ // CHUNK * 8, W), F32)],
        compiler_params=_params("parallel", "parallel"),
        name="gdn",
    )(y3, y3, y3, y3, y3, conv_w, conv_w, conv_w, alog_row, dtb_row, norm_w)


def _mlstm_kernel(q_ref, k_ref, v_ref, og_ref, sm_ref, bi_ref, bf_ref, nw_ref, o_ref):
    L = CHUNK
    ri = lax.broadcasted_iota(jnp.int32, (L, 2 * L), 0)
    cp = lax.broadcasted_iota(jnp.int32, (L, 2 * L), 1)
    ci = cp & (L - 1)
    first = cp < L
    tril = ri >= ci
    eye = ri == ci
    bi = bi_ref[...]
    bfr = bf_ref[...]
    nw = nw_ref[...]
    ha = pl.program_id(1) * 2
    zs = jnp.zeros((D_HEAD_REC, D_HEAD_REC), BF16)

    def wide(col_a, col_b, rows=L):
        return jnp.concatenate([jnp.broadcast_to(col_a, (rows, LANES)),
                                jnp.broadcast_to(col_b, (rows, LANES))], axis=1)

    def step(n, carry):
        c_a, c_b, n_st, m_a, m_b = carry
        r0 = pl.multiple_of(n * L, L)
        q = q_ref[pl.ds(r0, L), :]
        k = k_ref[pl.ds(r0, L), :] * (D_HEAD_REC ** -0.5)
        vb = _bf(v_ref[pl.ds(r0, L), :])
        x = sm_ref[pl.ds(r0, L), :]
        lfc = _chunk_cumsum(-_softplus(-(x + bfr)))
        itx = x + bi
        b_a, b_b = _pick_lane(lfc, SM_BF + ha), _pick_lane(lfc, SM_BF + ha + 1)
        it_a, it_b = _pick_lane(itx, SM_BI + ha), _pick_lane(itx, SM_BI + ha + 1)
        b_pk = jnp.where(first, b_a, b_b)
        it_pk = jnp.where(first, it_a, it_b)
        b_row = jnp.sum(jnp.where(eye, b_pk, 0.0), axis=0, keepdims=True)
        it_row = jnp.sum(jnp.where(eye, it_pk, 0.0), axis=0, keepdims=True)
        dm = jnp.where(tril, b_pk - b_row + it_row, NEG)
        dmax_a = jnp.max(jnp.where(first, dm, NEG), axis=1, keepdims=True)
        dmax_b = jnp.max(jnp.where(first, NEG, dm), axis=1, keepdims=True)
        qk = _mm_nt(_bf(q), _bd_wide(_bf(k)))
        a_a, a_b = b_a + m_a, b_b + m_b
        mt_a, mt_b = jnp.maximum(a_a, dmax_a), jnp.maximum(a_b, dmax_b)
        sm = jnp.exp(dm - jnp.where(first, mt_a, mt_b)) * qk
        si_a, si_b = jnp.exp(a_a - mt_a), jnp.exp(a_b - mt_b)
        cbd = jnp.concatenate([jnp.concatenate([_bf(c_a), zs], axis=1),
                               jnp.concatenate([zs, _bf(c_b)], axis=1)], axis=0)
        num = _mm(jnp.concatenate([_bf(q * wide(si_a, si_b)), _bf(sm)], axis=1),
                  jnp.concatenate([cbd, _bd_wide(vb)], axis=0))
        qn = q * n_st
        den_a = (si_a * jnp.sum(qn[:, :LANES], axis=1, keepdims=True)
                 + jnp.sum(jnp.where(first, sm, 0.0), axis=1, keepdims=True))
        den_b = (si_b * jnp.sum(qn[:, LANES:], axis=1, keepdims=True)
                 + jnp.sum(jnp.where(first, 0.0, sm), axis=1, keepdims=True))
        hh = num / wide(jnp.maximum(jnp.abs(den_a), jnp.exp(-mt_a)),
                        jnp.maximum(jnp.abs(den_b), jnp.exp(-mt_b)))
        bl_a, bl_b = b_a[L - 1:L, :], b_b[L - 1:L, :]
        ds_a, ds_b = bl_a - b_a + it_a, bl_b - b_b + it_b
        mn_a = jnp.maximum(bl_a + m_a, jnp.max(ds_a, axis=0, keepdims=True))
        mn_b = jnp.maximum(bl_b + m_b, jnp.max(ds_b, axis=0, keepdims=True))
        wk = k * wide(jnp.exp(ds_a - mn_a), jnp.exp(ds_b - mn_b))
        sc_a, sc_b = jnp.exp(bl_a + m_a - mn_a), jnp.exp(bl_b + m_b - mn_b)
        upd = _mm_tn(_bf(wk), vb)
        c_a = sc_a * c_a + upd[:LANES, :LANES]
        c_b = sc_b * c_b + upd[LANES:, LANES:]
        n_st = wide(sc_a, sc_b, 1) * n_st + jnp.sum(wk, axis=0, keepdims=True)
        on = jnp.concatenate([_rms_rows(hh[:, :LANES], nw), _rms_rows(hh[:, LANES:], nw)], axis=1)
        o_ref[pl.ds(r0, L), :] = on * _sigmoid(og_ref[pl.ds(r0, L), :])
        return c_a, c_b, n_st, mn_a, mn_b

    zc = jnp.zeros((D_HEAD_REC, D_HEAD_REC), F32)
    z1 = jnp.zeros((1, 1), F32)
    lax.fori_loop(0, SEQ // L, step, (zc, zc, jnp.zeros((1, 2 * LANES), F32), z1, z1))


def _mlstm(y3, bi_row, bf_row, norm_w):
    bsz = y3.shape[0]
    H = N_HEADS_REC
    W = 2 * LANES

    def col(off):
        return pl.BlockSpec((None, SEQ, W), lambda b, h, off=off: (b, 0, off // 2 + h))

    row = pl.BlockSpec((1, LANES), lambda b, h: (0, 0))
    return pl.pallas_call(
        _mlstm_kernel,
        grid=(bsz, H // 2),
        in_specs=[col(CB_B), col(CB_B + H), col(CB_B + 2 * H), col(CB_B + 3 * H),
                  pl.BlockSpec((None, SEQ, LANES), lambda b, h: (b, 0, CB_SMALL)),
                  row, row, row],
        out_specs=pl.BlockSpec((None, SEQ, W), lambda b, h: (b, 0, h)),
        out_shape=jax.ShapeDtypeStruct((bsz, SEQ, H * D_HEAD_REC), F32),
        compiler_params=_params("parallel", "parallel"),
        name="mlstm",
    )(y3, y3, y3, y3, y3, bi_row, bf_row, norm_w)


def _nsa_prep_kernel(qt_ref, ks_ref, kw_ref, k2_ref, v2_ref, wkl_ref, wkh_ref, wvl_ref, wvh_ref,
                     pekl_ref, pekh_ref, pevl_ref, pevh_ref, qn_ref, kn0_ref, kn1_ref, kn2_ref,
                     qn_out, ks_out, kw_out, ck_out, cv_out):
    qg = qn_ref[...] * (NSA_DH ** -0.5)
    zq = jnp.zeros((NSA_DH, NSA_HPG * Q_BLOCK), F32)
    zc = jnp.zeros((N_SEG, NSA_DH), F32)
    for g in range(NSA_GROUPS):
        for c in range(N_QBLK):
            x = qt_ref[g, c]
            xn = x * lax.rsqrt(jnp.mean(x * x, axis=0, keepdims=True) + EPS) * qg
            parts = [xn, zq] if g == 0 else [zq, xn]
            qn_out[g, c] = _bf(jnp.concatenate(parts, axis=0))
        k2 = k2_ref[g]
        v2 = v2_ref[g]
        a = _mm(_bf(k2 + pekl_ref[...]), wkl_ref[...])
        bh = _mm(_bf(k2 + pekh_ref[...]), wkh_ref[...])
        ck = _rms_rows(a + pltpu.roll(bh, N_SEG - 1, axis=0), kn0_ref[...])
        parts = [ck, zc] if g == 0 else [zc, ck]
        ck_out[g] = _bf(jnp.concatenate(parts, axis=1))
        at = _mm_nt(wvl_ref[...], _bf(v2 + pevl_ref[...]))
        bt = _mm_nt(wvh_ref[...], _bf(v2 + pevh_ref[...]))
        cv_out[g] = _bf(at + pltpu.roll(bt, N_SEG - 1, axis=1))

    lane = lax.broadcasted_iota(jnp.int32, (Q_BLOCK, LANES), 1)
    lo = lane < NSA_DH

    def norm_keys(src, dst, gain, pad):
        dst[0:pad, :] = jnp.zeros((pad, LANES), BF16)

        def body(i, carry):
            r0 = pl.multiple_of(i * Q_BLOCK, Q_BLOCK)
            x = src[pl.ds(r0, Q_BLOCK), :]
            xx = x * x
            s0 = jnp.sum(jnp.where(lo, xx, 0.0), axis=1, keepdims=True)
            s1 = jnp.sum(jnp.where(lo, 0.0, xx), axis=1, keepdims=True)
            ms = jnp.where(lo, s0, s1) * (1.0 / NSA_DH)
            dst[pl.ds(pad + r0, Q_BLOCK), :] = _bf(x * lax.rsqrt(ms + EPS) * gain)
            return carry
        lax.fori_loop(0, N_QBLK, body, 0)

    norm_keys(ks_ref, ks_out, kn1_ref[...], SEL_PAD)
    norm_keys(kw_ref, kw_out, kn2_ref[...], WINDOW)


def _nsa_prep(y3, qt, k2, v2, wkl, wkh, wvl, wvh, pekl, pekh, pevl, pevh, qn, kn0, kn1, kn2):
    bsz = y3.shape[0]
    G = NSA_GROUPS
    W4 = NSA_HPG * Q_BLOCK

    def full(a):
        nd = a.ndim
        return pl.BlockSpec(a.shape, lambda b, nd=nd: (0,) * nd)

    def per_b(shape):
        nd = len(shape)
        return pl.BlockSpec((None,) + shape, lambda b, nd=nd: (b,) + (0,) * nd)

    consts = [wkl, wkh, wvl, wvh, pekl, pekh, pevl, pevh, qn, kn0, kn1, kn2]
    return pl.pallas_call(
        _nsa_prep_kernel,
        grid=(bsz,),
        in_specs=[per_b((G, N_QBLK, NSA_DH, W4)),
                  pl.BlockSpec((None, SEQ, LANES), lambda b: (b, 0, CB_CKS)),
                  pl.BlockSpec((None, SEQ, LANES), lambda b: (b, 0, CB_CKW)),
                  per_b((G, N_SEG, CMP_STRIDE * NSA_DH)), per_b((G, N_SEG, CMP_STRIDE * NSA_DH))]
                 + [full(a) for a in consts],
        out_specs=[per_b((G, N_QBLK, 2 * NSA_DH, W4)), per_b((SEL_PAD + SEQ, LANES)),
                   per_b((WINDOW + SEQ, LANES)), per_b((G, N_SEG, LANES)), per_b((G, NSA_DH, N_SEG))],
        out_shape=[jax.ShapeDtypeStruct((bsz, G, N_QBLK, 2 * NSA_DH, W4), BF16),
                   jax.ShapeDtypeStruct((bsz, SEL_PAD + SEQ, LANES), BF16),
                   jax.ShapeDtypeStruct((bsz, WINDOW + SEQ, LANES), BF16),
                   jax.ShapeDtypeStruct((bsz, G, N_SEG, LANES), BF16),
                   jax.ShapeDtypeStruct((bsz, G, NSA_DH, N_SEG), BF16)],
        compiler_params=_params("parallel"),
        name="nsa_prep",
    )(qt, y3, y3, k2, v2, *consts)


def _nsa_kernel(q_ref, ck_ref, cv_ref, ks_ref, vs_ref, kw_ref, vw_ref, bct_ref, bt_ref, bw_ref, b31_ref,
                cg_ref, o_ref, sel_s, far_s, bc_s):
    c = pl.program_id(2)
    W4 = NSA_HPG * Q_BLOCK
    q = q_ref[...]
    b31 = b31_ref[...]

    nrow = lax.broadcasted_iota(jnp.int32, (N_SEG + 16, W4), 0) - 16
    near0 = (Q_BLOCK // CMP_STRIDE) * c - 16
    bc_s[...] = jnp.where(nrow < near0, b31, NEG)
    bc_s[pl.ds(pl.multiple_of(near0 + 16, 8), CMP_NEAR), :] = bct_ref[...]
    s = _mm(ck_ref[...], q) + bc_s[16:, :]
    e = jnp.where(s > 0.1 * NEG, jnp.exp(s - jnp.max(s, axis=0, keepdims=True)), 0.0)
    p = e / jnp.maximum(jnp.sum(e, axis=0, keepdims=True), 1e-30)
    o_cmp = _mm(cv_ref[...], _bf(p))

    psum = (p[:, 0:Q_BLOCK] + p[:, Q_BLOCK:2 * Q_BLOCK] + p[:, 2 * Q_BLOCK:3 * Q_BLOCK]
            + p[:, 3 * Q_BLOCK:4 * Q_BLOCK])
    jj = lax.broadcasted_iota(jnp.int32, (N_SEL, N_SEG), 0)
    nn = lax.broadcasted_iota(jnp.int32, (N_SEL, N_SEG), 1)
    ratio = SEL_BLOCK // CMP_STRIDE
    ov = jnp.where((nn >= ratio * jj - 1) & (nn <= ratio * jj + ratio - 1) & (nn < N_SEG - 1), 1.0, 0.0)
    imp = _mm(ov, psum, HP)
    jb = lax.broadcasted_iota(jnp.int32, (N_SEL, Q_BLOCK), 0)
    tq = c * Q_BLOCK + lax.broadcasted_iota(jnp.int32, (N_SEL, Q_BLOCK), 1)
    cur = tq >> 6
    imp = jnp.where((jb == 0) | (jb == cur), FORCE_SCORE, jnp.where(jb <= cur, imp, -1.0))
    rank = jnp.zeros((N_SEL, Q_BLOCK), F32)
    for jp in range(N_SEL):
        rowv = imp[jp:jp + 1, :]
        rank = rank + jnp.where(rowv > imp, 1.0, 0.0)
        if jp < N_SEL - 1:
            rank = rank + jnp.where(jb > jp, jnp.where(rowv == imp, 1.0, 0.0), 0.0)
    sel = jnp.where(rank < SEL_TOPN, 1.0, 0.0)
    sel_s[0:8, :] = jnp.zeros((8, Q_BLOCK), F32)
    sel_s[8:, :] = sel
    far_s[...] = jnp.where(jb < 2 * (c - 1), sel, 0.0)

    def block_mask(rows):
        m1 = jnp.concatenate([jnp.broadcast_to(r, (SEL_BLOCK, Q_BLOCK)) for r in rows], axis=0)
        return jnp.concatenate([m1] * NSA_HPG, axis=1) > 0.5

    def v_tiles(ref, first, n):
        return jnp.concatenate([ref[first + j] for j in range(n)], axis=1)

    r0 = pl.multiple_of(c * Q_BLOCK, Q_BLOCK)
    near = block_mask([sel_s[pl.ds(6 + 2 * c + j, 1), :] for j in range(4)])
    s = jnp.where(near, _mm(ks_ref[pl.ds(r0, 2 * Q_BLOCK), :], q) + bt_ref[...], NEG)
    m_run = jnp.max(s, axis=0, keepdims=True)
    pe = jnp.exp(s - m_run)
    l_run = jnp.sum(pe, axis=0, keepdims=True)
    acc = _mm(v_tiles(vs_ref, c, 2), _bf(pe))

    def far_step(st, carry):
        m_old, l_old, acc_old = carry
        k0 = pl.multiple_of(SEL_PAD + st * FAR_TILE, Q_BLOCK)
        b0 = pl.multiple_of(st * (FAR_TILE // SEL_BLOCK), 8)
        rows = far_s[pl.ds(b0, FAR_TILE // SEL_BLOCK), :]
        msk = block_mask([rows[j:j + 1, :] for j in range(FAR_TILE // SEL_BLOCK)])
        sc = jnp.where(msk, _mm(ks_ref[pl.ds(k0, FAR_TILE), :], q) + b31, NEG)
        m_new = jnp.maximum(m_old, jnp.max(sc, axis=0, keepdims=True))
        alpha = jnp.exp(m_old - m_new)
        pf = jnp.exp(sc - m_new)
        l_new = alpha * l_old + jnp.sum(pf, axis=0, keepdims=True)
        vt = v_tiles(vs_ref, 1 + st * (FAR_TILE // Q_BLOCK), FAR_TILE // Q_BLOCK)
        return m_new, l_new, alpha * acc_old + _mm(vt, _bf(pf))

    n_far = (c + 2) // (FAR_TILE // Q_BLOCK)
    m_run, l_run, acc = lax.fori_loop(0, n_far, far_step, (m_run, l_run, acc))
    o_sel = acc / l_run

    n_w = WINDOW + Q_BLOCK
    wrow = lax.broadcasted_iota(jnp.int32, (n_w, W4), 0)
    s = jnp.where(wrow >= WINDOW - c * Q_BLOCK, _mm(kw_ref[pl.ds(r0, n_w), :], q) + bw_ref[...], NEG)
    pw = jnp.exp(s - jnp.max(s, axis=0, keepdims=True))
    o_win = _mm(v_tiles(vw_ref, c, n_w // Q_BLOCK), _bf(pw)) / jnp.sum(pw, axis=0, keepdims=True)

    gates = _sigmoid(cg_ref[...])

    def gate_row(r):
        return jnp.concatenate([gates[k * 3 + r:k * 3 + r + 1, :] for k in range(NSA_HPG)], axis=1)

    o_ref[...] = gate_row(0) * o_cmp + gate_row(1) * o_sel + gate_row(2) * o_win


def _nsa_attend(qn, ck, cv, ksn, vst, kwn, vwt, tables, cgt):
    bsz = qn.shape[0]
    G = NSA_GROUPS
    W4 = NSA_HPG * Q_BLOCK
    bct, bt, bw, b31 = tables

    def per_bg(a):
        shape = a.shape[2:]
        return pl.BlockSpec((None, None) + shape, lambda b, g, c, n=len(shape): (b, g) + (0,) * n)

    def per_b(a):
        shape = a.shape[1:]
        return pl.BlockSpec((None,) + shape, lambda b, g, c, n=len(shape): (b,) + (0,) * n)

    def per_g(a):
        shape = a.shape[1:]
        return pl.BlockSpec((None,) + shape, lambda b, g, c, n=len(shape): (g,) + (0,) * n)

    return pl.pallas_call(
        _nsa_kernel,
        grid=(bsz, G, N_QBLK),
        in_specs=[
            pl.BlockSpec((None, None, None, 2 * NSA_DH, W4), lambda b, g, c: (b, g, c, 0, 0)),
            per_bg(ck), per_bg(cv), per_b(ksn), per_bg(vst), per_b(kwn), per_bg(vwt),
            per_g(bct), per_g(bt), per_g(bw), per_g(b31),
            pl.BlockSpec((None, None, NSA_HPG * 3, Q_BLOCK), lambda b, g, c: (b, g, 0, c)),
        ],
        out_specs=pl.BlockSpec((None, None, None, NSA_DH, W4), lambda b, g, c: (b, g, c, 0, 0)),
        out_shape=jax.ShapeDtypeStruct((bsz, G, N_QBLK, NSA_DH, W4), F32),
        scratch_shapes=[pltpu.VMEM((8 + N_SEL, Q_BLOCK), F32), pltpu.VMEM((N_SEL, Q_BLOCK), F32),
                        pltpu.VMEM((16 + N_SEG, W4), F32)],
        compiler_params=_params("parallel", "parallel", "arbitrary"),
        name="nsa_attend",
    )(qn, ck, cv, ksn, vst, kwn, vwt, bct, bt, bw, b31, cgt)


def _bucket_starts():
    max_exact = REL_BUCKETS // 2
    d = np.arange(4 * REL_MAX_DIST)
    large = max_exact + (np.log(np.maximum(d, 1) / max_exact) / math.log(REL_MAX_DIST / max_exact)
                         * (REL_BUCKETS - max_exact)).astype(np.int64)
    bucket = np.where(d < max_exact, d, np.minimum(large, REL_BUCKETS - 1))
    assert np.all(np.diff(bucket) >= 0) and bucket[-1] == REL_BUCKETS - 1
    return [int(np.argmax(bucket >= j)) for j in range(REL_BUCKETS)]


BUCKET_STARTS = _bucket_starts()
TABLE_ROWS = 128


def _bias_kernel(rb_ref, bct_ref, bt_ref, bw_ref):
    W4 = NSA_HPG * Q_BLOCK
    rb = rb_ref[...]

    def table(rows, r0, dist_fn, limit):
        r = r0 + lax.broadcasted_iota(jnp.int32, (rows, W4), 0)
        i = lax.broadcasted_iota(jnp.int32, (rows, W4), 1) & (Q_BLOCK - 1)
        dist = dist_fn(r, i)
        acc = jnp.broadcast_to(rb[0:1, :], (rows, W4))
        for j in range(1, REL_BUCKETS):
            acc = jnp.where(dist >= BUCKET_STARTS[j], rb[j:j + 1, :], acc)
        return jnp.where((dist < 0) | (dist >= limit), NEG, acc)

    big = 1 << 30
    bct_ref[...] = table(CMP_NEAR, 0, lambda r, i: i - CMP_STRIDE * r + (16 * CMP_STRIDE - CMP_BLOCK + 1), big)

    def near_rows(k, carry):
        r0 = pl.multiple_of(k * TABLE_ROWS, TABLE_ROWS)
        bt_ref[pl.ds(r0, TABLE_ROWS), :] = table(TABLE_ROWS, r0, lambda r, i: Q_BLOCK + i - r, big)
        return carry

    lax.fori_loop(0, 2 * Q_BLOCK // TABLE_ROWS, near_rows, 0)

    def win_rows(k, carry):
        r0 = pl.multiple_of(k * TABLE_ROWS, TABLE_ROWS)
        bw_ref[pl.ds(r0, TABLE_ROWS), :] = table(TABLE_ROWS, r0, lambda r, i: WINDOW + i - r, WINDOW)
        return carry

    lax.fori_loop(0, (WINDOW + Q_BLOCK) // TABLE_ROWS, win_rows, 0)


def _bias_tables(rel_bias):
    G, K = NSA_GROUPS, NSA_HPG
    W4 = K * Q_BLOCK
    rb = jnp.repeat(rel_bias.reshape(REL_BUCKETS, G, K).transpose(1, 0, 2), Q_BLOCK, axis=2)
    shapes = [(G, CMP_NEAR, W4), (G, 2 * Q_BLOCK, W4), (G, WINDOW + Q_BLOCK, W4)]
    bct, bt, bw = pl.pallas_call(
        _bias_kernel,
        grid=(G,),
        in_specs=[pl.BlockSpec((None, REL_BUCKETS, W4), lambda g: (g, 0, 0))],
        out_specs=[pl.BlockSpec((None,) + s[1:], lambda g: (g, 0, 0)) for s in shapes],
        out_shape=[jax.ShapeDtypeStruct(s, F32) for s in shapes],
        compiler_params=_params("parallel"),
        name="bias_tables",
    )(rb)
    return bct, bt, bw, rb[:, REL_BUCKETS - 1:, :]


def _nsa(y3, small, q_norm, k_norm, cmp_pe, w_cmp, tables):
    bsz = y3.shape[0]
    G, K, DH = NSA_GROUPS, NSA_HPG, NSA_DH

    def cols(cb, n=1):
        return y3[:, :, cb * LANES:(cb + n) * LANES]

    qt = cols(CB_CQ, 4).reshape(bsz, N_QBLK, Q_BLOCK, G, K, DH)
    qt = qt.transpose(0, 3, 1, 5, 4, 2).reshape(bsz, G, N_QBLK, DH, K * Q_BLOCK)

    def segs(x):
        return x.reshape(bsz, N_SEG, CMP_STRIDE, G, DH).transpose(0, 3, 1, 2, 4).reshape(
            bsz, G, N_SEG, CMP_STRIDE * DH)

    def tiles_t(x, pad_tiles):
        xt = _bf(x.reshape(bsz, N_QBLK, Q_BLOCK, G, DH).transpose(0, 3, 1, 4, 2))
        return jnp.pad(xt, ((0, 0), (0, 0), (pad_tiles, 0), (0, 0), (0, 0)))

    wk = w_cmp[0].reshape(2, CMP_STRIDE * DH, DH)
    wv = w_cmp[1].reshape(2, CMP_STRIDE * DH, DH)
    pe = cmp_pe.reshape(2, 2, 1, CMP_STRIDE * DH)
    qn, ksn, kwn, ck, cv = _nsa_prep(
        y3, qt, segs(cols(CB_CKC)), segs(cols(CB_CVC)),
        _bf(wk[0]), _bf(wk[1]), _bf(wv[0].T), _bf(wv[1].T),
        pe[0, 0], pe[0, 1], pe[1, 0], pe[1, 1],
        q_norm.reshape(DH, 1), k_norm[0].reshape(1, DH),
        jnp.tile(k_norm[1], G).reshape(1, LANES), jnp.tile(k_norm[2], G).reshape(1, LANES))
    cgt = small[:, :, SM_CG:SM_CG + NSA_HEADS * 3].reshape(bsz, SEQ, G, K * 3).transpose(0, 2, 3, 1)
    ot = _nsa_attend(qn, ck, cv, ksn, tiles_t(cols(CB_CVS), SEL_PAD // Q_BLOCK),
                     kwn, tiles_t(cols(CB_CVW), WINDOW // Q_BLOCK), tables, cgt)
    ot = ot.reshape(bsz, G, N_QBLK, DH, K, Q_BLOCK).transpose(0, 2, 5, 1, 4, 3)
    return ot.reshape(bsz, SEQ, NSA_HEADS * DH)


def _merge_kernel(h_ref, ya_ref, yb_ref, yc_ref, ga_ref, gb_ref, gc_ref, wb_ref, wo_ref, o_ref):
    acc = None
    for n, (y_ref, g_ref) in enumerate(((ya_ref, ga_ref), (yb_ref, gb_ref), (yc_ref, gc_ref))):
        t = _sigmoid(g_ref[...]) * _mm(_bf(y_ref[...]), wb_ref[n])
        acc = t if acc is None else acc + t
    o_ref[...] = h_ref[...] + _mm(_bf(acc), wo_ref[...])


def _merge(h2, ya, yb, yc, y2, wb, wo):
    m, d = h2.shape
    tm = 512
    row = lambda w: pl.BlockSpec((tm, w), lambda i: (i, 0))
    gate = lambda n: pl.BlockSpec((tm, d), lambda i, n=n: (i, n))
    return pl.pallas_call(
        _merge_kernel,
        grid=(m // tm,),
        in_specs=[row(d), row(BRANCH_W), row(BRANCH_W), row(BRANCH_W), gate(0), gate(1), gate(2),
                  pl.BlockSpec((N_BRANCH, BRANCH_W, d), lambda i: (0, 0, 0)),
                  pl.BlockSpec((d, d), lambda i: (0, 0))],
        out_specs=row(d),
        out_shape=jax.ShapeDtypeStruct((m, d), F32),
        compiler_params=_params("parallel"),
        name="merge",
    )(h2, ya, yb, yc, y2, y2, y2, wb, wo)


def _ffn_kernel(h_ref, g_ref, wg_ref, wu_ref, wo_ref, o_ref, u_ref, acc_ref):
    f = pl.program_id(1)

    @pl.when(f == 0)
    def _():
        x = h_ref[...]
        u_ref[...] = _bf(_rms_rows(x, g_ref[...]))
        acc_ref[...] = x

    u = u_ref[...]
    gt = _mm(u, wg_ref[...])
    up = _mm(u, wu_ref[...])
    acc_ref[...] += _mm(_bf(gt * _sigmoid(gt) * up), wo_ref[...])

    @pl.when(f == pl.num_programs(1) - 1)
    def _():
        o_ref[...] = acc_ref[...]


def _ffn(h2, gain, w_in, w_out):
    m, d = h2.shape
    tm, nf = 512, 2
    tf = D_FF // nf
    return pl.pallas_call(
        _ffn_kernel,
        grid=(m // tm, nf),
        in_specs=[pl.BlockSpec((tm, d), lambda i, f: (i, 0)),
                  pl.BlockSpec((1, d), lambda i, f: (0, 0)),
                  pl.BlockSpec((d, tf), lambda i, f: (0, f)),
                  pl.BlockSpec((d, tf), lambda i, f: (0, f + nf)),
                  pl.BlockSpec((tf, d), lambda i, f: (f, 0))],
        out_specs=pl.BlockSpec((tm, d), lambda i, f: (i, 0)),
        out_shape=jax.ShapeDtypeStruct((m, d), F32),
        scratch_shapes=[pltpu.VMEM((tm, d), BF16), pltpu.VMEM((tm, d), F32)],
        compiler_params=_params("parallel", "arbitrary"),
        name="ffn",
    )(h2, gain, w_in, w_in, w_out)


def _ple_kernel(h_ref, g_ref, p_ref, wg_ref, wp_ref, o_ref):
    x = h_ref[...]
    gate = _sigmoid(_mm(_bf(_rms_rows(x, g_ref[...])), wg_ref[...]))
    o_ref[...] = x + gate * _mm(_bf(p_ref[...]), wp_ref[...])


def _ple(h2, gain, p2, wg, wp):
    m, d = h2.shape
    tm = 512
    return pl.pallas_call(
        _ple_kernel,
        grid=(m // tm,),
        in_specs=[pl.BlockSpec((tm, d), lambda i: (i, 0)),
                  pl.BlockSpec((1, d), lambda i: (0, 0)),
                  pl.BlockSpec((tm, PLE_DIM), lambda i: (i, 0)),
                  pl.BlockSpec((d, d), lambda i: (0, 0)),
                  pl.BlockSpec((PLE_DIM, d), lambda i: (0, 0))],
        out_specs=pl.BlockSpec((tm, d), lambda i: (i, 0)),
        out_shape=jax.ShapeDtypeStruct((m, d), F32),
        compiler_params=_params("parallel"),
        name="ple",
    )(h2, gain, p2, wg, wp)


def _reorder_w_in(w):
    hd = N_HEADS_REC * D_HEAD_REC
    sizes = [hd, hd, hd, hd, N_HEADS_REC, N_HEADS_REC, hd, hd, hd, hd, N_HEADS_REC, N_HEADS_REC,
             NSA_HEADS * NSA_DH] + [NSA_GROUPS * NSA_DH] * 6 + [NSA_HEADS * 3, N_BRANCH * D_MODEL]
    offs = np.concatenate([[0], np.cumsum(sizes)])
    seg = [w[:, offs[i]:offs[i + 1]] for i in range(len(sizes))]
    (aq, ak, av, az, aa, ab, bq, bk, bv, bo, bi, bf, cq, ckc, cvc, cks, cvs, ckw, cvw, cg, mg) = seg
    small = jnp.concatenate([aa, ab, bi, bf, cg], axis=1)
    main = jnp.concatenate([mg, aq, ak, av, az, bq, bk, bv, bo, cq, ckc, cvc, cks, cvs, ckw, cvw, small],
                           axis=1)
    return _bf(jnp.pad(main, ((0, 0), (0, N_COLS - main.shape[1]))))


def _lane_row(vals, offset):
    return jnp.zeros((1, LANES), F32).at[0, offset:offset + vals.shape[0]].set(vals)


def kernel(x, p, rel_bias, norm_mix, w_in, conv_w, gdn_a_log, gdn_dt_bias, gdn_norm, mlstm_b_i, mlstm_b_f,
           mlstm_norm, nsa_q_norm, nsa_k_norm, nsa_cmp_pe, nsa_w_cmp, w_branch, w_out, norm_ffn, w_ffn_in,
           w_ffn_out, norm_ple, w_ple_gate, w_ple_proj):
    bsz, seq, d = x.shape
    assert seq == SEQ and d == D_MODEL
    depth = w_in.shape[0]
    m = bsz * seq
    tables = _bias_tables(rel_bias)
    h2 = x.reshape(m, d)
    for l in range(depth):
        y2 = _in_proj(h2, norm_mix[l].reshape(1, d), _reorder_w_in(w_in[l]))
        y3 = y2.reshape(bsz, seq, N_COLS)
        small = y3[:, :, CB_SMALL * LANES:(CB_SMALL + 1) * LANES]
        ya = _gdn(y3, conv_w[l], _lane_row(gdn_a_log[l], SM_AA), _lane_row(gdn_dt_bias[l], SM_AA),
                  gdn_norm[l].reshape(1, D_HEAD_REC))
        yb = _mlstm(y3, _lane_row(mlstm_b_i[l], SM_BI), _lane_row(mlstm_b_f[l], SM_BF),
                    mlstm_norm[l].reshape(1, D_HEAD_REC))
        yc = _nsa(y3, small, nsa_q_norm[l], nsa_k_norm[l], nsa_cmp_pe[l], nsa_w_cmp[l], tables)
        h2 = _merge(h2, ya.reshape(m, BRANCH_W), yb.reshape(m, BRANCH_W), yc.reshape(m, BRANCH_W), y2,
                    _bf(w_branch[l]), _bf(w_out[l]))
        h2 = _ffn(h2, norm_ffn[l].reshape(1, d), _bf(w_ffn_in[l]), _bf(w_ffn_out[l]))
        h2 = _ple(h2, norm_ple[l].reshape(1, d), p[l].reshape(m, PLE_DIM), _bf(w_ple_gate[l]),
                  _bf(w_ple_proj[l]))
    return h2.reshape(bsz, seq, d)
```

```python
import functools
import math

import numpy as np
import jax
import jax.numpy as jnp
from jax import lax
from jax.experimental import pallas as pl
from jax.experimental.pallas import tpu as pltpu

D_MODEL = 1024
SEQ = 2048
N_HEADS_REC = 4
D_HEAD_REC = 128
CHUNK = 64
GDN_HPS = 2
GDN_CONV = 4
NSA_HEADS = 8
NSA_GROUPS = 2
NSA_HPG = NSA_HEADS // NSA_GROUPS
NSA_DH = 64
CMP_BLOCK = 32
CMP_STRIDE = 16
SEL_BLOCK = 64
SEL_TOPN = 4
WINDOW = 512
Q_BLOCK = 128
N_QBLK = SEQ // Q_BLOCK
N_SEL = SEQ // SEL_BLOCK
N_SEG = SEQ // CMP_STRIDE
SEL_PAD = Q_BLOCK
FAR_TILE = 4 * Q_BLOCK
CMP_NEAR = 24
REL_BUCKETS = 32
REL_MAX_DIST = 128
N_BRANCH = 3
BRANCH_W = 512
D_FF = 2816
PLE_DIM = 256
EPS = 1e-6
NEG = -1e30
FORCE_SCORE = 1e4

LANES = 128
VMEM_LIMIT = 48 * 1024 * 1024

F32 = jnp.float32
BF16 = jnp.bfloat16
HP = lax.Precision.HIGHEST

CB_A = 0
CB_B = 16
CB_CQ = 32
CB_CKC, CB_CVC, CB_CKS, CB_CVS, CB_CKW, CB_CVW = 36, 37, 38, 39, 40, 41
CB_SMALL = 42
N_COLS = 44 * LANES
SM_AA, SM_AB, SM_BI, SM_BF, SM_CG = 0, 4, 8, 12, 16


def _mm(a, b, precision=None):
    return lax.dot_general(a, b, (((1,), (0,)), ((), ())), precision=precision,
                           preferred_element_type=F32)


def _mm_nt(a, b, precision=None):
    return lax.dot_general(a, b, (((1,), (1,)), ((), ())), precision=precision,
                           preferred_element_type=F32)


def _mm_tn(a, b, precision=None):
    return lax.dot_general(a, b, (((0,), (0,)), ((), ())), precision=precision,
                           preferred_element_type=F32)


def _bf(x):
    return x.astype(BF16)


def _sigmoid(x):
    return 1.0 / (1.0 + jnp.exp(-x))


def _softplus(x):
    return jnp.maximum(x, 0.0) + jnp.log1p(jnp.exp(-jnp.abs(x)))


def _rms_rows(x, g):
    return x * lax.rsqrt(jnp.mean(x * x, axis=-1, keepdims=True) + EPS) * g


def _params(*sem):
    return pltpu.CompilerParams(dimension_semantics=sem, vmem_limit_bytes=VMEM_LIMIT)


def _proj_kernel(x_ref, g_ref, w_ref, o_ref, u_ref):
    @pl.when(pl.program_id(1) == 0)
    def _():
        u_ref[...] = _bf(_rms_rows(x_ref[...], g_ref[...]))

    o_ref[...] = _mm_nt(u_ref[...], w_ref[...])


def _in_proj(x2, gain, wt):
    m, d = x2.shape
    n = wt.shape[0]
    tm, tn = 1024, n // 4
    return pl.pallas_call(
        _proj_kernel,
        grid=(m // tm, n // tn),
        in_specs=[pl.BlockSpec((tm, d), lambda i, j: (i, 0)),
                  pl.BlockSpec((1, d), lambda i, j: (0, 0)),
                  pl.BlockSpec((tn, d), lambda i, j: (j, 0))],
        out_specs=pl.BlockSpec((tm, tn), lambda i, j: (i, j)),
        out_shape=jax.ShapeDtypeStruct((m, n), F32),
        scratch_shapes=[pltpu.VMEM((tm, d), BF16)],
        compiler_params=_params("parallel", "arbitrary"),
        name="in_proj",
    )(x2, gain, wt)


def _chunk_masks():
    ri = lax.broadcasted_iota(jnp.int32, (CHUNK, CHUNK), 0)
    ci = lax.broadcasted_iota(jnp.int32, (CHUNK, CHUNK), 1)
    return ri, ci


def _chunk_cumsum(x):
    rowi = lax.broadcasted_iota(jnp.int32, x.shape, 0)
    s = 1
    while s < CHUNK:
        x = x + jnp.where(rowi >= s, pltpu.roll(x, s, axis=0), 0.0)
        s *= 2
    return x


def _pick_lane(x, lane_idx):
    lane = lax.broadcasted_iota(jnp.int32, x.shape, 1)
    return jnp.sum(jnp.where(lane == lane_idx, x, 0.0), axis=1, keepdims=True)


def _col_to_row(col, eye):
    return jnp.sum(jnp.where(eye, col, 0.0), axis=0, keepdims=True)


def _bd_pair(pk):
    lo = lax.broadcasted_iota(jnp.int32, pk.shape, 1) < CHUNK
    zero = jnp.zeros_like(pk)
    return jnp.concatenate([jnp.where(lo, pk, zero), jnp.where(lo, zero, pk)], axis=0)


def _bd_wide(x):
    zero = jnp.zeros((x.shape[0], LANES), x.dtype)
    return jnp.concatenate([jnp.concatenate([x[:, :LANES], zero], axis=1),
                            jnp.concatenate([zero, x[:, LANES:]], axis=1)], axis=0)


def _inv_unit_lower(x, bd, eyef):
    xd = jnp.where(bd, x, 0.0)
    xo = jnp.where(bd, 0.0, x)
    mm = lambda a, b: _mm(_bf(a), _bd_pair(_bf(b)))
    x2 = mm(xd, xd)
    x4 = mm(x2, x2)
    x8 = mm(x4, x4)
    p = eyef - xd
    p = p + mm(p, x2)
    p = p + mm(p, x4)
    p = p + mm(p, x8)
    m = mm(p, xo)
    m2 = mm(m, m)
    q = eyef - m
    q = q + mm(q, m2)
    return mm(q, p)


def _gdn_kernel(q_ref, k_ref, v_ref, z_ref, sm_ref, cwq_ref, cwk_ref, cwv_ref, alog_ref, dtb_ref,
                nw_ref, o_ref, qg_s, kd_s, u_s, w_s, at_s, eg_s):
    L = CHUNK
    W = 2 * LANES
    ri = lax.broadcasted_iota(jnp.int32, (L, 2 * L), 0)
    cp = lax.broadcasted_iota(jnp.int32, (L, 2 * L), 1)
    ci = cp & (L - 1)
    first = cp < L
    tril = ri >= ci
    strict = ri > ci
    eye = ri == ci
    bd = (ri >> 4) == (ci >> 4)
    eyef = jnp.where(eye, 1.0, 0.0)
    alog = alog_ref[...]
    dtb = dtb_ref[...]
    nw = nw_ref[...]
    cwq, cwk, cwv = cwq_ref[...], cwk_ref[...], cwv_ref[...]
    ha = pl.program_id(1) * 2

    def conv_silu(ref, w, n, r0):
        cur = ref[pl.ds(r0, L), :]
        p0 = pl.multiple_of(jnp.maximum(r0 - 8, 0), 8)
        prev = jnp.where(n > 0, ref[pl.ds(p0, 8), :], 0.0)
        win = jnp.concatenate([prev, cur], axis=0)
        acc = cur * w[GDN_CONV - 1:GDN_CONV, :]
        for s in range(1, GDN_CONV):
            acc = acc + pltpu.roll(win, s, axis=0)[8:, :] * w[GDN_CONV - 1 - s:GDN_CONV - s, :]
        return acc * _sigmoid(acc)

    def wide(col_a, col_b, rows=L):
        return jnp.concatenate([jnp.broadcast_to(col_a, (rows, LANES)),
                                jnp.broadcast_to(col_b, (rows, LANES))], axis=1)

    def l2n(x):
        xx = x * x
        return x * wide(lax.rsqrt(jnp.sum(xx[:, :LANES], axis=-1, keepdims=True) + 1e-6),
                        lax.rsqrt(jnp.sum(xx[:, LANES:], axis=-1, keepdims=True) + 1e-6))

    def prep(n, carry):
        r0 = pl.multiple_of(n * L, L)
        x = sm_ref[pl.ds(r0, L), :]
        g = _chunk_cumsum(-jnp.exp(alog) * _softplus(x + dtb))
        sx = _sigmoid(x)
        q = l2n(conv_silu(q_ref, cwq, n, r0)) * (D_HEAD_REC ** -0.5)
        k = l2n(conv_silu(k_ref, cwk, n, r0))
        v = conv_silu(v_ref, cwv, n, r0)
        gc_a, gc_b = _pick_lane(g, SM_AA + ha), _pick_lane(g, SM_AA + ha + 1)
        beta = wide(_pick_lane(sx, SM_AB + ha), _pick_lane(sx, SM_AB + ha + 1))
        gc_pk = jnp.where(first, gc_a, gc_b)
        diff = gc_pk - jnp.sum(jnp.where(eye, gc_pk, 0.0), axis=0, keepdims=True)
        decay = jnp.where(tril, jnp.exp(jnp.where(tril, diff, 0.0)), 0.0)
        kb = k * beta
        kq = _mm_nt(_bf(jnp.concatenate([kb, q], axis=0)), _bd_wide(_bf(k)))
        xm = jnp.where(strict, kq[:L] * decay, 0.0)
        t = _bf(_inv_unit_lower(xm, bd, eyef))
        egc = wide(jnp.exp(gc_a), jnp.exp(gc_b))
        gl_a, gl_b = gc_a[L - 1:L, :], gc_b[L - 1:L, :]
        rhs = jnp.concatenate([_bd_wide(_bf(v * beta)), _bd_wide(_bf(kb * egc))], axis=1)
        uw = _mm(t, rhs)
        u_s[pl.ds(r0, L), :] = uw[:, :W]
        w_s[pl.ds(r0, L), :] = _bf(uw[:, W:])
        at_s[pl.ds(r0, L), :] = _bf(kq[L:] * decay)
        qg_s[pl.ds(r0, L), :] = _bf(q * egc)
        kd_s[pl.ds(r0, L), :] = _bf(k * wide(jnp.exp(gl_a - gc_a), jnp.exp(gl_b - gc_b)))
        eg_s[pl.ds(pl.multiple_of(n * 8, 8), 8), :] = wide(jnp.exp(gl_a), jnp.exp(gl_b), 8)
        return carry

    lax.fori_loop(0, SEQ // L, prep, 0, unroll=2)

    zs = jnp.zeros((D_HEAD_REC, D_HEAD_REC), BF16)

    def step(n, states):
        sa, sb = states
        r0 = pl.multiple_of(n * L, L)
        sbd = jnp.concatenate([jnp.concatenate([_bf(sa), zs], axis=1),
                               jnp.concatenate([zs, _bf(sb)], axis=1)], axis=0)
        eg = eg_s[pl.ds(pl.multiple_of(n * 8, 8), 1), :]
        r = _mm(jnp.concatenate([w_s[pl.ds(r0, L), :], qg_s[pl.ds(r0, L), :]], axis=0), sbd)
        vnb = _bf(u_s[pl.ds(r0, L), :] - r[:L])
        o = r[L:] + _mm(at_s[pl.ds(r0, L), :], _bd_wide(vnb))
        upd = _mm_tn(kd_s[pl.ds(r0, L), :], vnb)
        sa = sa * eg[:, :LANES] + upd[:LANES, :LANES]
        sb = sb * eg[:, LANES:] + upd[LANES:, LANES:]
        z = z_ref[pl.ds(r0, L), :]
        on = jnp.concatenate([_rms_rows(o[:, :LANES], nw), _rms_rows(o[:, LANES:], nw)], axis=1)
        o_ref[pl.ds(r0, L), :] = on * (z * _sigmoid(z))
        return sa, sb

    zero = jnp.zeros((D_HEAD_REC, D_HEAD_REC), F32)
    lax.fori_loop(0, SEQ // L, step, (zero, zero))


def _gdn(y3, conv_w, alog_row, dtb_row, norm_w):
    bsz = y3.shape[0]
    H = N_HEADS_REC
    W = 2 * LANES

    def col(off):
        return pl.BlockSpec((None, SEQ, W), lambda b, h, off=off: (b, 0, off // 2 + h))

    def cw(off):
        return pl.BlockSpec((GDN_CONV, W), lambda b, h, off=off: (0, off // 2 + h))

    row = pl.BlockSpec((1, LANES), lambda b, h: (0, 0))
    return pl.pallas_call(
        _gdn_kernel,
        grid=(bsz, H // 2),
        in_specs=[col(CB_A), col(CB_A + H), col(CB_A + 2 * H), col(CB_A + 3 * H),
                  pl.BlockSpec((None, SEQ, LANES), lambda b, h: (b, 0, CB_SMALL)),
                  cw(0), cw(H), cw(2 * H), row, row, row],
        out_specs=pl.BlockSpec((None, SEQ, W), lambda b, h: (b, 0, h)),
        out_shape=jax.ShapeDtypeStruct((bsz, SEQ, H * D_HEAD_REC), F32),
        scratch_shapes=[pltpu.VMEM((SEQ, W), BF16), pltpu.VMEM((SEQ, W), BF16),
                        pltpu.VMEM((SEQ, W), F32), pltpu.VMEM((SEQ, W), BF16),
                        pltpu.VMEM((SEQ, 2 * CHUNK), BF16),
                        pltpu.VMEM((SEQ // CHUNK * 8, W), F32)],
        compiler_params=_params("parallel", "parallel"),
        name="gdn",
    )(y3, y3, y3, y3, y3, conv_w, conv_w, conv_w, alog_row, dtb_row, norm_w)


def _mlstm_kernel(q_ref, k_ref, v_ref, og_ref, sm_ref, bi_ref, bf_ref, nw_ref, o_ref):
    L = CHUNK
    ri = lax.broadcasted_iota(jnp.int32, (L, 2 * L), 0)
    cp = lax.broadcasted_iota(jnp.int32, (L, 2 * L), 1)
    ci = cp & (L - 1)
    first = cp < L
    tril = ri >= ci
    eye = ri == ci
    bi = bi_ref[...]
    bfr = bf_ref[...]
    nw = nw_ref[...]
    ha = pl.program_id(1) * 2
    zs = jnp.zeros((D_HEAD_REC, D_HEAD_REC), BF16)

    def wide(col_a, col_b, rows=L):
        return jnp.concatenate([jnp.broadcast_to(col_a, (rows, LANES)),
                                jnp.broadcast_to(col_b, (rows, LANES))], axis=1)

    def step(n, carry):
        c_a, c_b, n_st, m_a, m_b = carry
        r0 = pl.multiple_of(n * L, L)
        q = q_ref[pl.ds(r0, L), :]
        k = k_ref[pl.ds(r0, L), :] * (D_HEAD_REC ** -0.5)
        vb = _bf(v_ref[pl.ds(r0, L), :])
        x = sm_ref[pl.ds(r0, L), :]
        lfc = _chunk_cumsum(-_softplus(-(x + bfr)))
        itx = x + bi
        b_a, b_b = _pick_lane(lfc, SM_BF + ha), _pick_lane(lfc, SM_BF + ha + 1)
        it_a, it_b = _pick_lane(itx, SM_BI + ha), _pick_lane(itx, SM_BI + ha + 1)
        b_pk = jnp.where(first, b_a, b_b)
        it_pk = jnp.where(first, it_a, it_b)
        b_row = jnp.sum(jnp.where(eye, b_pk, 0.0), axis=0, keepdims=True)
        it_row = jnp.sum(jnp.where(eye, it_pk, 0.0), axis=0, keepdims=True)
        dm = jnp.where(tril, b_pk - b_row + it_row, NEG)
        dmax_a = jnp.max(jnp.where(first, dm, NEG), axis=1, keepdims=True)
        dmax_b = jnp.max(jnp.where(first, NEG, dm), axis=1, keepdims=True)
        qk = _mm_nt(_bf(q), _bd_wide(_bf(k)))
        a_a, a_b = b_a + m_a, b_b + m_b
        mt_a, mt_b = jnp.maximum(a_a, dmax_a), jnp.maximum(a_b, dmax_b)
        sm = jnp.exp(dm - jnp.where(first, mt_a, mt_b)) * qk
        si_a, si_b = jnp.exp(a_a - mt_a), jnp.exp(a_b - mt_b)
        cbd = jnp.concatenate([jnp.concatenate([_bf(c_a), zs], axis=1),
                               jnp.concatenate([zs, _bf(c_b)], axis=1)], axis=0)
        num = _mm(jnp.concatenate([_bf(q * wide(si_a, si_b)), _bf(sm)], axis=1),
                  jnp.concatenate([cbd, _bd_wide(vb)], axis=0))
        qn = q * n_st
        den_a = (si_a * jnp.sum(qn[:, :LANES], axis=1, keepdims=True)
                 + jnp.sum(jnp.where(first, sm, 0.0), axis=1, keepdims=True))
        den_b = (si_b * jnp.sum(qn[:, LANES:], axis=1, keepdims=True)
                 + jnp.sum(jnp.where(first, 0.0, sm), axis=1, keepdims=True))
        hh = num / wide(jnp.maximum(jnp.abs(den_a), jnp.exp(-mt_a)),
                        jnp.maximum(jnp.abs(den_b), jnp.exp(-mt_b)))
        bl_a, bl_b = b_a[L - 1:L, :], b_b[L - 1:L, :]
        ds_a, ds_b = bl_a - b_a + it_a, bl_b - b_b + it_b
        mn_a = jnp.maximum(bl_a + m_a, jnp.max(ds_a, axis=0, keepdims=True))
        mn_b = jnp.maximum(bl_b + m_b, jnp.max(ds_b, axis=0, keepdims=True))
        wk = k * wide(jnp.exp(ds_a - mn_a), jnp.exp(ds_b - mn_b))
        sc_a, sc_b = jnp.exp(bl_a + m_a - mn_a), jnp.exp(bl_b + m_b - mn_b)
        upd = _mm_tn(_bf(wk), vb)
        c_a = sc_a * c_a + upd[:LANES, :LANES]
        c_b = sc_b * c_b + upd[LANES:, LANES:]
        n_st = wide(sc_a, sc_b, 1) * n_st + jnp.sum(wk, axis=0, keepdims=True)
        on = jnp.concatenate([_rms_rows(hh[:, :LANES], nw), _rms_rows(hh[:, LANES:], nw)], axis=1)
        o_ref[pl.ds(r0, L), :] = on * _sigmoid(og_ref[pl.ds(r0, L), :])
        return c_a, c_b, n_st, mn_a, mn_b

    zc = jnp.zeros((D_HEAD_REC, D_HEAD_REC), F32)
    z1 = jnp.zeros((1, 1), F32)
    lax.fori_loop(0, SEQ // L, step, (zc, zc, jnp.zeros((1, 2 * LANES), F32), z1, z1))


def _mlstm(y3, bi_row, bf_row, norm_w):
    bsz = y3.shape[0]
    H = N_HEADS_REC
    W = 2 * LANES

    def col(off):
        return pl.BlockSpec((None, SEQ, W), lambda b, h, off=off: (b, 0, off // 2 + h))

    row = pl.BlockSpec((1, LANES), lambda b, h: (0, 0))
    return pl.pallas_call(
        _mlstm_kernel,
        grid=(bsz, H // 2),
        in_specs=[col(CB_B), col(CB_B + H), col(CB_B + 2 * H), col(CB_B + 3 * H),
                  pl.BlockSpec((None, SEQ, LANES), lambda b, h: (b, 0, CB_SMALL)),
                  row, row, row],
        out_specs=pl.BlockSpec((None, SEQ, W), lambda b, h: (b, 0, h)),
        out_shape=jax.ShapeDtypeStruct((bsz, SEQ, H * D_HEAD_REC), F32),
        compiler_params=_params("parallel", "parallel"),
        name="mlstm",
    )(y3, y3, y3, y3, y3, bi_row, bf_row, norm_w)


def _nsa_prep_kernel(qt_ref, ks_ref, kw_ref, k2_ref, v2_ref, wkl_ref, wkh_ref, wvl_ref, wvh_ref,
                     pekl_ref, pekh_ref, pevl_ref, pevh_ref, qn_ref, kn0_ref, kn1_ref, kn2_ref,
                     qn_out, ks_out, kw_out, ck_out, cv_out):
    qg = qn_ref[...] * (NSA_DH ** -0.5)
    zq = jnp.zeros((NSA_DH, NSA_HPG * Q_BLOCK), F32)
    zc = jnp.zeros((N_SEG, NSA_DH), F32)
    for g in range(NSA_GROUPS):
        for c in range(N_QBLK):
            x = qt_ref[g, c]
            xn = x * lax.rsqrt(jnp.mean(x * x, axis=0, keepdims=True) + EPS) * qg
            parts = [xn, zq] if g == 0 else [zq, xn]
            qn_out[g, c] = _bf(jnp.concatenate(parts, axis=0))
        k2 = k2_ref[g]
        v2 = v2_ref[g]
        a = _mm(_bf(k2 + pekl_ref[...]), wkl_ref[...])
        bh = _mm(_bf(k2 + pekh_ref[...]), wkh_ref[...])
        ck = _rms_rows(a + pltpu.roll(bh, N_SEG - 1, axis=0), kn0_ref[...])
        parts = [ck, zc] if g == 0 else [zc, ck]
        ck_out[g] = _bf(jnp.concatenate(parts, axis=1))
        at = _mm_nt(wvl_ref[...], _bf(v2 + pevl_ref[...]))
        bt = _mm_nt(wvh_ref[...], _bf(v2 + pevh_ref[...]))
        cv_out[g] = _bf(at + pltpu.roll(bt, N_SEG - 1, axis=1))

    lane = lax.broadcasted_iota(jnp.int32, (Q_BLOCK, LANES), 1)
    lo = lane < NSA_DH

    def norm_keys(src, dst, gain, pad):
        dst[0:pad, :] = jnp.zeros((pad, LANES), BF16)

        def body(i, carry):
            r0 = pl.multiple_of(i * Q_BLOCK, Q_BLOCK)
            x = src[pl.ds(r0, Q_BLOCK), :]
            xx = x * x
            s0 = jnp.sum(jnp.where(lo, xx, 0.0), axis=1, keepdims=True)
            s1 = jnp.sum(jnp.where(lo, 0.0, xx), axis=1, keepdims=True)
            ms = jnp.where(lo, s0, s1) * (1.0 / NSA_DH)
            dst[pl.ds(pad + r0, Q_BLOCK), :] = _bf(x * lax.rsqrt(ms + EPS) * gain)
            return carry
        lax.fori_loop(0, N_QBLK, body, 0)

    norm_keys(ks_ref, ks_out, kn1_ref[...], SEL_PAD)
    norm_keys(kw_ref, kw_out, kn2_ref[...], WINDOW)


def _nsa_prep(y3, qt, k2, v2, wkl, wkh, wvl, wvh, pekl, pekh, pevl, pevh, qn, kn0, kn1, kn2):
    bsz = y3.shape[0]
    G = NSA_GROUPS
    W4 = NSA_HPG * Q_BLOCK

    def full(a):
        nd = a.ndim
        return pl.BlockSpec(a.shape, lambda b, nd=nd: (0,) * nd)

    def per_b(shape):
        nd = len(shape)
        return pl.BlockSpec((None,) + shape, lambda b, nd=nd: (b,) + (0,) * nd)

    consts = [wkl, wkh, wvl, wvh, pekl, pekh, pevl, pevh, qn, kn0, kn1, kn2]
    return pl.pallas_call(
        _nsa_prep_kernel,
        grid=(bsz,),
        in_specs=[per_b((G, N_QBLK, NSA_DH, W4)),
                  pl.BlockSpec((None, SEQ, LANES), lambda b: (b, 0, CB_CKS)),
                  pl.BlockSpec((None, SEQ, LANES), lambda b: (b, 0, CB_CKW)),
                  per_b((G, N_SEG, CMP_STRIDE * NSA_DH)), per_b((G, N_SEG, CMP_STRIDE * NSA_DH))]
                 + [full(a) for a in consts],
        out_specs=[per_b((G, N_QBLK, 2 * NSA_DH, W4)), per_b((SEL_PAD + SEQ, LANES)),
                   per_b((WINDOW + SEQ, LANES)), per_b((G, N_SEG, LANES)), per_b((G, NSA_DH, N_SEG))],
        out_shape=[jax.ShapeDtypeStruct((bsz, G, N_QBLK, 2 * NSA_DH, W4), BF16),
                   jax.ShapeDtypeStruct((bsz, SEL_PAD + SEQ, LANES), BF16),
                   jax.ShapeDtypeStruct((bsz, WINDOW + SEQ, LANES), BF16),
                   jax.ShapeDtypeStruct((bsz, G, N_SEG, LANES), BF16),
                   jax.ShapeDtypeStruct((bsz, G, NSA_DH, N_SEG), BF16)],
        compiler_params=_params("parallel"),
        name="nsa_prep",
    )(qt, y3, y3, k2, v2, *consts)


def _nsa_kernel(q_ref, ck_ref, cv_ref, ks_ref, vs_ref, kw_ref, vw_ref, bct_ref, bt_ref, bw_ref, b31_ref,
                cg_ref, o_ref, sel_s, far_s, bc_s):
    c = pl.program_id(2)
    W4 = NSA_HPG * Q_BLOCK
    q = q_ref[...]
    b31 = b31_ref[...]

    nrow = lax.broadcasted_iota(jnp.int32, (N_SEG + 16, W4), 0) - 16
    near0 = (Q_BLOCK // CMP_STRIDE) * c - 16
    bc_s[...] = jnp.where(nrow < near0, b31, NEG)
    bc_s[pl.ds(pl.multiple_of(near0 + 16, 8), CMP_NEAR), :] = bct_ref[...]
    s = _mm(ck_ref[...], q) + bc_s[16:, :]
    e = jnp.where(s > 0.1 * NEG, jnp.exp(s - jnp.max(s, axis=0, keepdims=True)), 0.0)
    p = e / jnp.maximum(jnp.sum(e, axis=0, keepdims=True), 1e-30)
    o_cmp = _mm(cv_ref[...], _bf(p))

    psum = (p[:, 0:Q_BLOCK] + p[:, Q_BLOCK:2 * Q_BLOCK] + p[:, 2 * Q_BLOCK:3 * Q_BLOCK]
            + p[:, 3 * Q_BLOCK:4 * Q_BLOCK])
    jj = lax.broadcasted_iota(jnp.int32, (N_SEL, N_SEG), 0)
    nn = lax.broadcasted_iota(jnp.int32, (N_SEL, N_SEG), 1)
    ratio = SEL_BLOCK // CMP_STRIDE
    ov = jnp.where((nn >= ratio * jj - 1) & (nn <= ratio * jj + ratio - 1) & (nn < N_SEG - 1), 1.0, 0.0)
    imp = _mm(ov, psum, HP)
    jb = lax.broadcasted_iota(jnp.int32, (N_SEL, Q_BLOCK), 0)
    tq = c * Q_BLOCK + lax.broadcasted_iota(jnp.int32, (N_SEL, Q_BLOCK), 1)
    cur = tq >> 6
    imp = jnp.where((jb == 0) | (jb == cur), FORCE_SCORE, jnp.where(jb <= cur, imp, -1.0))
    rank = jnp.zeros((N_SEL, Q_BLOCK), F32)
    for jp in range(N_SEL):
        rowv = imp[jp:jp + 1, :]
        rank = rank + jnp.where(rowv > imp, 1.0, 0.0)
        if jp < N_SEL - 1:
            rank = rank + jnp.where(jb > jp, jnp.where(rowv == imp, 1.0, 0.0), 0.0)
    sel = jnp.where(rank < SEL_TOPN, 1.0, 0.0)
    sel_s[0:8, :] = jnp.zeros((8, Q_BLOCK), F32)
    sel_s[8:, :] = sel
    far_s[...] = jnp.where(jb < 2 * (c - 1), sel, 0.0)

    def block_mask(rows):
        m1 = jnp.concatenate([jnp.broadcast_to(r, (SEL_BLOCK, Q_BLOCK)) for r in rows], axis=0)
        return jnp.concatenate([m1] * NSA_HPG, axis=1) > 0.5

    def v_tiles(ref, first, n):
        return jnp.concatenate([ref[first + j] for j in range(n)], axis=1)

    r0 = pl.multiple_of(c * Q_BLOCK, Q_BLOCK)
    near = block_mask([sel_s[pl.ds(6 + 2 * c + j, 1), :] for j in range(4)])
    s = jnp.where(near, _mm(ks_ref[pl.ds(r0, 2 * Q_BLOCK), :], q) + bt_ref[...], NEG)
    m_run = jnp.max(s, axis=0, keepdims=True)
    pe = jnp.exp(s - m_run)
    l_run = jnp.sum(pe, axis=0, keepdims=True)
    acc = _mm(v_tiles(vs_ref, c, 2), _bf(pe))

    def far_step(st, carry):
        m_old, l_old, acc_old = carry
        k0 = pl.multiple_of(SEL_PAD + st * FAR_TILE, Q_BLOCK)
        b0 = pl.multiple_of(st * (FAR_TILE // SEL_BLOCK), 8)
        rows = far_s[pl.ds(b0, FAR_TILE // SEL_BLOCK), :]
        msk = block_mask([rows[j:j + 1, :] for j in range(FAR_TILE // SEL_BLOCK)])
        sc = jnp.where(msk, _mm(ks_ref[pl.ds(k0, FAR_TILE), :], q) + b31, NEG)
        m_new = jnp.maximum(m_old, jnp.max(sc, axis=0, keepdims=True))
        alpha = jnp.exp(m_old - m_new)
        pf = jnp.exp(sc - m_new)
        l_new = alpha * l_old + jnp.sum(pf, axis=0, keepdims=True)
        vt = v_tiles(vs_ref, 1 + st * (FAR_TILE // Q_BLOCK), FAR_TILE // Q_BLOCK)
        return m_new, l_new, alpha * acc_old + _mm(vt, _bf(pf))

    n_far = (c + 2) // (FAR_TILE // Q_BLOCK)
    m_run, l_run, acc = lax.fori_loop(0, n_far, far_step, (m_run, l_run, acc))
    o_sel = acc / l_run

    n_w = WINDOW + Q_BLOCK
    wrow = lax.broadcasted_iota(jnp.int32, (n_w, W4), 0)
    s = jnp.where(wrow >= WINDOW - c * Q_BLOCK, _mm(kw_ref[pl.ds(r0, n_w), :], q) + bw_ref[...], NEG)
    pw = jnp.exp(s - jnp.max(s, axis=0, keepdims=True))
    o_win = _mm(v_tiles(vw_ref, c, n_w // Q_BLOCK), _bf(pw)) / jnp.sum(pw, axis=0, keepdims=True)

    gates = _sigmoid(cg_ref[...])

    def gate_row(r):
        return jnp.concatenate([gates[k * 3 + r:k * 3 + r + 1, :] for k in range(NSA_HPG)], axis=1)

    o_ref[...] = gate_row(0) * o_cmp + gate_row(1) * o_sel + gate_row(2) * o_win


def _nsa_attend(qn, ck, cv, ksn, vst, kwn, vwt, tables, cgt):
    bsz = qn.shape[0]
    G = NSA_GROUPS
    W4 = NSA_HPG * Q_BLOCK
    bct, bt, bw, b31 = tables

    def per_bg(a):
        shape = a.shape[2:]
        return pl.BlockSpec((None, None) + shape, lambda b, g, c, n=len(shape): (b, g) + (0,) * n)

    def per_b(a):
        shape = a.shape[1:]
        return pl.BlockSpec((None,) + shape, lambda b, g, c, n=len(shape): (b,) + (0,) * n)

    def per_g(a):
        shape = a.shape[1:]
        return pl.BlockSpec((None,) + shape, lambda b, g, c, n=len(shape): (g,) + (0,) * n)

    return pl.pallas_call(
        _nsa_kernel,
        grid=(bsz, G, N_QBLK),
        in_specs=[
            pl.BlockSpec((None, None, None, 2 * NSA_DH, W4), lambda b, g, c: (b, g, c, 0, 0)),
            per_bg(ck), per_bg(cv), per_b(ksn), per_bg(vst), per_b(kwn), per_bg(vwt),
            per_g(bct), per_g(bt), per_g(bw), per_g(b31),
            pl.BlockSpec((None, None, NSA_HPG * 3, Q_BLOCK), lambda b, g, c: (b, g, 0, c)),
        ],
        out_specs=pl.BlockSpec((None, None, None, NSA_DH, W4), lambda b, g, c: (b, g, c, 0, 0)),
        out_shape=jax.ShapeDtypeStruct((bsz, G, N_QBLK, NSA_DH, W4), F32),
        scratch_shapes=[pltpu.VMEM((8 + N_SEL, Q_BLOCK), F32), pltpu.VMEM((N_SEL, Q_BLOCK), F32),
                        pltpu.VMEM((16 + N_SEG, W4), F32)],
        compiler_params=_params("parallel", "parallel", "arbitrary"),
        name="nsa_attend",
    )(qn, ck, cv, ksn, vst, kwn, vwt, bct, bt, bw, b31, cgt)


def _bucket_starts():
    max_exact = REL_BUCKETS // 2
    d = np.arange(4 * REL_MAX_DIST)
    large = max_exact + (np.log(np.maximum(d, 1) / max_exact) / math.log(REL_MAX_DIST / max_exact)
                         * (REL_BUCKETS - max_exact)).astype(np.int64)
    bucket = np.where(d < max_exact, d, np.minimum(large, REL_BUCKETS - 1))
    assert np.all(np.diff(bucket) >= 0) and bucket[-1] == REL_BUCKETS - 1
    return [int(np.argmax(bucket >= j)) for j in range(REL_BUCKETS)]


BUCKET_STARTS = _bucket_starts()
TABLE_ROWS = 128


def _bias_kernel(rb_ref, bct_ref, bt_ref, bw_ref):
    W4 = NSA_HPG * Q_BLOCK
    rb = rb_ref[...]

    def table(rows, r0, dist_fn, limit):
        r = r0 + lax.broadcasted_iota(jnp.int32, (rows, W4), 0)
        i = lax.broadcasted_iota(jnp.int32, (rows, W4), 1) & (Q_BLOCK - 1)
        dist = dist_fn(r, i)
        acc = jnp.broadcast_to(rb[0:1, :], (rows, W4))
        for j in range(1, REL_BUCKETS):
            acc = jnp.where(dist >= BUCKET_STARTS[j], rb[j:j + 1, :], acc)
        return jnp.where((dist < 0) | (dist >= limit), NEG, acc)

    big = 1 << 30
    bct_ref[...] = table(CMP_NEAR, 0, lambda r, i: i - CMP_STRIDE * r + (16 * CMP_STRIDE - CMP_BLOCK + 1), big)

    def near_rows(k, carry):
        r0 = pl.multiple_of(k * TABLE_ROWS, TABLE_ROWS)
        bt_ref[pl.ds(r0, TABLE_ROWS), :] = table(TABLE_ROWS, r0, lambda r, i: Q_BLOCK + i - r, big)
        return carry

    lax.fori_loop(0, 2 * Q_BLOCK // TABLE_ROWS, near_rows, 0)

    def win_rows(k, carry):
        r0 = pl.multiple_of(k * TABLE_ROWS, TABLE_ROWS)
        bw_ref[pl.ds(r0, TABLE_ROWS), :] = table(TABLE_ROWS, r0, lambda r, i: WINDOW + i - r, WINDOW)
        return carry

    lax.fori_loop(0, (WINDOW + Q_BLOCK) // TABLE_ROWS, win_rows, 0)


def _bias_tables(rel_bias):
    G, K = NSA_GROUPS, NSA_HPG
    W4 = K * Q_BLOCK
    rb = jnp.repeat(rel_bias.reshape(REL_BUCKETS, G, K).transpose(1, 0, 2), Q_BLOCK, axis=2)
    shapes = [(G, CMP_NEAR, W4), (G, 2 * Q_BLOCK, W4), (G, WINDOW + Q_BLOCK, W4)]
    bct, bt, bw = pl.pallas_call(
        _bias_kernel,
        grid=(G,),
        in_specs=[pl.BlockSpec((None, REL_BUCKETS, W4), lambda g: (g, 0, 0))],
        out_specs=[pl.BlockSpec((None,) + s[1:], lambda g: (g, 0, 0)) for s in shapes],
        out_shape=[jax.ShapeDtypeStruct(s, F32) for s in shapes],
        compiler_params=_params("parallel"),
        name="bias_tables",
    )(rb)
    return bct, bt, bw, rb[:, REL_BUCKETS - 1:, :]


def _nsa(y3, small, q_norm, k_norm, cmp_pe, w_cmp, tables):
    bsz = y3.shape[0]
    G, K, DH = NSA_GROUPS, NSA_HPG, NSA_DH

    def cols(cb, n=1):
        return y3[:, :, cb * LANES:(cb + n) * LANES]

    qt = cols(CB_CQ, 4).reshape(bsz, N_QBLK, Q_BLOCK, G, K, DH)
    qt = qt.transpose(0, 3, 1, 5, 4, 2).reshape(bsz, G, N_QBLK, DH, K * Q_BLOCK)

    def segs(x):
        return x.reshape(bsz, N_SEG, CMP_STRIDE, G, DH).transpose(0, 3, 1, 2, 4).reshape(
            bsz, G, N_SEG, CMP_STRIDE * DH)

    def tiles_t(x, pad_tiles):
        xt = _bf(x.reshape(bsz, N_QBLK, Q_BLOCK, G, DH).transpose(0, 3, 1, 4, 2))
        return jnp.pad(xt, ((0, 0), (0, 0), (pad_tiles, 0), (0, 0), (0, 0)))

    wk = w_cmp[0].reshape(2, CMP_STRIDE * DH, DH)
    wv = w_cmp[1].reshape(2, CMP_STRIDE * DH, DH)
    pe = cmp_pe.reshape(2, 2, 1, CMP_STRIDE * DH)
    qn, ksn, kwn, ck, cv = _nsa_prep(
        y3, qt, segs(cols(CB_CKC)), segs(cols(CB_CVC)),
        _bf(wk[0]), _bf(wk[1]), _bf(wv[0].T), _bf(wv[1].T),
        pe[0, 0], pe[0, 1], pe[1, 0], pe[1, 1],
        q_norm.reshape(DH, 1), k_norm[0].reshape(1, DH),
        jnp.tile(k_norm[1], G).reshape(1, LANES), jnp.tile(k_norm[2], G).reshape(1, LANES))
    cgt = small[:, :, SM_CG:SM_CG + NSA_HEADS * 3].reshape(bsz, SEQ, G, K * 3).transpose(0, 2, 3, 1)
    ot = _nsa_attend(qn, ck, cv, ksn, tiles_t(cols(CB_CVS), SEL_PAD // Q_BLOCK),
                     kwn, tiles_t(cols(CB_CVW), WINDOW // Q_BLOCK), tables, cgt)
    ot = ot.reshape(bsz, G, N_QBLK, DH, K, Q_BLOCK).transpose(0, 2, 5, 1, 4, 3)
    return ot.reshape(bsz, SEQ, NSA_HEADS * DH)


def _merge_kernel(h_ref, g_ref, ya_ref, yb_ref, yc_ref, wg_ref, wb_ref, wo_ref, o_ref):
    x = h_ref[...]
    u = _bf(_rms_rows(x, g_ref[...]))
    acc = None
    for n, y_ref in enumerate((ya_ref, yb_ref, yc_ref)):
        t = _sigmoid(_mm_nt(u, wg_ref[n])) * _mm(_bf(y_ref[...]), wb_ref[n])
        acc = t if acc is None else acc + t
    o_ref[...] = x + _mm(_bf(acc), wo_ref[...])


def _merge(h2, gain, ya, yb, yc, wgt, wb, wo):
    m, d = h2.shape
    tm = 512
    row = lambda w: pl.BlockSpec((tm, w), lambda i: (i, 0))
    return pl.pallas_call(
        _merge_kernel,
        grid=(m // tm,),
        in_specs=[row(d), pl.BlockSpec((1, d), lambda i: (0, 0)),
                  row(BRANCH_W), row(BRANCH_W), row(BRANCH_W),
                  pl.BlockSpec((N_BRANCH, d, d), lambda i: (0, 0, 0)),
                  pl.BlockSpec((N_BRANCH, BRANCH_W, d), lambda i: (0, 0, 0)),
                  pl.BlockSpec((d, d), lambda i: (0, 0))],
        out_specs=row(d),
        out_shape=jax.ShapeDtypeStruct((m, d), F32),
        compiler_params=_params("parallel"),
        name="merge",
    )(h2, gain, ya, yb, yc, wgt, wb, wo)


def _ffn_kernel(h_ref, g_ref, wg_ref, wu_ref, wo_ref, o_ref, u_ref, acc_ref):
    f = pl.program_id(1)

    @pl.when(f == 0)
    def _():
        x = h_ref[...]
        u_ref[...] = _bf(_rms_rows(x, g_ref[...]))
        acc_ref[...] = x

    u = u_ref[...]
    gt = _mm(u, wg_ref[...])
    up = _mm(u, wu_ref[...])
    acc_ref[...] += _mm(_bf(gt * _sigmoid(gt) * up), wo_ref[...])

    @pl.when(f == pl.num_programs(1) - 1)
    def _():
        o_ref[...] = acc_ref[...]


def _ffn(h2, gain, w_in, w_out):
    m, d = h2.shape
    tm, nf = 512, 2
    tf = D_FF // nf
    return pl.pallas_call(
        _ffn_kernel,
        grid=(m // tm, nf),
        in_specs=[pl.BlockSpec((tm, d), lambda i, f: (i, 0)),
                  pl.BlockSpec((1, d), lambda i, f: (0, 0)),
                  pl.BlockSpec((d, tf), lambda i, f: (0, f)),
                  pl.BlockSpec((d, tf), lambda i, f: (0, f + nf)),
                  pl.BlockSpec((tf, d), lambda i, f: (f, 0))],
        out_specs=pl.BlockSpec((tm, d), lambda i, f: (i, 0)),
        out_shape=jax.ShapeDtypeStruct((m, d), F32),
        scratch_shapes=[pltpu.VMEM((tm, d), BF16), pltpu.VMEM((tm, d), F32)],
        compiler_params=_params("parallel", "arbitrary"),
        name="ffn",
    )(h2, gain, w_in, w_in, w_out)


def _ple_kernel(h_ref, g_ref, p_ref, wg_ref, wp_ref, o_ref):
    x = h_ref[...]
    gate = _sigmoid(_mm(_bf(_rms_rows(x, g_ref[...])), wg_ref[...]))
    o_ref[...] = x + gate * _mm(_bf(p_ref[...]), wp_ref[...])


def _ple(h2, gain, p2, wg, wp):
    m, d = h2.shape
    tm = 512
    return pl.pallas_call(
        _ple_kernel,
        grid=(m // tm,),
        in_specs=[pl.BlockSpec((tm, d), lambda i: (i, 0)),
                  pl.BlockSpec((1, d), lambda i: (0, 0)),
                  pl.BlockSpec((tm, PLE_DIM), lambda i: (i, 0)),
                  pl.BlockSpec((d, d), lambda i: (0, 0)),
                  pl.BlockSpec((PLE_DIM, d), lambda i: (0, 0))],
        out_specs=pl.BlockSpec((tm, d), lambda i: (i, 0)),
        out_shape=jax.ShapeDtypeStruct((m, d), F32),
        compiler_params=_params("parallel"),
        name="ple",
    )(h2, gain, p2, wg, wp)


def _reorder_w_in(w):
    hd = N_HEADS_REC * D_HEAD_REC
    sizes = [hd, hd, hd, hd, N_HEADS_REC, N_HEADS_REC, hd, hd, hd, hd, N_HEADS_REC, N_HEADS_REC,
             NSA_HEADS * NSA_DH] + [NSA_GROUPS * NSA_DH] * 6 + [NSA_HEADS * 3, N_BRANCH * D_MODEL]
    offs = np.concatenate([[0], np.cumsum(sizes)])
    wt = _bf(w.T)
    seg = [wt[offs[i]:offs[i + 1]] for i in range(len(sizes))]
    (aq, ak, av, az, aa, ab, bq, bk, bv, bo, bi, bf, cq, ckc, cvc, cks, cvs, ckw, cvw, cg, mg) = seg
    main = jnp.concatenate([aq, ak, av, az, bq, bk, bv, bo, cq, ckc, cvc, cks, cvs, ckw, cvw,
                            aa, ab, bi, bf, cg], axis=0)
    main = jnp.pad(main, ((0, N_COLS - main.shape[0]), (0, 0)))
    return main, mg.reshape(N_BRANCH, D_MODEL, D_MODEL)


def _lane_row(vals, offset):
    return jnp.zeros((1, LANES), F32).at[0, offset:offset + vals.shape[0]].set(vals)


def kernel(x, p, rel_bias, norm_mix, w_in, conv_w, gdn_a_log, gdn_dt_bias, gdn_norm, mlstm_b_i, mlstm_b_f,
           mlstm_norm, nsa_q_norm, nsa_k_norm, nsa_cmp_pe, nsa_w_cmp, w_branch, w_out, norm_ffn, w_ffn_in,
           w_ffn_out, norm_ple, w_ple_gate, w_ple_proj):
    bsz, seq, d = x.shape
    assert seq == SEQ and d == D_MODEL
    depth = w_in.shape[0]
    m = bsz * seq
    tables = _bias_tables(rel_bias)
    h2 = x.reshape(m, d)
    for l in range(depth):
        w_mix_t, w_gate_t = _reorder_w_in(w_in[l])
        y2 = _in_proj(h2, norm_mix[l].reshape(1, d), w_mix_t)
        y3 = y2.reshape(bsz, seq, N_COLS)
        small = y3[:, :, CB_SMALL * LANES:(CB_SMALL + 1) * LANES]
        ya = _gdn(y3, conv_w[l], _lane_row(gdn_a_log[l], SM_AA), _lane_row(gdn_dt_bias[l], SM_AA),
                  gdn_norm[l].reshape(1, D_HEAD_REC))
        yb = _mlstm(y3, _lane_row(mlstm_b_i[l], SM_BI), _lane_row(mlstm_b_f[l], SM_BF),
                    mlstm_norm[l].reshape(1, D_HEAD_REC))
        yc = _nsa(y3, small, nsa_q_norm[l], nsa_k_norm[l], nsa_cmp_pe[l], nsa_w_cmp[l], tables)
        h2 = _merge(h2, norm_mix[l].reshape(1, d), ya.reshape(m, BRANCH_W), yb.reshape(m, BRANCH_W),
                    yc.reshape(m, BRANCH_W), w_gate_t, _bf(w_branch[l]), _bf(w_out[l]))
        h2 = _ffn(h2, norm_ffn[l].reshape(1, d), _bf(w_ffn_in[l]), _bf(w_ffn_out[l]))
        h2 = _ple(h2, norm_ple[l].reshape(1, d), p[l].reshape(m, PLE_DIM), _bf(w_ple_gate[l]),
                  _bf(w_ple_proj[l]))
    return h2.reshape(bsz, seq, d)
```

```python
import functools
import math

import numpy as np
import jax
import jax.numpy as jnp
from jax import lax
from jax.experimental import pallas as pl
from jax.experimental.pallas import tpu as pltpu

D_MODEL = 1024
SEQ = 2048
N_HEADS_REC = 4
D_HEAD_REC = 128
CHUNK = 64
GDN_HPS = 2
GDN_CONV = 4
NSA_HEADS = 8
NSA_GROUPS = 2
NSA_HPG = NSA_HEADS // NSA_GROUPS
NSA_DH = 64
CMP_BLOCK = 32
CMP_STRIDE = 16
SEL_BLOCK = 64
SEL_TOPN = 4
WINDOW = 512
Q_BLOCK = 128
N_QBLK = SEQ // Q_BLOCK
N_SEL = SEQ // SEL_BLOCK
N_SEG = SEQ // CMP_STRIDE
SEL_PAD = Q_BLOCK
FAR_TILE = 4 * Q_BLOCK
CMP_NEAR = 24
REL_BUCKETS = 32
REL_MAX_DIST = 128
N_BRANCH = 3
BRANCH_W = 512
D_FF = 2816
PLE_DIM = 256
EPS = 1e-6
NEG = -1e30
FORCE_SCORE = 1e4

LANES = 128
VMEM_LIMIT = 48 * 1024 * 1024

F32 = jnp.float32
BF16 = jnp.bfloat16
HP = lax.Precision.HIGHEST

CB_A = 0
CB_B = 16
CB_CQ = 32
CB_CKC, CB_CVC, CB_CKS, CB_CVS, CB_CKW, CB_CVW = 36, 37, 38, 39, 40, 41
CB_SMALL = 42
N_COLS = 44 * LANES
SM_AA, SM_AB, SM_BI, SM_BF, SM_CG = 0, 4, 8, 12, 16


def _mm(a, b, precision=None):
    return lax.dot_general(a, b, (((1,), (0,)), ((), ())), precision=precision,
                           preferred_element_type=F32)


def _mm_nt(a, b, precision=None):
    return lax.dot_general(a, b, (((1,), (1,)), ((), ())), precision=precision,
                           preferred_element_type=F32)


def _mm_tn(a, b, precision=None):
    return lax.dot_general(a, b, (((0,), (0,)), ((), ())), precision=precision,
                           preferred_element_type=F32)


def _bf(x):
    return x.astype(BF16)


def _sigmoid(x):
    return 1.0 / (1.0 + jnp.exp(-x))


def _softplus(x):
    return jnp.maximum(x, 0.0) + jnp.log1p(jnp.exp(-jnp.abs(x)))


def _rms_rows(x, g):
    return x * lax.rsqrt(jnp.mean(x * x, axis=-1, keepdims=True) + EPS) * g


def _params(*sem):
    return pltpu.CompilerParams(dimension_semantics=sem, vmem_limit_bytes=VMEM_LIMIT)


def _proj_kernel(x_ref, g_ref, w_ref, o_ref, u_ref):
    @pl.when(pl.program_id(1) == 0)
    def _():
        u_ref[...] = _bf(_rms_rows(x_ref[...], g_ref[...]))

    o_ref[...] = _mm(u_ref[...], w_ref[...])


def _in_proj(x2, gain, w):
    m, d = x2.shape
    n = w.shape[1]
    tm, tn = 1024, n // 4
    return pl.pallas_call(
        _proj_kernel,
        grid=(m // tm, n // tn),
        in_specs=[pl.BlockSpec((tm, d), lambda i, j: (i, 0)),
                  pl.BlockSpec((1, d), lambda i, j: (0, 0)),
                  pl.BlockSpec((d, tn), lambda i, j: (0, j))],
        out_specs=pl.BlockSpec((tm, tn), lambda i, j: (i, j)),
        out_shape=jax.ShapeDtypeStruct((m, n), F32),
        scratch_shapes=[pltpu.VMEM((tm, d), BF16)],
        compiler_params=_params("parallel", "arbitrary"),
        name="in_proj",
    )(x2, gain, w)


def _chunk_masks():
    ri = lax.broadcasted_iota(jnp.int32, (CHUNK, CHUNK), 0)
    ci = lax.broadcasted_iota(jnp.int32, (CHUNK, CHUNK), 1)
    return ri, ci


def _chunk_cumsum(x):
    rowi = lax.broadcasted_iota(jnp.int32, x.shape, 0)
    s = 1
    while s < CHUNK:
        x = x + jnp.where(rowi >= s, pltpu.roll(x, s, axis=0), 0.0)
        s *= 2
    return x


def _pick_lane(x, lane_idx):
    lane = lax.broadcasted_iota(jnp.int32, x.shape, 1)
    return jnp.sum(jnp.where(lane == lane_idx, x, 0.0), axis=1, keepdims=True)


def _col_to_row(col, eye):
    return jnp.sum(jnp.where(eye, col, 0.0), axis=0, keepdims=True)


def _bd_pair(pk):
    lo = lax.broadcasted_iota(jnp.int32, pk.shape, 1) < CHUNK
    zero = jnp.zeros_like(pk)
    return jnp.concatenate([jnp.where(lo, pk, zero), jnp.where(lo, zero, pk)], axis=0)


def _bd_wide(x):
    zero = jnp.zeros((x.shape[0], LANES), x.dtype)
    return jnp.concatenate([jnp.concatenate([x[:, :LANES], zero], axis=1),
                            jnp.concatenate([zero, x[:, LANES:]], axis=1)], axis=0)


def _inv_unit_lower(x, bd, eyef):
    xd = jnp.where(bd, x, 0.0)
    xo = jnp.where(bd, 0.0, x)
    mm = lambda a, b: _mm(_bf(a), _bd_pair(_bf(b)))
    x2 = mm(xd, xd)
    x4 = mm(x2, x2)
    x8 = mm(x4, x4)
    p = eyef - xd
    p = p + mm(p, x2)
    p = p + mm(p, x4)
    p = p + mm(p, x8)
    m = mm(p, xo)
    m2 = mm(m, m)
    q = eyef - m
    q = q + mm(q, m2)
    return mm(q, p)


def _gdn_kernel(q_ref, k_ref, v_ref, z_ref, sm_ref, cwq_ref, cwk_ref, cwv_ref, alog_ref, dtb_ref,
                nw_ref, o_ref, qg_s, kd_s, u_s, w_s, at_s, eg_s):
    L = CHUNK
    W = 2 * LANES
    ri = lax.broadcasted_iota(jnp.int32, (L, 2 * L), 0)
    cp = lax.broadcasted_iota(jnp.int32, (L, 2 * L), 1)
    ci = cp & (L - 1)
    first = cp < L
    tril = ri >= ci
    strict = ri > ci
    eye = ri == ci
    bd = (ri >> 4) == (ci >> 4)
    eyef = jnp.where(eye, 1.0, 0.0)
    alog = alog_ref[...]
    dtb = dtb_ref[...]
    nw = nw_ref[...]
    cwq, cwk, cwv = cwq_ref[...], cwk_ref[...], cwv_ref[...]
    ha = pl.program_id(1) * 2

    def conv_silu(ref, w, n, r0):
        cur = ref[pl.ds(r0, L), :]
        p0 = pl.multiple_of(jnp.maximum(r0 - 8, 0), 8)
        prev = jnp.where(n > 0, ref[pl.ds(p0, 8), :], 0.0)
        win = jnp.concatenate([prev, cur], axis=0)
        acc = cur * w[GDN_CONV - 1:GDN_CONV, :]
        for s in range(1, GDN_CONV):
            acc = acc + pltpu.roll(win, s, axis=0)[8:, :] * w[GDN_CONV - 1 - s:GDN_CONV - s, :]
        return acc * _sigmoid(acc)

    def wide(col_a, col_b, rows=L):
        return jnp.concatenate([jnp.broadcast_to(col_a, (rows, LANES)),
                                jnp.broadcast_to(col_b, (rows, LANES))], axis=1)

    def l2n(x):
        xx = x * x
        return x * wide(lax.rsqrt(jnp.sum(xx[:, :LANES], axis=-1, keepdims=True) + 1e-6),
                        lax.rsqrt(jnp.sum(xx[:, LANES:], axis=-1, keepdims=True) + 1e-6))

    def prep(n, carry):
        r0 = pl.multiple_of(n * L, L)
        x = sm_ref[pl.ds(r0, L), :]
        g = _chunk_cumsum(-jnp.exp(alog) * _softplus(x + dtb))
        sx = _sigmoid(x)
        q = l2n(conv_silu(q_ref, cwq, n, r0)) * (D_HEAD_REC ** -0.5)
        k = l2n(conv_silu(k_ref, cwk, n, r0))
        v = conv_silu(v_ref, cwv, n, r0)
        gc_a, gc_b = _pick_lane(g, SM_AA + ha), _pick_lane(g, SM_AA + ha + 1)
        beta = wide(_pick_lane(sx, SM_AB + ha), _pick_lane(sx, SM_AB + ha + 1))
        gc_pk = jnp.where(first, gc_a, gc_b)
        diff = gc_pk - jnp.sum(jnp.where(eye, gc_pk, 0.0), axis=0, keepdims=True)
        decay = jnp.where(tril, jnp.exp(jnp.where(tril, diff, 0.0)), 0.0)
        kb = k * beta
        kq = _mm_nt(_bf(jnp.concatenate([kb, q], axis=0)), _bd_wide(_bf(k)))
        xm = jnp.where(strict, kq[:L] * decay, 0.0)
        t = _bf(_inv_unit_lower(xm, bd, eyef))
        egc = wide(jnp.exp(gc_a), jnp.exp(gc_b))
        gl_a, gl_b = gc_a[L - 1:L, :], gc_b[L - 1:L, :]
        rhs = jnp.concatenate([_bd_wide(_bf(v * beta)), _bd_wide(_bf(kb * egc))], axis=1)
        uw = _mm(t, rhs)
        u_s[pl.ds(r0, L), :] = uw[:, :W]
        w_s[pl.ds(r0, L), :] = _bf(uw[:, W:])
        at_s[pl.ds(r0, L), :] = _bf(kq[L:] * decay)
        qg_s[pl.ds(r0, L), :] = _bf(q * egc)
        kd_s[pl.ds(r0, L), :] = _bf(k * wide(jnp.exp(gl_a - gc_a), jnp.exp(gl_b - gc_b)))
        eg_s[pl.ds(pl.multiple_of(n * 8, 8), 8), :] = wide(jnp.exp(gl_a), jnp.exp(gl_b), 8)
        return carry

    lax.fori_loop(0, SEQ // L, prep, 0, unroll=2)

    zs = jnp.zeros((D_HEAD_REC, D_HEAD_REC), BF16)

    def step(n, states):
        sa, sb = states
        r0 = pl.multiple_of(n * L, L)
        sbd = jnp.concatenate([jnp.concatenate([_bf(sa), zs], axis=1),
                               jnp.concatenate([zs, _bf(sb)], axis=1)], axis=0)
        eg = eg_s[pl.ds(pl.multiple_of(n * 8, 8), 1), :]
        r = _mm(jnp.concatenate([w_s[pl.ds(r0, L), :], qg_s[pl.ds(r0, L), :]], axis=0), sbd)
        vnb = _bf(u_s[pl.ds(r0, L), :] - r[:L])
        o = r[L:] + _mm(at_s[pl.ds(r0, L), :], _bd_wide(vnb))
        upd = _mm_tn(kd_s[pl.ds(r0, L), :], vnb)
        sa = sa * eg[:, :LANES] + upd[:LANES, :LANES]
        sb = sb * eg[:, LANES:] + upd[LANES:, LANES:]
        z = z_ref[pl.ds(r0, L), :]
        on = jnp.concatenate([_rms_rows(o[:, :LANES], nw), _rms_rows(o[:, LANES:], nw)], axis=1)
        o_ref[pl.ds(r0, L), :] = on * (z * _sigmoid(z))
        return sa, sb

    zero = jnp.zeros((D_HEAD_REC, D_HEAD_REC), F32)
    lax.fori_loop(0, SEQ // L, step, (zero, zero))


def _gdn(y3, conv_w, alog_row, dtb_row, norm_w):
    bsz = y3.shape[0]
    H = N_HEADS_REC
    W = 2 * LANES

    def col(off):
        return pl.BlockSpec((None, SEQ, W), lambda b, h, off=off: (b, 0, off // 2 + h))

    def cw(off):
        return pl.BlockSpec((GDN_CONV, W), lambda b, h, off=off: (0, off // 2 + h))

    row = pl.BlockSpec((1, LANES), lambda b, h: (0, 0))
    return pl.pallas_call(
        _gdn_kernel,
        grid=(bsz, H // 2),
        in_specs=[col(CB_A), col(CB_A + H), col(CB_A + 2 * H), col(CB_A + 3 * H),
                  pl.BlockSpec((None, SEQ, LANES), lambda b, h: (b, 0, CB_SMALL)),
                  cw(0), cw(H), cw(2 * H), row, row, row],
        out_specs=pl.BlockSpec((None, SEQ, W), lambda b, h: (b, 0, h)),
        out_shape=jax.ShapeDtypeStruct((bsz, SEQ, H * D_HEAD_REC), F32),
        scratch_shapes=[pltpu.VMEM((SEQ, W), BF16), pltpu.VMEM((SEQ, W), BF16),
                        pltpu.VMEM((SEQ, W), F32), pltpu.VMEM((SEQ, W), BF16),
                        pltpu.VMEM((SEQ, 2 * CHUNK), BF16),
                        pltpu.VMEM((SEQ // CHUNK * 8, W), F32)],
        compiler_params=_params("parallel", "parallel"),
        name="gdn",
    )(y3, y3, y3, y3, y3, conv_w, conv_w, conv_w, alog_row, dtb_row, norm_w)


def _mlstm_kernel(q_ref, k_ref, v_ref, og_ref, sm_ref, bi_ref, bf_ref, nw_ref, o_ref):
    L = CHUNK
    ri = lax.broadcasted_iota(jnp.int32, (L, 2 * L), 0)
    cp = lax.broadcasted_iota(jnp.int32, (L, 2 * L), 1)
    ci = cp & (L - 1)
    first = cp < L
    tril = ri >= ci
    eye = ri == ci
    bi = bi_ref[...]
    bfr = bf_ref[...]
    nw = nw_ref[...]
    ha = pl.program_id(1) * 2
    zs = jnp.zeros((D_HEAD_REC, D_HEAD_REC), BF16)

    def wide(col_a, col_b, rows=L):
        return jnp.concatenate([jnp.broadcast_to(col_a, (rows, LANES)),
                                jnp.broadcast_to(col_b, (rows, LANES))], axis=1)

    def step(n, carry):
        c_a, c_b, n_st, m_a, m_b = carry
        r0 = pl.multiple_of(n * L, L)
        q = q_ref[pl.ds(r0, L), :]
        k = k_ref[pl.ds(r0, L), :] * (D_HEAD_REC ** -0.5)
        vb = _bf(v_ref[pl.ds(r0, L), :])
        x = sm_ref[pl.ds(r0, L), :]
        lfc = _chunk_cumsum(-_softplus(-(x + bfr)))
        itx = x + bi
        b_a, b_b = _pick_lane(lfc, SM_BF + ha), _pick_lane(lfc, SM_BF + ha + 1)
        it_a, it_b = _pick_lane(itx, SM_BI + ha), _pick_lane(itx, SM_BI + ha + 1)
        b_pk = jnp.where(first, b_a, b_b)
        it_pk = jnp.where(first, it_a, it_b)
        b_row = jnp.sum(jnp.where(eye, b_pk, 0.0), axis=0, keepdims=True)
        it_row = jnp.sum(jnp.where(eye, it_pk, 0.0), axis=0, keepdims=True)
        dm = jnp.where(tril, b_pk - b_row + it_row, NEG)
        dmax_a = jnp.max(jnp.where(first, dm, NEG), axis=1, keepdims=True)
        dmax_b = jnp.max(jnp.where(first, NEG, dm), axis=1, keepdims=True)
        qk = _mm_nt(_bf(q), _bd_wide(_bf(k)))
        a_a, a_b = b_a + m_a, b_b + m_b
        mt_a, mt_b = jnp.maximum(a_a, dmax_a), jnp.maximum(a_b, dmax_b)
        sm = jnp.exp(dm - jnp.where(first, mt_a, mt_b)) * qk
        si_a, si_b = jnp.exp(a_a - mt_a), jnp.exp(a_b - mt_b)
        cbd = jnp.concatenate([jnp.concatenate([_bf(c_a), zs], axis=1),
                               jnp.concatenate([zs, _bf(c_b)], axis=1)], axis=0)
        num = _mm(jnp.concatenate([_bf(q * wide(si_a, si_b)), _bf(sm)], axis=1),
                  jnp.concatenate([cbd, _bd_wide(vb)], axis=0))
        qn = q * n_st
        den_a = (si_a * jnp.sum(qn[:, :LANES], axis=1, keepdims=True)
                 + jnp.sum(jnp.where(first, sm, 0.0), axis=1, keepdims=True))
        den_b = (si_b * jnp.sum(qn[:, LANES:], axis=1, keepdims=True)
                 + jnp.sum(jnp.where(first, 0.0, sm), axis=1, keepdims=True))
        hh = num / wide(jnp.maximum(jnp.abs(den_a), jnp.exp(-mt_a)),
                        jnp.maximum(jnp.abs(den_b), jnp.exp(-mt_b)))
        bl_a, bl_b = b_a[L - 1:L, :], b_b[L - 1:L, :]
        ds_a, ds_b = bl_a - b_a + it_a, bl_b - b_b + it_b
        mn_a = jnp.maximum(bl_a + m_a, jnp.max(ds_a, axis=0, keepdims=True))
        mn_b = jnp.maximum(bl_b + m_b, jnp.max(ds_b, axis=0, keepdims=True))
        wk = k * wide(jnp.exp(ds_a - mn_a), jnp.exp(ds_b - mn_b))
        sc_a, sc_b = jnp.exp(bl_a + m_a - mn_a), jnp.exp(bl_b + m_b - mn_b)
        upd = _mm_tn(_bf(wk), vb)
        c_a = sc_a * c_a + upd[:LANES, :LANES]
        c_b = sc_b * c_b + upd[LANES:, LANES:]
        n_st = wide(sc_a, sc_b, 1) * n_st + jnp.sum(wk, axis=0, keepdims=True)
        on = jnp.concatenate([_rms_rows(hh[:, :LANES], nw), _rms_rows(hh[:, LANES:], nw)], axis=1)
        o_ref[pl.ds(r0, L), :] = on * _sigmoid(og_ref[pl.ds(r0, L), :])
        return c_a, c_b, n_st, mn_a, mn_b

    zc = jnp.zeros((D_HEAD_REC, D_HEAD_REC), F32)
    z1 = jnp.zeros((1, 1), F32)
    lax.fori_loop(0, SEQ // L, step, (zc, zc, jnp.zeros((1, 2 * LANES), F32), z1, z1))


def _mlstm(y3, bi_row, bf_row, norm_w):
    bsz = y3.shape[0]
    H = N_HEADS_REC
    W = 2 * LANES

    def col(off):
        return pl.BlockSpec((None, SEQ, W), lambda b, h, off=off: (b, 0, off // 2 + h))

    row = pl.BlockSpec((1, LANES), lambda b, h: (0, 0))
    return pl.pallas_call(
        _mlstm_kernel,
        grid=(bsz, H // 2),
        in_specs=[col(CB_B), col(CB_B + H), col(CB_B + 2 * H), col(CB_B + 3 * H),
                  pl.BlockSpec((None, SEQ, LANES), lambda b, h: (b, 0, CB_SMALL)),
                  row, row, row],
        out_specs=pl.BlockSpec((None, SEQ, W), lambda b, h: (b, 0, h)),
        out_shape=jax.ShapeDtypeStruct((bsz, SEQ, H * D_HEAD_REC), F32),
        compiler_params=_params("parallel", "parallel"),
        name="mlstm",
    )(y3, y3, y3, y3, y3, bi_row, bf_row, norm_w)


def _nsa_prep_kernel(qt_ref, ks_ref, kw_ref, k2_ref, v2_ref, wkl_ref, wkh_ref, wvl_ref, wvh_ref,
                     pekl_ref, pekh_ref, pevl_ref, pevh_ref, qn_ref, kn0_ref, kn1_ref, kn2_ref,
                     qn_out, ks_out, kw_out, ck_out, cv_out):
    qg = qn_ref[...] * (NSA_DH ** -0.5)
    zq = jnp.zeros((NSA_DH, NSA_HPG * Q_BLOCK), F32)
    zc = jnp.zeros((N_SEG, NSA_DH), F32)
    for g in range(NSA_GROUPS):
        for c in range(N_QBLK):
            x = qt_ref[g, c]
            xn = x * lax.rsqrt(jnp.mean(x * x, axis=0, keepdims=True) + EPS) * qg
            parts = [xn, zq] if g == 0 else [zq, xn]
            qn_out[g, c] = _bf(jnp.concatenate(parts, axis=0))
        k2 = k2_ref[g]
        v2 = v2_ref[g]
        a = _mm(_bf(k2 + pekl_ref[...]), wkl_ref[...])
        bh = _mm(_bf(k2 + pekh_ref[...]), wkh_ref[...])
        ck = _rms_rows(a + pltpu.roll(bh, N_SEG - 1, axis=0), kn0_ref[...])
        parts = [ck, zc] if g == 0 else [zc, ck]
        ck_out[g] = _bf(jnp.concatenate(parts, axis=1))
        at = _mm_nt(wvl_ref[...], _bf(v2 + pevl_ref[...]))
        bt = _mm_nt(wvh_ref[...], _bf(v2 + pevh_ref[...]))
        cv_out[g] = _bf(at + pltpu.roll(bt, N_SEG - 1, axis=1))

    lane = lax.broadcasted_iota(jnp.int32, (Q_BLOCK, LANES), 1)
    lo = lane < NSA_DH

    def norm_keys(src, dst, gain, pad):
        dst[0:pad, :] = jnp.zeros((pad, LANES), BF16)

        def body(i, carry):
            r0 = pl.multiple_of(i * Q_BLOCK, Q_BLOCK)
            x = src[pl.ds(r0, Q_BLOCK), :]
            xx = x * x
            s0 = jnp.sum(jnp.where(lo, xx, 0.0), axis=1, keepdims=True)
            s1 = jnp.sum(jnp.where(lo, 0.0, xx), axis=1, keepdims=True)
            ms = jnp.where(lo, s0, s1) * (1.0 / NSA_DH)
            dst[pl.ds(pad + r0, Q_BLOCK), :] = _bf(x * lax.rsqrt(ms + EPS) * gain)
            return carry
        lax.fori_loop(0, N_QBLK, body, 0)

    norm_keys(ks_ref, ks_out, kn1_ref[...], SEL_PAD)
    norm_keys(kw_ref, kw_out, kn2_ref[...], WINDOW)


def _nsa_prep(y3, qt, k2, v2, wkl, wkh, wvl, wvh, pekl, pekh, pevl, pevh, qn, kn0, kn1, kn2):
    bsz = y3.shape[0]
    G = NSA_GROUPS
    W4 = NSA_HPG * Q_BLOCK

    def full(a):
        nd = a.ndim
        return pl.BlockSpec(a.shape, lambda b, nd=nd: (0,) * nd)

    def per_b(shape):
        nd = len(shape)
        return pl.BlockSpec((None,) + shape, lambda b, nd=nd: (b,) + (0,) * nd)

    consts = [wkl, wkh, wvl, wvh, pekl, pekh, pevl, pevh, qn, kn0, kn1, kn2]
    return pl.pallas_call(
        _nsa_prep_kernel,
        grid=(bsz,),
        in_specs=[per_b((G, N_QBLK, NSA_DH, W4)),
                  pl.BlockSpec((None, SEQ, LANES), lambda b: (b, 0, CB_CKS)),
                  pl.BlockSpec((None, SEQ, LANES), lambda b: (b, 0, CB_CKW)),
                  per_b((G, N_SEG, CMP_STRIDE * NSA_DH)), per_b((G, N_SEG, CMP_STRIDE * NSA_DH))]
                 + [full(a) for a in consts],
        out_specs=[per_b((G, N_QBLK, 2 * NSA_DH, W4)), per_b((SEL_PAD + SEQ, LANES)),
                   per_b((WINDOW + SEQ, LANES)), per_b((G, N_SEG, LANES)), per_b((G, NSA_DH, N_SEG))],
        out_shape=[jax.ShapeDtypeStruct((bsz, G, N_QBLK, 2 * NSA_DH, W4), BF16),
                   jax.ShapeDtypeStruct((bsz, SEL_PAD + SEQ, LANES), BF16),
                   jax.ShapeDtypeStruct((bsz, WINDOW + SEQ, LANES), BF16),
                   jax.ShapeDtypeStruct((bsz, G, N_SEG, LANES), BF16),
                   jax.ShapeDtypeStruct((bsz, G, NSA_DH, N_SEG), BF16)],
        compiler_params=_params("parallel"),
        name="nsa_prep",
    )(qt, y3, y3, k2, v2, *consts)


def _nsa_kernel(q_ref, ck_ref, cv_ref, ks_ref, vs_ref, kw_ref, vw_ref, bct_ref, bt_ref, bw_ref, b31_ref,
                cg_ref, o_ref, sel_s, far_s, bc_s):
    c = pl.program_id(2)
    W4 = NSA_HPG * Q_BLOCK
    q = q_ref[...]
    b31 = b31_ref[...]

    def block_mask(rows):
        m1 = jnp.concatenate([jnp.broadcast_to(r, (SEL_BLOCK, Q_BLOCK)) for r in rows], axis=0)
        return jnp.concatenate([m1] * NSA_HPG, axis=1) > 0.5

    def v_tiles(ref, first, n):
        return jnp.concatenate([ref[first + j] for j in range(n)], axis=1)

    r0 = pl.multiple_of(c * Q_BLOCK, Q_BLOCK)
    n_w = WINDOW + Q_BLOCK
    qk_cmp = _mm(ck_ref[...], q)
    qk_near = _mm(ks_ref[pl.ds(r0, 2 * Q_BLOCK), :], q)
    qk_win = _mm(kw_ref[pl.ds(r0, n_w), :], q)

    nrow = lax.broadcasted_iota(jnp.int32, (N_SEG + 16, W4), 0) - 16
    near0 = (Q_BLOCK // CMP_STRIDE) * c - 16
    bc_s[...] = jnp.where(nrow < near0, b31, NEG)
    bc_s[pl.ds(pl.multiple_of(near0 + 16, 8), CMP_NEAR), :] = bct_ref[...]
    s = qk_cmp + bc_s[16:, :]
    e = jnp.where(s > 0.1 * NEG, jnp.exp(s - jnp.max(s, axis=0, keepdims=True)), 0.0)
    p = e / jnp.maximum(jnp.sum(e, axis=0, keepdims=True), 1e-30)
    o_cmp = _mm(cv_ref[...], _bf(p))

    psum = (p[:, 0:Q_BLOCK] + p[:, Q_BLOCK:2 * Q_BLOCK] + p[:, 2 * Q_BLOCK:3 * Q_BLOCK]
            + p[:, 3 * Q_BLOCK:4 * Q_BLOCK])
    jj = lax.broadcasted_iota(jnp.int32, (N_SEL, N_SEG), 0)
    nn = lax.broadcasted_iota(jnp.int32, (N_SEL, N_SEG), 1)
    ratio = SEL_BLOCK // CMP_STRIDE
    ov = jnp.where((nn >= ratio * jj - 1) & (nn <= ratio * jj + ratio - 1) & (nn < N_SEG - 1), 1.0, 0.0)
    imp = _mm(ov, psum, HP)

    wrow = lax.broadcasted_iota(jnp.int32, (n_w, W4), 0)
    s = jnp.where(wrow >= WINDOW - c * Q_BLOCK, qk_win + bw_ref[...], NEG)
    pw = jnp.exp(s - jnp.max(s, axis=0, keepdims=True))
    o_win = _mm(v_tiles(vw_ref, c, n_w // Q_BLOCK), _bf(pw)) / jnp.sum(pw, axis=0, keepdims=True)

    jb =lax.broadcasted_iota(jnp.int32, (N_SEL, Q_BLOCK), 0)
    tq = c * Q_BLOCK + lax.broadcasted_iota(jnp.int32, (N_SEL, Q_BLOCK), 1)
    cur = tq >> 6
    imp = jnp.where((jb == 0) | (jb == cur), FORCE_SCORE, jnp.where(jb <= cur, imp, -1.0))
    rank = jnp.zeros((N_SEL, Q_BLOCK), F32)
    for jp in range(N_SEL):
        rowv = imp[jp:jp + 1, :]
        rank = rank + jnp.where(rowv > imp, 1.0, 0.0)
        if jp < N_SEL - 1:
            rank = rank + jnp.where(jb > jp, jnp.where(rowv == imp, 1.0, 0.0), 0.0)
    sel = jnp.where(rank < SEL_TOPN, 1.0, 0.0)
    sel_s[0:8, :] = jnp.zeros((8, Q_BLOCK), F32)
    sel_s[8:, :] = sel
    far_s[...] = jnp.where(jb < 2 * (c - 1), sel, 0.0)

    near = block_mask([sel_s[pl.ds(6 + 2 * c + j, 1), :] for j in range(4)])
    s = jnp.where(near, qk_near + bt_ref[...], NEG)
    m_run = jnp.max(s, axis=0, keepdims=True)
    pe = jnp.exp(s - m_run)
    l_run = jnp.sum(pe, axis=0, keepdims=True)
    acc = _mm(v_tiles(vs_ref, c, 2), _bf(pe))

    def far_step(st, carry):
        m_old, l_old, acc_old = carry
        k0 = pl.multiple_of(SEL_PAD + st * FAR_TILE, Q_BLOCK)
        b0 = pl.multiple_of(st * (FAR_TILE // SEL_BLOCK), 8)
        rows = far_s[pl.ds(b0, FAR_TILE // SEL_BLOCK), :]
        msk = block_mask([rows[j:j + 1, :] for j in range(FAR_TILE // SEL_BLOCK)])
        sc = jnp.where(msk, _mm(ks_ref[pl.ds(k0, FAR_TILE), :], q) + b31, NEG)
        m_new = jnp.maximum(m_old, jnp.max(sc, axis=0, keepdims=True))
        alpha = jnp.exp(m_old - m_new)
        pf = jnp.exp(sc - m_new)
        l_new = alpha * l_old + jnp.sum(pf, axis=0, keepdims=True)
        vt = v_tiles(vs_ref, 1 + st * (FAR_TILE // Q_BLOCK), FAR_TILE // Q_BLOCK)
        return m_new, l_new, alpha * acc_old + _mm(vt, _bf(pf))

    n_far = (c + 2) // (FAR_TILE // Q_BLOCK)
    m_run, l_run, acc = lax.fori_loop(0, n_far, far_step, (m_run, l_run, acc))
    o_sel = acc / l_run

    gates = _sigmoid(cg_ref[...])

    def gate_row(r):
        return jnp.concatenate([gates[k * 3 + r:k * 3 + r + 1, :] for k in range(NSA_HPG)], axis=1)

    o_ref[...] = gate_row(0) * o_cmp + gate_row(1) * o_sel + gate_row(2) * o_win


def _nsa_attend(qn, ck, cv, ksn, vst, kwn, vwt, tables, cgt):
    bsz = qn.shape[0]
    G = NSA_GROUPS
    W4 = NSA_HPG * Q_BLOCK
    bct, bt, bw, b31 = tables

    def per_bg(a):
        shape = a.shape[2:]
        return pl.BlockSpec((None, None) + shape, lambda b, g, c, n=len(shape): (b, g) + (0,) * n)

    def per_b(a):
        shape = a.shape[1:]
        return pl.BlockSpec((None,) + shape, lambda b, g, c, n=len(shape): (b,) + (0,) * n)

    def per_g(a):
        shape = a.shape[1:]
        return pl.BlockSpec((None,) + shape, lambda b, g, c, n=len(shape): (g,) + (0,) * n)

    return pl.pallas_call(
        _nsa_kernel,
        grid=(bsz, G, N_QBLK),
        in_specs=[
            pl.BlockSpec((None, None, None, 2 * NSA_DH, W4), lambda b, g, c: (b, g, c, 0, 0)),
            per_bg(ck), per_bg(cv), per_b(ksn), per_bg(vst), per_b(kwn), per_bg(vwt),
            per_g(bct), per_g(bt), per_g(bw), per_g(b31),
            pl.BlockSpec((None, None, NSA_HPG * 3, Q_BLOCK), lambda b, g, c: (b, g, 0, c)),
        ],
        out_specs=pl.BlockSpec((None, None, None, NSA_DH, W4), lambda b, g, c: (b, g, c, 0, 0)),
        out_shape=jax.ShapeDtypeStruct((bsz, G, N_QBLK, NSA_DH, W4), F32),
        scratch_shapes=[pltpu.VMEM((8 + N_SEL, Q_BLOCK), F32), pltpu.VMEM((N_SEL, Q_BLOCK), F32),
                        pltpu.VMEM((16 + N_SEG, W4), F32)],
        compiler_params=_params("parallel", "parallel", "arbitrary"),
        name="nsa_attend",
    )(qn, ck, cv, ksn, vst, kwn, vwt, bct, bt, bw, b31, cgt)


def _bucket_starts():
    max_exact = REL_BUCKETS // 2
    d = np.arange(4 * REL_MAX_DIST)
    large = max_exact + (np.log(np.maximum(d, 1) / max_exact) / math.log(REL_MAX_DIST / max_exact)
                         * (REL_BUCKETS - max_exact)).astype(np.int64)
    bucket = np.where(d < max_exact, d, np.minimum(large, REL_BUCKETS - 1))
    assert np.all(np.diff(bucket) >= 0) and bucket[-1] == REL_BUCKETS - 1
    return [int(np.argmax(bucket >= j)) for j in range(REL_BUCKETS)]


BUCKET_STARTS = _bucket_starts()
TABLE_ROWS = 128


def _bias_kernel(rb_ref, bct_ref, bt_ref, bw_ref):
    W4 = NSA_HPG * Q_BLOCK
    rb = rb_ref[...]

    def table(rows, r0, dist_fn, limit):
        r = r0 + lax.broadcasted_iota(jnp.int32, (rows, W4), 0)
        i = lax.broadcasted_iota(jnp.int32, (rows, W4), 1) & (Q_BLOCK - 1)
        dist = dist_fn(r, i)
        acc = jnp.broadcast_to(rb[0:1, :], (rows, W4))
        for j in range(1, REL_BUCKETS):
            acc = jnp.where(dist >= BUCKET_STARTS[j], rb[j:j + 1, :], acc)
        return jnp.where((dist < 0) | (dist >= limit), NEG, acc)

    big = 1 << 30
    bct_ref[...] = table(CMP_NEAR, 0, lambda r, i: i - CMP_STRIDE * r + (16 * CMP_STRIDE - CMP_BLOCK + 1), big)

    def near_rows(k, carry):
        r0 = pl.multiple_of(k * TABLE_ROWS, TABLE_ROWS)
        bt_ref[pl.ds(r0, TABLE_ROWS), :] = table(TABLE_ROWS, r0, lambda r, i: Q_BLOCK + i - r, big)
        return carry

    lax.fori_loop(0, 2 * Q_BLOCK // TABLE_ROWS, near_rows, 0)

    def win_rows(k, carry):
        r0 = pl.multiple_of(k * TABLE_ROWS, TABLE_ROWS)
        bw_ref[pl.ds(r0, TABLE_ROWS), :] = table(TABLE_ROWS, r0, lambda r, i: WINDOW + i - r, WINDOW)
        return carry

    lax.fori_loop(0, (WINDOW + Q_BLOCK) // TABLE_ROWS, win_rows, 0)


def _bias_tables(rel_bias):
    G, K = NSA_GROUPS, NSA_HPG
    W4 = K * Q_BLOCK
    rb = jnp.repeat(rel_bias.reshape(REL_BUCKETS, G, K).transpose(1, 0, 2), Q_BLOCK, axis=2)
    shapes = [(G, CMP_NEAR, W4), (G, 2 * Q_BLOCK, W4), (G, WINDOW + Q_BLOCK, W4)]
    bct, bt, bw = pl.pallas_call(
        _bias_kernel,
        grid=(G,),
        in_specs=[pl.BlockSpec((None, REL_BUCKETS, W4), lambda g: (g, 0, 0))],
        out_specs=[pl.BlockSpec((None,) + s[1:], lambda g: (g, 0, 0)) for s in shapes],
        out_shape=[jax.ShapeDtypeStruct(s, F32) for s in shapes],
        compiler_params=_params("parallel"),
        name="bias_tables",
    )(rb)
    return bct, bt, bw, rb[:, REL_BUCKETS - 1:, :]


def _nsa(y3, small, q_norm, k_norm, cmp_pe, w_cmp, tables):
    bsz = y3.shape[0]
    G, K, DH = NSA_GROUPS, NSA_HPG, NSA_DH

    def cols(cb, n=1):
        return y3[:, :, cb * LANES:(cb + n) * LANES]

    qt = cols(CB_CQ, 4).reshape(bsz, N_QBLK, Q_BLOCK, G, K, DH)
    qt = qt.transpose(0, 3, 1, 5, 4, 2).reshape(bsz, G, N_QBLK, DH, K * Q_BLOCK)

    def segs(x):
        return x.reshape(bsz, N_SEG, CMP_STRIDE, G, DH).transpose(0, 3, 1, 2, 4).reshape(
            bsz, G, N_SEG, CMP_STRIDE * DH)

    def tiles_t(x, pad_tiles):
        xt = _bf(x.reshape(bsz, N_QBLK, Q_BLOCK, G, DH).transpose(0, 3, 1, 4, 2))
        return jnp.pad(xt, ((0, 0), (0, 0), (pad_tiles, 0), (0, 0), (0, 0)))

    wk = w_cmp[0].reshape(2, CMP_STRIDE * DH, DH)
    wv = w_cmp[1].reshape(2, CMP_STRIDE * DH, DH)
    pe = cmp_pe.reshape(2, 2, 1, CMP_STRIDE * DH)
    qn, ksn, kwn, ck, cv = _nsa_prep(
        y3, qt, segs(cols(CB_CKC)), segs(cols(CB_CVC)),
        _bf(wk[0]), _bf(wk[1]), _bf(wv[0].T), _bf(wv[1].T),
        pe[0, 0], pe[0, 1], pe[1, 0], pe[1, 1],
        q_norm.reshape(DH, 1), k_norm[0].reshape(1, DH),
        jnp.tile(k_norm[1], G).reshape(1, LANES), jnp.tile(k_norm[2], G).reshape(1, LANES))
    cgt = small[:, :, SM_CG:SM_CG + NSA_HEADS * 3].reshape(bsz, SEQ, G, K * 3).transpose(0, 2, 3, 1)
    ot = _nsa_attend(qn, ck, cv, ksn, tiles_t(cols(CB_CVS), SEL_PAD // Q_BLOCK),
                     kwn, tiles_t(cols(CB_CVW), WINDOW // Q_BLOCK), tables, cgt)
    ot = ot.reshape(bsz, G, N_QBLK, DH, K, Q_BLOCK).transpose(0, 2, 5, 1, 4, 3)
    return ot.reshape(bsz, SEQ, NSA_HEADS * DH)


def _merge_kernel(h_ref, g_ref, ya_ref, yb_ref, yc_ref, wg_ref, wb_ref, wo_ref, o_ref):
    x = h_ref[...]
    u = _bf(_rms_rows(x, g_ref[...]))
    acc = None
    for n, y_ref in enumerate((ya_ref, yb_ref, yc_ref)):
        t = _sigmoid(_mm(u, wg_ref[:, n * D_MODEL:(n + 1) * D_MODEL])) * _mm(_bf(y_ref[...]), wb_ref[n])
        acc = t if acc is None else acc + t
    o_ref[...] = x + _mm(_bf(acc), wo_ref[...])


def _merge(h2, gain, ya, yb, yc, wgt, wb, wo):
    m, d = h2.shape
    tm = 512
    row = lambda w: pl.BlockSpec((tm, w), lambda i: (i, 0))
    return pl.pallas_call(
        _merge_kernel,
        grid=(m // tm,),
        in_specs=[row(d), pl.BlockSpec((1, d), lambda i: (0, 0)),
                  row(BRANCH_W), row(BRANCH_W), row(BRANCH_W),
                  pl.BlockSpec((d, N_BRANCH * d), lambda i: (0, 0)),
                  pl.BlockSpec((N_BRANCH, BRANCH_W, d), lambda i: (0, 0, 0)),
                  pl.BlockSpec((d, d), lambda i: (0, 0))],
        out_specs=row(d),
        out_shape=jax.ShapeDtypeStruct((m, d), F32),
        compiler_params=_params("parallel"),
        name="merge",
    )(h2, gain, ya, yb, yc, wgt, wb, wo)


def _ffn_kernel(h_ref, g_ref, wg_ref, wu_ref, wo_ref, o_ref, u_ref, acc_ref):
    f = pl.program_id(1)

    @pl.when(f == 0)
    def _():
        x = h_ref[...]
        u_ref[...] = _bf(_rms_rows(x, g_ref[...]))
        acc_ref[...] = x

    u = u_ref[...]
    gt = _mm(u, wg_ref[...])
    up = _mm(u, wu_ref[...])
    acc_ref[...] += _mm(_bf(gt * _sigmoid(gt) * up), wo_ref[...])

    @pl.when(f == pl.num_programs(1) - 1)
    def _():
        o_ref[...] = acc_ref[...]


def _ffn(h2, gain, w_in, w_out):
    m, d = h2.shape
    tm, nf = 512, 2
    tf = D_FF // nf
    return pl.pallas_call(
        _ffn_kernel,
        grid=(m // tm, nf),
        in_specs=[pl.BlockSpec((tm, d), lambda i, f: (i, 0)),
                  pl.BlockSpec((1, d), lambda i, f: (0, 0)),
                  pl.BlockSpec((d, tf), lambda i, f: (0, f)),
                  pl.BlockSpec((d, tf), lambda i, f: (0, f + nf)),
                  pl.BlockSpec((tf, d), lambda i, f: (f, 0))],
        out_specs=pl.BlockSpec((tm, d), lambda i, f: (i, 0)),
        out_shape=jax.ShapeDtypeStruct((m, d), F32),
        scratch_shapes=[pltpu.VMEM((tm, d), BF16), pltpu.VMEM((tm, d), F32)],
        compiler_params=_params("parallel", "arbitrary"),
        name="ffn",
    )(h2, gain, w_in, w_in, w_out)


def _ple_kernel(h_ref, g_ref, p_ref, wg_ref, wp_ref, o_ref):
    x = h_ref[...]
    gate = _sigmoid(_mm(_bf(_rms_rows(x, g_ref[...])), wg_ref[...]))
    o_ref[...] = x + gate * _mm(_bf(p_ref[...]), wp_ref[...])


def _ple(h2, gain, p2, wg, wp):
    m, d = h2.shape
    tm = 512
    return pl.pallas_call(
        _ple_kernel,
        grid=(m // tm,),
        in_specs=[pl.BlockSpec((tm, d), lambda i: (i, 0)),
                  pl.BlockSpec((1, d), lambda i: (0, 0)),
                  pl.BlockSpec((tm, PLE_DIM), lambda i: (i, 0)),
                  pl.BlockSpec((d, d), lambda i: (0, 0)),
                  pl.BlockSpec((PLE_DIM, d), lambda i: (0, 0))],
        out_specs=pl.BlockSpec((tm, d), lambda i: (i, 0)),
        out_shape=jax.ShapeDtypeStruct((m, d), F32),
        compiler_params=_params("parallel"),
        name="ple",
    )(h2, gain, p2, wg, wp)


W_IN_GROUPS = ((0, 0, 16), (16, 8, 16), (32, 16, 10))
W_IN_GATES = (42, 40, 24)
W_IN_SCALAR_BLOCKS = (16, 32, 42)
W_IN_SCALAR_LANES = (8, 16, 40)


def _regroup_kernel(src_ref, off_ref, a_ref, b_ref, s0_ref, s1_ref, s2_ref, o_ref, *, offsets, n_main):
    j = pl.program_id(0)
    x = jnp.concatenate([a_ref[...], b_ref[...]], axis=1)
    for off in offsets:
        @pl.when((off_ref[j] == off) & (j < n_main))
        def _(off=off):
            o_ref[...] = _bf(x[:, off:off + LANES])

    @pl.when(j == n_main)
    def _():
        lane = lax.broadcasted_iota(jnp.int32, (D_MODEL, LANES), 1)
        l0, l1, l2 = W_IN_SCALAR_LANES
        sc = jnp.where(lane < l0, s0_ref[...], jnp.where(lane < l1, s1_ref[...], s2_ref[...]))
        o_ref[...] = _bf(jnp.where(lane < l2, sc, 0.0))

    @pl.when(j > n_main)
    def _():
        o_ref[...] = jnp.zeros(o_ref.shape, BF16)


def _regroup(w_in, layer, groups, n_out, with_scalars):
    src = np.concatenate([blk + np.arange(n) for blk, _, n in groups])
    off = np.concatenate([np.full(n, o) for _, o, n in groups])
    n_main = len(src)
    pad = n_out - n_main
    src = np.concatenate([src, np.zeros(pad, np.int64)]).astype(np.int32)
    off = np.concatenate([off, np.zeros(pad, np.int64)]).astype(np.int32)
    blk = lambda f: pl.BlockSpec((None, D_MODEL, LANES), f)
    fixed = [blk(lambda j, s, o, b=b: (layer, 0, b)) for b in W_IN_SCALAR_BLOCKS]
    return pl.pallas_call(
        functools.partial(_regroup_kernel, offsets=tuple(sorted({o for _, o, _ in groups})),
                          n_main=n_main if with_scalars else n_out),
        grid_spec=pltpu.PrefetchScalarGridSpec(
            num_scalar_prefetch=2,
            grid=(n_out,),
            in_specs=[blk(lambda j, s, o: (layer, 0, s[j])), blk(lambda j, s, o: (layer, 0, s[j] + 1))] + fixed,
            out_specs=pl.BlockSpec((D_MODEL, LANES), lambda j, s, o: (0, j))),
        out_shape=jax.ShapeDtypeStruct((D_MODEL, n_out * LANES), BF16),
        compiler_params=_params("arbitrary"),
        name="regroup_w_in",
    )(jnp.asarray(src), jnp.asarray(off), w_in, w_in, w_in, w_in, w_in)


def _lane_row(vals, offset):
    return jnp.zeros((1, LANES), F32).at[0, offset:offset + vals.shape[0]].set(vals)


def kernel(x, p, rel_bias, norm_mix, w_in, conv_w, gdn_a_log, gdn_dt_bias, gdn_norm, mlstm_b_i, mlstm_b_f,
           mlstm_norm, nsa_q_norm, nsa_k_norm, nsa_cmp_pe, nsa_w_cmp, w_branch, w_out, norm_ffn, w_ffn_in,
           w_ffn_out, norm_ple, w_ple_gate, w_ple_proj):
    bsz, seq, d = x.shape
    assert seq == SEQ and d == D_MODEL
    depth = w_in.shape[0]
    m = bsz * seq
    tables = _bias_tables(rel_bias)
    h2 = x.reshape(m, d)
    for l in range(depth):
        w_mix = _regroup(w_in, l, W_IN_GROUPS, N_COLS // LANES, True)
        w_gate = _regroup(w_in, l, (W_IN_GATES,), W_IN_GATES[2], False)
        y2 = _in_proj(h2, norm_mix[l].reshape(1, d), w_mix)
        y3 = y2.reshape(bsz, seq, N_COLS)
        small = y3[:, :, CB_SMALL * LANES:(CB_SMALL + 1) * LANES]
        ya = _gdn(y3, conv_w[l], _lane_row(gdn_a_log[l], SM_AA), _lane_row(gdn_dt_bias[l], SM_AA),
                  gdn_norm[l].reshape(1, D_HEAD_REC))
        yb = _mlstm(y3, _lane_row(mlstm_b_i[l], SM_BI), _lane_row(mlstm_b_f[l], SM_BF),
                    mlstm_norm[l].reshape(1, D_HEAD_REC))
        yc = _nsa(y3, small, nsa_q_norm[l], nsa_k_norm[l], nsa_cmp_pe[l], nsa_w_cmp[l], tables)
        h2 = _merge(h2, norm_mix[l].reshape(1, d), ya.reshape(m, BRANCH_W), yb.reshape(m, BRANCH_W),
                    yc.reshape(m, BRANCH_W), w_gate, _bf(w_branch[l]), _bf(w_out[l]))
        h2 = _ffn(h2, norm_ffn[l].reshape(1, d), _bf(w_ffn_in[l]), _bf(w_ffn_out[l]))
        h2 = _ple(h2, norm_ple[l].reshape(1, d), p[l].reshape(m, PLE_DIM), _bf(w_ple_gate[l]),
                  _bf(w_ple_proj[l]))
    return h2.reshape(bsz, seq, d)
```

```python
import functools
import math

import numpy as np
import jax
import jax.numpy as jnp
from jax import lax
from jax.experimental import pallas as pl
from jax.experimental.pallas import tpu as pltpu

D_MODEL = 1024
SEQ = 2048
N_HEADS_REC = 4
D_HEAD_REC = 128
CHUNK = 64
GDN_HPS = 2
GDN_CONV = 4
NSA_HEADS = 8
NSA_GROUPS = 2
NSA_HPG = NSA_HEADS // NSA_GROUPS
NSA_DH = 64
CMP_BLOCK = 32
CMP_STRIDE = 16
SEL_BLOCK = 64
SEL_TOPN = 4
WINDOW = 512
Q_BLOCK = 128
N_QBLK = SEQ // Q_BLOCK
N_SEL = SEQ // SEL_BLOCK
N_SEG = SEQ // CMP_STRIDE
SEL_PAD = Q_BLOCK
FAR_TILE = 4 * Q_BLOCK
CMP_NEAR = 24
REL_BUCKETS = 32
REL_MAX_DIST = 128
N_BRANCH = 3
BRANCH_W = 512
D_FF = 2816
PLE_DIM = 256
EPS = 1e-6
NEG = -1e30
FORCE_SCORE = 1e4

LANES = 128
VMEM_LIMIT = 48 * 1024 * 1024

F32 = jnp.float32
BF16 = jnp.bfloat16
HP = lax.Precision.HIGHEST

CB_A = 0
CB_B = 16
CB_CKC, CB_CVC, CB_CKS, CB_CKW = 32, 33, 34, 35
CB_SMALL = 36
N_COLS = 40 * LANES
TR_Q = 0
TR_VS = NSA_HEADS * NSA_DH
TR_VW = TR_VS + NSA_GROUPS * NSA_DH
TR_CG = TR_VW + NSA_GROUPS * NSA_DH
CG_ROWS = 16
N_TROWS = TR_CG + NSA_GROUPS * CG_ROWS
SM_AA, SM_AB, SM_BI, SM_BF, SM_CG = 0, 4, 8, 12, 16


def _mm(a, b, precision=None):
    return lax.dot_general(a, b, (((1,), (0,)), ((), ())), precision=precision,
                           preferred_element_type=F32)


def _mm_nt(a, b, precision=None):
    return lax.dot_general(a, b, (((1,), (1,)), ((), ())), precision=precision,
                           preferred_element_type=F32)


def _mm_tn(a, b, precision=None):
    return lax.dot_general(a, b, (((0,), (0,)), ((), ())), precision=precision,
                           preferred_element_type=F32)


def _bf(x):
    return x.astype(BF16)


def _sigmoid(x):
    return 1.0 / (1.0 + jnp.exp(-x))


def _softplus(x):
    return jnp.maximum(x, 0.0) + jnp.log1p(jnp.exp(-jnp.abs(x)))


def _rms_rows(x, g):
    return x * lax.rsqrt(jnp.mean(x * x, axis=-1, keepdims=True) + EPS) * g


def _params(*sem):
    return pltpu.CompilerParams(dimension_semantics=sem, vmem_limit_bytes=VMEM_LIMIT)


def _proj_kernel(x_ref, g_ref, w_ref, o_ref, u_ref):
    @pl.when(pl.program_id(1) == 0)
    def _():
        u_ref[...] = _bf(_rms_rows(x_ref[...], g_ref[...]))

    o_ref[...] = _mm(u_ref[...], w_ref[...])


def _in_proj(x2, gain, w):
    m, d = x2.shape
    n = w.shape[1]
    tm, tn = 1024, n // 4
    return pl.pallas_call(
        _proj_kernel,
        grid=(m // tm, n // tn),
        in_specs=[pl.BlockSpec((tm, d), lambda i, j: (i, 0)),
                  pl.BlockSpec((1, d), lambda i, j: (0, 0)),
                  pl.BlockSpec((d, tn), lambda i, j: (0, j))],
        out_specs=pl.BlockSpec((tm, tn), lambda i, j: (i, j)),
        out_shape=jax.ShapeDtypeStruct((m, n), F32),
        scratch_shapes=[pltpu.VMEM((tm, d), BF16)],
        compiler_params=_params("parallel", "arbitrary"),
        name="in_proj",
    )(x2, gain, w)


def _proj_t_kernel(x_ref, g_ref, wt_ref, o_ref):
    o_ref[...] = _mm_nt(wt_ref[...], _bf(_rms_rows(x_ref[...], g_ref[...])))


def _in_proj_t(x2, gain, wt, bsz):
    m, d = x2.shape
    n = wt.shape[0]
    ts = 512
    nt = SEQ // ts
    return pl.pallas_call(
        _proj_t_kernel,
        grid=(bsz, nt),
        in_specs=[pl.BlockSpec((ts, d), lambda b, s: (b * nt + s, 0)),
                  pl.BlockSpec((1, d), lambda b, s: (0, 0)),
                  pl.BlockSpec((n, d), lambda b, s: (0, 0))],
        out_specs=pl.BlockSpec((None, n, ts), lambda b, s: (b, 0, s)),
        out_shape=jax.ShapeDtypeStruct((bsz, n, SEQ), F32),
        compiler_params=_params("parallel", "parallel"),
        name="in_proj_t",
    )(x2, gain, wt)


def _chunk_masks():
    ri = lax.broadcasted_iota(jnp.int32, (CHUNK, CHUNK), 0)
    ci = lax.broadcasted_iota(jnp.int32, (CHUNK, CHUNK), 1)
    return ri, ci


def _chunk_cumsum(x):
    rowi = lax.broadcasted_iota(jnp.int32, x.shape, 0)
    s = 1
    while s < CHUNK:
        x = x + jnp.where(rowi >= s, pltpu.roll(x, s, axis=0), 0.0)
        s *= 2
    return x


def _pick_lane(x, lane_idx):
    lane = lax.broadcasted_iota(jnp.int32, x.shape, 1)
    return jnp.sum(jnp.where(lane == lane_idx, x, 0.0), axis=1, keepdims=True)


def _col_to_row(col, eye):
    return jnp.sum(jnp.where(eye, col, 0.0), axis=0, keepdims=True)


def _bd_pair(pk):
    lo = lax.broadcasted_iota(jnp.int32, pk.shape, 1) < CHUNK
    zero = jnp.zeros_like(pk)
    return jnp.concatenate([jnp.where(lo, pk, zero), jnp.where(lo, zero, pk)], axis=0)


def _bd_wide(x):
    zero = jnp.zeros((x.shape[0], LANES), x.dtype)
    return jnp.concatenate([jnp.concatenate([x[:, :LANES], zero], axis=1),
                            jnp.concatenate([zero, x[:, LANES:]], axis=1)], axis=0)


def _inv_unit_lower(x, bd, eyef):
    xd = jnp.where(bd, x, 0.0)
    xo = jnp.where(bd, 0.0, x)
    mm = lambda a, b: _mm(_bf(a), _bd_pair(_bf(b)))
    x2 = mm(xd, xd)
    x4 = mm(x2, x2)
    x8 = mm(x4, x4)
    p = eyef - xd
    p = p + mm(p, x2)
    p = p + mm(p, x4)
    p = p + mm(p, x8)
    m = mm(p, xo)
    m2 = mm(m, m)
    q = eyef - m
    q = q + mm(q, m2)
    return mm(q, p)


def _gdn_kernel(q_ref, k_ref, v_ref, z_ref, sm_ref, cwq_ref, cwk_ref, cwv_ref, alog_ref, dtb_ref,
                nw_ref, o_ref, qg_s, kd_s, u_s, w_s, at_s, eg_s):
    L = CHUNK
    W = 2 * LANES
    ri = lax.broadcasted_iota(jnp.int32, (L, 2 * L), 0)
    cp = lax.broadcasted_iota(jnp.int32, (L, 2 * L), 1)
    ci = cp & (L - 1)
    first = cp < L
    tril = ri >= ci
    strict = ri > ci
    eye = ri == ci
    bd = (ri >> 4) == (ci >> 4)
    eyef = jnp.where(eye, 1.0, 0.0)
    alog = alog_ref[...]
    dtb = dtb_ref[...]
    nw = nw_ref[...]
    cwq, cwk, cwv = cwq_ref[...], cwk_ref[...], cwv_ref[...]
    ha = pl.program_id(1) * 2

    def conv_silu(ref, w, n, r0):
        cur = ref[pl.ds(r0, L), :]
        p0 = pl.multiple_of(jnp.maximum(r0 - 8, 0), 8)
        prev = jnp.where(n > 0, ref[pl.ds(p0, 8), :], 0.0)
        win = jnp.concatenate([prev, cur], axis=0)
        acc = cur * w[GDN_CONV - 1:GDN_CONV, :]
        for s in range(1, GDN_CONV):
            acc = acc + pltpu.roll(win, s, axis=0)[8:, :] * w[GDN_CONV - 1 - s:GDN_CONV - s, :]
        return acc * _sigmoid(acc)

    def wide(col_a, col_b, rows=L):
        return jnp.concatenate([jnp.broadcast_to(col_a, (rows, LANES)),
                                jnp.broadcast_to(col_b, (rows, LANES))], axis=1)

    def l2n(x):
        xx = x * x
        return x * wide(lax.rsqrt(jnp.sum(xx[:, :LANES], axis=-1, keepdims=True) + 1e-6),
                        lax.rsqrt(jnp.sum(xx[:, LANES:], axis=-1, keepdims=True) + 1e-6))

    def prep(n, carry):
        r0 = pl.multiple_of(n * L, L)
        x = sm_ref[pl.ds(r0, L), :]
        g = _chunk_cumsum(-jnp.exp(alog) * _softplus(x + dtb))
        sx = _sigmoid(x)
        q = l2n(conv_silu(q_ref, cwq, n, r0)) * (D_HEAD_REC ** -0.5)
        k = l2n(conv_silu(k_ref, cwk, n, r0))
        v = conv_silu(v_ref, cwv, n, r0)
        gc_a, gc_b = _pick_lane(g, SM_AA + ha), _pick_lane(g, SM_AA + ha + 1)
        beta = wide(_pick_lane(sx, SM_AB + ha), _pick_lane(sx, SM_AB + ha + 1))
        gc_pk = jnp.where(first, gc_a, gc_b)
        diff = gc_pk - jnp.sum(jnp.where(eye, gc_pk, 0.0), axis=0, keepdims=True)
        decay = jnp.where(tril, jnp.exp(jnp.where(tril, diff, 0.0)), 0.0)
        kb = k * beta
        kq = _mm_nt(_bf(jnp.concatenate([kb, q], axis=0)), _bd_wide(_bf(k)))
        xm = jnp.where(strict, kq[:L] * decay, 0.0)
        t = _bf(_inv_unit_lower(xm, bd, eyef))
        egc = wide(jnp.exp(gc_a), jnp.exp(gc_b))
        gl_a, gl_b = gc_a[L - 1:L, :], gc_b[L - 1:L, :]
        rhs = jnp.concatenate([_bd_wide(_bf(v * beta)), _bd_wide(_bf(kb * egc))], axis=1)
        uw = _mm(t, rhs)
        u_s[pl.ds(r0, L), :] = uw[:, :W]
        w_s[pl.ds(r0, L), :] = _bf(uw[:, W:])
        at_s[pl.ds(r0, L), :] = _bf(kq[L:] * decay)
        qg_s[pl.ds(r0, L), :] = _bf(q * egc)
        kd_s[pl.ds(r0, L), :] = _bf(k * wide(jnp.exp(gl_a - gc_a), jnp.exp(gl_b - gc_b)))
        eg_s[pl.ds(pl.multiple_of(n * 8, 8), 8), :] = wide(jnp.exp(gl_a), jnp.exp(gl_b), 8)
        return carry

    lax.fori_loop(0, SEQ // L, prep, 0, unroll=2)

    zs = jnp.zeros((D_HEAD_REC, D_HEAD_REC), BF16)

    def step(n, states):
        sa, sb = states
        r0 = pl.multiple_of(n * L, L)
        sbd = jnp.concatenate([jnp.concatenate([_bf(sa), zs], axis=1),
                               jnp.concatenate([zs, _bf(sb)], axis=1)], axis=0)
        eg = eg_s[pl.ds(pl.multiple_of(n * 8, 8), 1), :]
        r = _mm(jnp.concatenate([w_s[pl.ds(r0, L), :], qg_s[pl.ds(r0, L), :]], axis=0), sbd)
        vnb = _bf(u_s[pl.ds(r0, L), :] - r[:L])
        o = r[L:] + _mm(at_s[pl.ds(r0, L), :], _bd_wide(vnb))
        upd = _mm_tn(kd_s[pl.ds(r0, L), :], vnb)
        sa = sa * eg[:, :LANES] + upd[:LANES, :LANES]
        sb = sb * eg[:, LANES:] + upd[LANES:, LANES:]
        z = z_ref[pl.ds(r0, L), :]
        on = jnp.concatenate([_rms_rows(o[:, :LANES], nw), _rms_rows(o[:, LANES:], nw)], axis=1)
        o_ref[pl.ds(r0, L), :] = on * (z * _sigmoid(z))
        return sa, sb

    zero = jnp.zeros((D_HEAD_REC, D_HEAD_REC), F32)
    lax.fori_loop(0, SEQ // L, step, (zero, zero))


def _gdn(y3, conv_w, alog_row, dtb_row, norm_w):
    bsz = y3.shape[0]
    H = N_HEADS_REC
    W = 2 * LANES

    def col(off):
        return pl.BlockSpec((None, SEQ, W), lambda b, h, off=off: (b, 0, off // 2 + h))

    def cw(off):
        return pl.BlockSpec((GDN_CONV, W), lambda b, h, off=off: (0, off // 2 + h))

    row = pl.BlockSpec((1, LANES), lambda b, h: (0, 0))
    return pl.pallas_call(
        _gdn_kernel,
        grid=(bsz, H // 2),
        in_specs=[col(CB_A), col(CB_A + H), col(CB_A + 2 * H), col(CB_A + 3 * H),
                  pl.BlockSpec((None, SEQ, LANES), lambda b, h: (b, 0, CB_SMALL)),
                  cw(0), cw(H), cw(2 * H), row, row, row],
        out_specs=pl.BlockSpec((None, SEQ, W), lambda b, h: (b, 0, h)),
        out_shape=jax.ShapeDtypeStruct((bsz, SEQ, H * D_HEAD_REC), F32),
        scratch_shapes=[pltpu.VMEM((SEQ, W), BF16), pltpu.VMEM((SEQ, W), BF16),
                        pltpu.VMEM((SEQ, W), F32), pltpu.VMEM((SEQ, W), BF16),
                        pltpu.VMEM((SEQ, 2 * CHUNK), BF16),
                        pltpu.VMEM((SEQ // CHUNK * 8, W), F32)],
        compiler_params=_params("parallel", "parallel"),
        name="gdn",
    )(y3, y3, y3, y3, y3, conv_w, conv_w, conv_w, alog_row, dtb_row, norm_w)


def _mlstm_kernel(q_ref, k_ref, v_ref, og_ref, sm_ref, bi_ref, bf_ref, nw_ref, o_ref):
    L = CHUNK
    ri = lax.broadcasted_iota(jnp.int32, (L, 2 * L), 0)
    cp = lax.broadcasted_iota(jnp.int32, (L, 2 * L), 1)
    ci = cp & (L - 1)
    first = cp < L
    tril = ri >= ci
    eye = ri == ci
    bi = bi_ref[...]
    bfr = bf_ref[...]
    nw = nw_ref[...]
    ha = pl.program_id(1) * 2
    zs = jnp.zeros((D_HEAD_REC, D_HEAD_REC), BF16)

    def wide(col_a, col_b, rows=L):
        return jnp.concatenate([jnp.broadcast_to(col_a, (rows, LANES)),
                                jnp.broadcast_to(col_b, (rows, LANES))], axis=1)

    def step(n, carry):
        c_a, c_b, n_st, m_a, m_b = carry
        r0 = pl.multiple_of(n * L, L)
        q = q_ref[pl.ds(r0, L), :]
        k = k_ref[pl.ds(r0, L), :] * (D_HEAD_REC ** -0.5)
        vb = _bf(v_ref[pl.ds(r0, L), :])
        x = sm_ref[pl.ds(r0, L), :]
        lfc = _chunk_cumsum(-_softplus(-(x + bfr)))
        itx = x + bi
        b_a, b_b = _pick_lane(lfc, SM_BF + ha), _pick_lane(lfc, SM_BF + ha + 1)
        it_a, it_b = _pick_lane(itx, SM_BI + ha), _pick_lane(itx, SM_BI + ha + 1)
        b_pk = jnp.where(first, b_a, b_b)
        it_pk = jnp.where(first, it_a, it_b)
        b_row = jnp.sum(jnp.where(eye, b_pk, 0.0), axis=0, keepdims=True)
        it_row = jnp.sum(jnp.where(eye, it_pk, 0.0), axis=0, keepdims=True)
        dm = jnp.where(tril, b_pk - b_row + it_row, NEG)
        dmax_a = jnp.max(jnp.where(first, dm, NEG), axis=1, keepdims=True)
        dmax_b = jnp.max(jnp.where(first, NEG, dm), axis=1, keepdims=True)
        qk = _mm_nt(_bf(q), _bd_wide(_bf(k)))
        a_a, a_b = b_a + m_a, b_b + m_b
        mt_a, mt_b = jnp.maximum(a_a, dmax_a), jnp.maximum(a_b, dmax_b)
        sm = jnp.exp(dm - jnp.where(first, mt_a, mt_b)) * qk
        si_a, si_b = jnp.exp(a_a - mt_a), jnp.exp(a_b - mt_b)
        cbd = jnp.concatenate([jnp.concatenate([_bf(c_a), zs], axis=1),
                               jnp.concatenate([zs, _bf(c_b)], axis=1)], axis=0)
        num = _mm(jnp.concatenate([_bf(q * wide(si_a, si_b)), _bf(sm)], axis=1),
                  jnp.concatenate([cbd, _bd_wide(vb)], axis=0))
        qn = q * n_st
        den_a = (si_a * jnp.sum(qn[:, :LANES], axis=1, keepdims=True)
                 + jnp.sum(jnp.where(first, sm, 0.0), axis=1, keepdims=True))
        den_b = (si_b * jnp.sum(qn[:, LANES:], axis=1, keepdims=True)
                 + jnp.sum(jnp.where(first, 0.0, sm), axis=1, keepdims=True))
        hh = num / wide(jnp.maximum(jnp.abs(den_a), jnp.exp(-mt_a)),
                        jnp.maximum(jnp.abs(den_b), jnp.exp(-mt_b)))
        bl_a, bl_b = b_a[L - 1:L, :], b_b[L - 1:L, :]
        ds_a, ds_b = bl_a - b_a + it_a, bl_b - b_b + it_b
        mn_a = jnp.maximum(bl_a + m_a, jnp.max(ds_a, axis=0, keepdims=True))
        mn_b = jnp.maximum(bl_b + m_b, jnp.max(ds_b, axis=0, keepdims=True))
        wk = k * wide(jnp.exp(ds_a - mn_a), jnp.exp(ds_b - mn_b))
        sc_a, sc_b = jnp.exp(bl_a + m_a - mn_a), jnp.exp(bl_b + m_b - mn_b)
        upd = _mm_tn(_bf(wk), vb)
        c_a = sc_a * c_a + upd[:LANES, :LANES]
        c_b = sc_b * c_b + upd[LANES:, LANES:]
        n_st = wide(sc_a, sc_b, 1) * n_st + jnp.sum(wk, axis=0, keepdims=True)
        on = jnp.concatenate([_rms_rows(hh[:, :LANES], nw), _rms_rows(hh[:, LANES:], nw)], axis=1)
        o_ref[pl.ds(r0, L), :] = on * _sigmoid(og_ref[pl.ds(r0, L), :])
        return c_a, c_b, n_st, mn_a, mn_b

    zc = jnp.zeros((D_HEAD_REC, D_HEAD_REC), F32)
    z1 = jnp.zeros((1, 1), F32)
    lax.fori_loop(0, SEQ // L, step, (zc, zc, jnp.zeros((1, 2 * LANES), F32), z1, z1))


def _mlstm(y3, bi_row, bf_row, norm_w):
    bsz = y3.shape[0]
    H = N_HEADS_REC
    W = 2 * LANES

    def col(off):
        return pl.BlockSpec((None, SEQ, W), lambda b, h, off=off: (b, 0, off // 2 + h))

    row = pl.BlockSpec((1, LANES), lambda b, h: (0, 0))
    return pl.pallas_call(
        _mlstm_kernel,
        grid=(bsz, H // 2),
        in_specs=[col(CB_B), col(CB_B + H), col(CB_B + 2 * H), col(CB_B + 3 * H),
                  pl.BlockSpec((None, SEQ, LANES), lambda b, h: (b, 0, CB_SMALL)),
                  row, row, row],
        out_specs=pl.BlockSpec((None, SEQ, W), lambda b, h: (b, 0, h)),
        out_shape=jax.ShapeDtypeStruct((bsz, SEQ, H * D_HEAD_REC), F32),
        compiler_params=_params("parallel", "parallel"),
        name="mlstm",
    )(y3, y3, y3, y3, y3, bi_row, bf_row, norm_w)


def _nsa_prep_kernel(yt_ref, ks_ref, kw_ref, k2_ref, v2_ref, wkl_ref, wkh_ref, wvl_ref, wvh_ref,
                     pekl_ref, pekh_ref, pevl_ref, pevh_ref, qn_ref, kn0_ref, kn1_ref, kn2_ref,
                     qn_out, ks_out, kw_out, ck_out, cv_out, vs_out, vw_out):
    G, K, DH = NSA_GROUPS, NSA_HPG, NSA_DH
    lane = lax.broadcasted_iota(jnp.int32, (Q_BLOCK, LANES), 1)
    lo = lane < DH

    def half_ms(x):
        xx = x * x
        s0 = jnp.sum(jnp.where(lo, xx, 0.0), axis=1, keepdims=True)
        s1 = jnp.sum(jnp.where(lo, 0.0, xx), axis=1, keepdims=True)
        return jnp.where(lo, s0, s1) * (1.0 / DH)

    k2 = k2_ref[...]
    v2 = v2_ref[...]
    ckb = (_mm(_bf(k2 + pekl_ref[...]), wkl_ref[...])
           + pltpu.roll(_mm(_bf(k2 + pekh_ref[...]), wkh_ref[...]), N_SEG - 1, axis=0))
    ckn = ckb * lax.rsqrt(half_ms(ckb) + EPS) * kn0_ref[...]
    ck_out[0] = _bf(jnp.where(lo, ckn, 0.0))
    ck_out[1] = _bf(jnp.where(lo, 0.0, ckn))
    cvb = (_mm_nt(wvl_ref[...], _bf(v2 + pevl_ref[...]))
           + pltpu.roll(_mm_nt(wvh_ref[...], _bf(v2 + pevh_ref[...])), N_SEG - 1, axis=1))

    qg = qn_ref[...] * (DH ** -0.5)
    zq = jnp.zeros((DH, K * Q_BLOCK), F32)
    zv = jnp.zeros((DH, Q_BLOCK), BF16)
    for g in range(G):
        cv_out[g] = _bf(cvb[g * DH:(g + 1) * DH, :])
        for c in range(N_QBLK):
            cs = slice(c * Q_BLOCK, (c + 1) * Q_BLOCK)
            r0 = TR_Q + g * K * DH
            x = jnp.concatenate([yt_ref[r0 + k * DH:r0 + (k + 1) * DH, cs] for k in range(K)], axis=1)
            xn = x * lax.rsqrt(jnp.mean(x * x, axis=0, keepdims=True) + EPS) * qg
            parts = [xn, zq] if g == 0 else [zq, xn]
            qn_out[g, c] = _bf(jnp.concatenate(parts, axis=0))
            vs_out[g, SEL_PAD // Q_BLOCK + c] = _bf(yt_ref[TR_VS + g * DH:TR_VS + (g + 1) * DH, cs])
            vw_out[g, WINDOW // Q_BLOCK + c] = _bf(yt_ref[TR_VW + g * DH:TR_VW + (g + 1) * DH, cs])
        for j in range(SEL_PAD // Q_BLOCK):
            vs_out[g, j] = zv
        for j in range(WINDOW // Q_BLOCK):
            vw_out[g, j] = zv

    def norm_keys(src, dst, gain, pad):
        dst[0:pad, :] = jnp.zeros((pad, LANES), BF16)

        def body(i, carry):
            r0 = pl.multiple_of(i * Q_BLOCK, Q_BLOCK)
            x = src[pl.ds(r0, Q_BLOCK), :]
            dst[pl.ds(pad + r0, Q_BLOCK), :] = _bf(x * lax.rsqrt(half_ms(x) + EPS) * gain)
            return carry
        lax.fori_loop(0, N_QBLK, body, 0)

    norm_keys(ks_ref, ks_out, kn1_ref[...], SEL_PAD)
    norm_keys(kw_ref, kw_out, kn2_ref[...], WINDOW)


def _nsa_prep(y3, yt, k2, v2, consts):
    bsz = y3.shape[0]
    G = NSA_GROUPS
    W4 = NSA_HPG * Q_BLOCK

    def full(a):
        nd = a.ndim
        return pl.BlockSpec(a.shape, lambda b, nd=nd: (0,) * nd)

    def per_b(shape):
        nd = len(shape)
        return pl.BlockSpec((None,) + shape, lambda b, nd=nd: (b,) + (0,) * nd)

    shapes = [((G, N_QBLK, 2 * NSA_DH, W4), BF16),
              ((SEL_PAD + SEQ, LANES), BF16), ((WINDOW + SEQ, LANES), BF16),
              ((G, N_SEG, LANES), BF16), ((G, NSA_DH, N_SEG), BF16),
              ((G, (SEL_PAD + SEQ) // Q_BLOCK, NSA_DH, Q_BLOCK), BF16),
              ((G, (WINDOW + SEQ) // Q_BLOCK, NSA_DH, Q_BLOCK), BF16)]
    return pl.pallas_call(
        _nsa_prep_kernel,
        grid=(bsz,),
        in_specs=[per_b((N_TROWS, SEQ)),
                  pl.BlockSpec((None, SEQ, LANES), lambda b: (b, 0, CB_CKS)),
                  pl.BlockSpec((None, SEQ, LANES), lambda b: (b, 0, CB_CKW)),
                  per_b((N_SEG, CMP_STRIDE * LANES)), per_b((N_SEG, CMP_STRIDE * LANES))]
                 + [full(a) for a in consts],
        out_specs=[per_b(s) for s, _ in shapes],
        out_shape=[jax.ShapeDtypeStruct((bsz,) + s, dt) for s, dt in shapes],
        compiler_params=_params("parallel"),
        name="nsa_prep",
    )(yt, y3, y3, k2, v2, *consts)


def _nsa_kernel(q_ref, ck_ref, cv_ref, ks_ref, vs_ref, kw_ref, vw_ref, bct_ref, bt_ref, bw_ref, b31_ref,
                cg_ref, o_ref, sel_s, far_s, bc_s):
    c = pl.program_id(2)
    W4 = NSA_HPG * Q_BLOCK
    q = q_ref[...]
    b31 = b31_ref[...]

    def block_mask(rows):
        m1 = jnp.concatenate([jnp.broadcast_to(r, (SEL_BLOCK, Q_BLOCK)) for r in rows], axis=0)
        return jnp.concatenate([m1] * NSA_HPG, axis=1) > 0.5

    def v_tiles(ref, first, n):
        return jnp.concatenate([ref[first + j] for j in range(n)], axis=1)

    r0 = pl.multiple_of(c * Q_BLOCK, Q_BLOCK)
    n_w = WINDOW + Q_BLOCK
    qk_cmp = _mm(ck_ref[...], q)
    qk_near = _mm(ks_ref[pl.ds(r0, 2 * Q_BLOCK), :], q)
    qk_win = _mm(kw_ref[pl.ds(r0, n_w), :], q)

    nrow = lax.broadcasted_iota(jnp.int32, (N_SEG + 16, W4), 0) - 16
    near0 = (Q_BLOCK // CMP_STRIDE) * c - 16
    bc_s[...] = jnp.where(nrow < near0, b31, NEG)
    bc_s[pl.ds(pl.multiple_of(near0 + 16, 8), CMP_NEAR), :] = bct_ref[...]
    s = qk_cmp + bc_s[16:, :]
    e = jnp.where(s > 0.1 * NEG, jnp.exp(s - jnp.max(s, axis=0, keepdims=True)), 0.0)
    p = e / jnp.maximum(jnp.sum(e, axis=0, keepdims=True), 1e-30)
    o_cmp = _mm(cv_ref[...], _bf(p))

    psum = (p[:, 0:Q_BLOCK] + p[:, Q_BLOCK:2 * Q_BLOCK] + p[:, 2 * Q_BLOCK:3 * Q_BLOCK]
            + p[:, 3 * Q_BLOCK:4 * Q_BLOCK])
    jj = lax.broadcasted_iota(jnp.int32, (N_SEL, N_SEG), 0)
    nn = lax.broadcasted_iota(jnp.int32, (N_SEL, N_SEG), 1)
    ratio = SEL_BLOCK // CMP_STRIDE
    ov = jnp.where((nn >= ratio * jj - 1) & (nn <= ratio * jj + ratio - 1) & (nn < N_SEG - 1), 1.0, 0.0)
    imp = _mm(ov, psum, HP)

    wrow = lax.broadcasted_iota(jnp.int32, (n_w, W4), 0)
    s = jnp.where(wrow >= WINDOW - c * Q_BLOCK, qk_win + bw_ref[...], NEG)
    pw = jnp.exp(s - jnp.max(s, axis=0, keepdims=True))
    o_win = _mm(v_tiles(vw_ref, c, n_w // Q_BLOCK), _bf(pw)) / jnp.sum(pw, axis=0, keepdims=True)

    jb =lax.broadcasted_iota(jnp.int32, (N_SEL, Q_BLOCK), 0)
    tq = c * Q_BLOCK + lax.broadcasted_iota(jnp.int32, (N_SEL, Q_BLOCK), 1)
    cur = tq >> 6
    imp = jnp.where((jb == 0) | (jb == cur), FORCE_SCORE, jnp.where(jb <= cur, imp, -1.0))
    rank = jnp.zeros((N_SEL, Q_BLOCK), F32)
    for jp in range(N_SEL):
        rowv = imp[jp:jp + 1, :]
        rank = rank + jnp.where(rowv > imp, 1.0, 0.0)
        if jp < N_SEL - 1:
            rank = rank + jnp.where(jb > jp, jnp.where(rowv == imp, 1.0, 0.0), 0.0)
    sel = jnp.where(rank < SEL_TOPN, 1.0, 0.0)
    sel_s[0:8, :] = jnp.zeros((8, Q_BLOCK), F32)
    sel_s[8:, :] = sel
    far_s[...] = jnp.where(jb < 2 * (c - 1), sel, 0.0)

    near = block_mask([sel_s[pl.ds(6 + 2 * c + j, 1), :] for j in range(4)])
    s = jnp.where(near, qk_near + bt_ref[...], NEG)
    m_run = jnp.max(s, axis=0, keepdims=True)
    pe = jnp.exp(s - m_run)
    l_run = jnp.sum(pe, axis=0, keepdims=True)
    acc = _mm(v_tiles(vs_ref, c, 2), _bf(pe))

    def far_step(st, carry):
        m_old, l_old, acc_old = carry
        k0 = pl.multiple_of(SEL_PAD + st * FAR_TILE, Q_BLOCK)
        b0 = pl.multiple_of(st * (FAR_TILE // SEL_BLOCK), 8)
        rows = far_s[pl.ds(b0, FAR_TILE // SEL_BLOCK), :]
        msk = block_mask([rows[j:j + 1, :] for j in range(FAR_TILE // SEL_BLOCK)])
        sc = jnp.where(msk, _mm(ks_ref[pl.ds(k0, FAR_TILE), :], q) + b31, NEG)
        m_new = jnp.maximum(m_old, jnp.max(sc, axis=0, keepdims=True))
        alpha = jnp.exp(m_old - m_new)
        pf = jnp.exp(sc - m_new)
        l_new = alpha * l_old + jnp.sum(pf, axis=0, keepdims=True)
        vt = v_tiles(vs_ref, 1 + st * (FAR_TILE // Q_BLOCK), FAR_TILE // Q_BLOCK)
        return m_new, l_new, alpha * acc_old + _mm(vt, _bf(pf))

    n_far = (c + 2) // (FAR_TILE // Q_BLOCK)
    m_run, l_run, acc = lax.fori_loop(0, n_far, far_step, (m_run, l_run, acc))
    o_sel = acc / l_run

    gates = _sigmoid(cg_ref[...])

    def gate_row(r):
        return jnp.concatenate([gates[k * 3 + r:k * 3 + r + 1, :] for k in range(NSA_HPG)], axis=1)

    res = gate_row(0) * o_cmp + gate_row(1) * o_sel + gate_row(2) * o_win
    o_ref[...] = jnp.concatenate([res[:, k * Q_BLOCK:(k + 1) * Q_BLOCK].T for k in range(NSA_HPG)], axis=1)


def _nsa_attend(qn, ck, cv, ksn, vst, kwn, vwt, tables, cgt):
    bsz = qn.shape[0]
    G = NSA_GROUPS
    W4 = NSA_HPG * Q_BLOCK
    bct, bt, bw, b31 = tables

    def per_bg(a):
        shape = a.shape[2:]
        return pl.BlockSpec((None, None) + shape, lambda b, g, c, n=len(shape): (b, g) + (0,) * n)

    def per_b(a):
        shape = a.shape[1:]
        return pl.BlockSpec((None,) + shape, lambda b, g, c, n=len(shape): (b,) + (0,) * n)

    def per_g(a):
        shape = a.shape[1:]
        return pl.BlockSpec((None,) + shape, lambda b, g, c, n=len(shape): (g,) + (0,) * n)

    return pl.pallas_call(
        _nsa_kernel,
        grid=(bsz, G, N_QBLK),
        in_specs=[
            pl.BlockSpec((None, None, None, 2 * NSA_DH, W4), lambda b, g, c: (b, g, c, 0, 0)),
            per_bg(ck), per_bg(cv), per_b(ksn), per_bg(vst), per_b(kwn), per_bg(vwt),
            per_g(bct), per_g(bt), per_g(bw), per_g(b31),
            pl.BlockSpec((None, CG_ROWS, Q_BLOCK), lambda b, g, c: (b, TR_CG // CG_ROWS + g, c)),
        ],
        out_specs=pl.BlockSpec((None, Q_BLOCK, NSA_HPG * NSA_DH), lambda b, g, c: (b, c, g)),
        out_shape=jax.ShapeDtypeStruct((bsz, SEQ, NSA_HEADS * NSA_DH), F32),
        scratch_shapes=[pltpu.VMEM((8 + N_SEL, Q_BLOCK), F32), pltpu.VMEM((N_SEL, Q_BLOCK), F32),
                        pltpu.VMEM((16 + N_SEG, W4), F32)],
        compiler_params=_params("parallel", "parallel", "arbitrary"),
        name="nsa_attend",
    )(qn, ck, cv, ksn, vst, kwn, vwt, bct, bt, bw, b31, cgt)


def _bucket_starts():
    max_exact = REL_BUCKETS // 2
    d = np.arange(4 * REL_MAX_DIST)
    large = max_exact + (np.log(np.maximum(d, 1) / max_exact) / math.log(REL_MAX_DIST / max_exact)
                         * (REL_BUCKETS - max_exact)).astype(np.int64)
    bucket = np.where(d < max_exact, d, np.minimum(large, REL_BUCKETS - 1))
    assert np.all(np.diff(bucket) >= 0) and bucket[-1] == REL_BUCKETS - 1
    return [int(np.argmax(bucket >= j)) for j in range(REL_BUCKETS)]


BUCKET_STARTS = _bucket_starts()
TABLE_ROWS = 128


def _bias_kernel(rb_ref, bct_ref, bt_ref, bw_ref):
    W4 = NSA_HPG * Q_BLOCK
    rb = rb_ref[...]

    def table(rows, r0, dist_fn, limit):
        r = r0 + lax.broadcasted_iota(jnp.int32, (rows, W4), 0)
        i = lax.broadcasted_iota(jnp.int32, (rows, W4), 1) & (Q_BLOCK - 1)
        dist = dist_fn(r, i)
        acc = jnp.broadcast_to(rb[0:1, :], (rows, W4))
        for j in range(1, REL_BUCKETS):
            acc = jnp.where(dist >= BUCKET_STARTS[j], rb[j:j + 1, :], acc)
        return jnp.where((dist < 0) | (dist >= limit), NEG, acc)

    big = 1 << 30
    bct_ref[...] = table(CMP_NEAR, 0, lambda r, i: i - CMP_STRIDE * r + (16 * CMP_STRIDE - CMP_BLOCK + 1), big)

    def near_rows(k, carry):
        r0 = pl.multiple_of(k * TABLE_ROWS, TABLE_ROWS)
        bt_ref[pl.ds(r0, TABLE_ROWS), :] = table(TABLE_ROWS, r0, lambda r, i: Q_BLOCK + i - r, big)
        return carry

    lax.fori_loop(0, 2 * Q_BLOCK // TABLE_ROWS, near_rows, 0)

    def win_rows(k, carry):
        r0 = pl.multiple_of(k * TABLE_ROWS, TABLE_ROWS)
        bw_ref[pl.ds(r0, TABLE_ROWS), :] = table(TABLE_ROWS, r0, lambda r, i: WINDOW + i - r, WINDOW)
        return carry

    lax.fori_loop(0, (WINDOW + Q_BLOCK) // TABLE_ROWS, win_rows, 0)


def _bias_tables(rel_bias):
    G, K = NSA_GROUPS, NSA_HPG
    W4 = K * Q_BLOCK
    rb = jnp.repeat(rel_bias.reshape(REL_BUCKETS, G, K).transpose(1, 0, 2), Q_BLOCK, axis=2)
    shapes = [(G, CMP_NEAR, W4), (G, 2 * Q_BLOCK, W4), (G, WINDOW + Q_BLOCK, W4)]
    bct, bt, bw = pl.pallas_call(
        _bias_kernel,
        grid=(G,),
        in_specs=[pl.BlockSpec((None, REL_BUCKETS, W4), lambda g: (g, 0, 0))],
        out_specs=[pl.BlockSpec((None,) + s[1:], lambda g: (g, 0, 0)) for s in shapes],
        out_shape=[jax.ShapeDtypeStruct(s, F32) for s in shapes],
        compiler_params=_params("parallel"),
        name="bias_tables",
    )(rb)
    return bct, bt, bw, rb[:, REL_BUCKETS - 1:, :]


def _nsa(y3, yt, q_norm, k_norm, cmp_pe, w_cmp, tables):
    bsz = y3.shape[0]
    G, DH = NSA_GROUPS, NSA_DH

    def segments(cb):
        return y3[:, :, cb * LANES:(cb + 1) * LANES].reshape(bsz, N_SEG, CMP_STRIDE * LANES)

    def both_groups(w):
        return jnp.einsum('lde,gh->lgdhe', w, jnp.eye(G, dtype=w.dtype)).reshape(CMP_STRIDE * LANES, LANES)

    def pe_row(pe):
        return jnp.tile(pe[:, None, :], (1, G, 1)).reshape(1, CMP_STRIDE * LANES)

    wk = w_cmp[0].reshape(2, CMP_STRIDE, DH, DH)
    wv = w_cmp[1].reshape(2, CMP_STRIDE, DH, DH)
    pe = cmp_pe.reshape(2, 2, CMP_STRIDE, DH)
    gain2 = lambda g: jnp.tile(g, G).reshape(1, LANES)
    consts = [_bf(both_groups(wk[0])), _bf(both_groups(wk[1])),
              _bf(both_groups(wv[0]).T), _bf(both_groups(wv[1]).T),
              pe_row(pe[0, 0]), pe_row(pe[0, 1]), pe_row(pe[1, 0]), pe_row(pe[1, 1]),
              q_norm.reshape(DH, 1), gain2(k_norm[0]), gain2(k_norm[1]), gain2(k_norm[2])]
    qn, ksn, kwn, ck, cv, vst, vwt = _nsa_prep(y3, yt, segments(CB_CKC), segments(CB_CVC), consts)
    return _nsa_attend(qn, ck, cv, ksn, vst, kwn, vwt, tables, yt)


def _merge_kernel(h_ref, g_ref, ya_ref, yb_ref, yc_ref, wg_ref, wb_ref, wo_ref, o_ref):
    x = h_ref[...]
    u = _bf(_rms_rows(x, g_ref[...]))
    acc = None
    for n, y_ref in enumerate((ya_ref, yb_ref, yc_ref)):
        t = _sigmoid(_mm(u, wg_ref[:, n * D_MODEL:(n + 1) * D_MODEL])) * _mm(_bf(y_ref[...]), wb_ref[n])
        acc = t if acc is None else acc + t
    o_ref[...] = x + _mm(_bf(acc), wo_ref[...])


def _merge(h2, gain, ya, yb, yc, wgt, wb, wo):
    m, d = h2.shape
    tm = 512
    row = lambda w: pl.BlockSpec((tm, w), lambda i: (i, 0))
    return pl.pallas_call(
        _merge_kernel,
        grid=(m // tm,),
        in_specs=[row(d), pl.BlockSpec((1, d), lambda i: (0, 0)),
                  row(BRANCH_W), row(BRANCH_W), row(BRANCH_W),
                  pl.BlockSpec((d, N_BRANCH * d), lambda i: (0, 0)),
                  pl.BlockSpec((N_BRANCH, BRANCH_W, d), lambda i: (0, 0, 0)),
                  pl.BlockSpec((d, d), lambda i: (0, 0))],
        out_specs=row(d),
        out_shape=jax.ShapeDtypeStruct((m, d), F32),
        compiler_params=_params("parallel"),
        name="merge",
    )(h2, gain, ya, yb, yc, wgt, wb, wo)


def _ffn_kernel(h_ref, g_ref, wg_ref, wu_ref, wo_ref, o_ref, u_ref, acc_ref):
    f = pl.program_id(1)

    @pl.when(f == 0)
    def _():
        x = h_ref[...]
        u_ref[...] = _bf(_rms_rows(x, g_ref[...]))
        acc_ref[...] = x

    u = u_ref[...]
    gt = _mm(u, wg_ref[...])
    up = _mm(u, wu_ref[...])
    acc_ref[...] += _mm(_bf(gt * _sigmoid(gt) * up), wo_ref[...])

    @pl.when(f == pl.num_programs(1) - 1)
    def _():
        o_ref[...] = acc_ref[...]


def _ffn(h2, gain, w_in, w_out):
    m, d = h2.shape
    tm, nf = 512, 2
    tf = D_FF // nf
    return pl.pallas_call(
        _ffn_kernel,
        grid=(m // tm, nf),
        in_specs=[pl.BlockSpec((tm, d), lambda i, f: (i, 0)),
                  pl.BlockSpec((1, d), lambda i, f: (0, 0)),
                  pl.BlockSpec((d, tf), lambda i, f: (0, f)),
                  pl.BlockSpec((d, tf), lambda i, f: (0, f + nf)),
                  pl.BlockSpec((tf, d), lambda i, f: (f, 0))],
        out_specs=pl.BlockSpec((tm, d), lambda i, f: (i, 0)),
        out_shape=jax.ShapeDtypeStruct((m, d), F32),
        scratch_shapes=[pltpu.VMEM((tm, d), BF16), pltpu.VMEM((tm, d), F32)],
        compiler_params=_params("parallel", "arbitrary"),
        name="ffn",
    )(h2, gain, w_in, w_in, w_out)


def _ple_kernel(h_ref, g_ref, p_ref, wg_ref, wp_ref, o_ref):
    x = h_ref[...]
    gate = _sigmoid(_mm(_bf(_rms_rows(x, g_ref[...])), wg_ref[...]))
    o_ref[...] = x + gate * _mm(_bf(p_ref[...]), wp_ref[...])


def _ple(h2, gain, p2, wg, wp):
    m, d = h2.shape
    tm = 512
    return pl.pallas_call(
        _ple_kernel,
        grid=(m // tm,),
        in_specs=[pl.BlockSpec((tm, d), lambda i: (i, 0)),
                  pl.BlockSpec((1, d), lambda i: (0, 0)),
                  pl.BlockSpec((tm, PLE_DIM), lambda i: (i, 0)),
                  pl.BlockSpec((d, d), lambda i: (0, 0)),
                  pl.BlockSpec((PLE_DIM, d), lambda i: (0, 0))],
        out_specs=pl.BlockSpec((tm, d), lambda i: (i, 0)),
        out_shape=jax.ShapeDtypeStruct((m, d), F32),
        compiler_params=_params("parallel"),
        name="ple",
    )(h2, gain, p2, wg, wp)


W_IN_GROUPS = ((0, 0, 16), (16, 8, 16), (32, 16, 10))
W_IN_GATES = (42, 40, 24)
W_IN_MIX_BLOCKS = 44
W_IN_SCALAR_BLOCKS = (16, 32, 42)
W_IN_SCALAR_LANES = (8, 16, 40)


def _regroup_kernel(src_ref, off_ref, a_ref, b_ref, s0_ref, s1_ref, s2_ref, o_ref, *, offsets, n_main):
    j = pl.program_id(0)
    x = jnp.concatenate([a_ref[...], b_ref[...]], axis=1)
    for off in offsets:
        @pl.when((off_ref[j] == off) & (j < n_main))
        def _(off=off):
            o_ref[...] = _bf(x[:, off:off + LANES])

    @pl.when(j == n_main)
    def _():
        lane = lax.broadcasted_iota(jnp.int32, (D_MODEL, LANES), 1)
        l0, l1, l2 = W_IN_SCALAR_LANES
        sc = jnp.where(lane < l0, s0_ref[...], jnp.where(lane < l1, s1_ref[...], s2_ref[...]))
        o_ref[...] = _bf(jnp.where(lane < l2, sc, 0.0))

    @pl.when(j > n_main)
    def _():
        o_ref[...] = jnp.zeros(o_ref.shape, BF16)


def _regroup(w_in, layer, groups, n_out, with_scalars):
    src = np.concatenate([blk + np.arange(n) for blk, _, n in groups])
    off = np.concatenate([np.full(n, o) for _, o, n in groups])
    n_main = len(src)
    pad = n_out - n_main
    src = np.concatenate([src, np.zeros(pad, np.int64)]).astype(np.int32)
    off = np.concatenate([off, np.zeros(pad, np.int64)]).astype(np.int32)
    blk = lambda f: pl.BlockSpec((None, D_MODEL, LANES), f)
    fixed = [blk(lambda j, s, o, b=b: (layer, 0, b)) for b in W_IN_SCALAR_BLOCKS]
    return pl.pallas_call(
        functools.partial(_regroup_kernel, offsets=tuple(sorted({o for _, o, _ in groups})),
                          n_main=n_main if with_scalars else n_out),
        grid_spec=pltpu.PrefetchScalarGridSpec(
            num_scalar_prefetch=2,
            grid=(n_out,),
            in_specs=[blk(lambda j, s, o: (layer, 0, s[j])), blk(lambda j, s, o: (layer, 0, s[j] + 1))] + fixed,
            out_specs=pl.BlockSpec((D_MODEL, LANES), lambda j, s, o: (0, j))),
        out_shape=jax.ShapeDtypeStruct((D_MODEL, n_out * LANES), BF16),
        compiler_params=_params("arbitrary"),
        name="regroup_w_in",
    )(jnp.asarray(src), jnp.asarray(off), w_in, w_in, w_in, w_in, w_in)


def _split_w_mix(w_mix):
    blk = lambda b, n=1: w_mix[:, b * LANES:(b + n) * LANES]
    main = jnp.concatenate([blk(0, 32), blk(36, 3), blk(40), blk(42, 2)], axis=1)
    main = jnp.pad(main, ((0, 0), (0, N_COLS - main.shape[1])))
    cg = blk(42)[:, SM_CG:SM_CG + NSA_HEADS * 3].reshape(D_MODEL, NSA_GROUPS, NSA_HPG * 3)
    cg = jnp.pad(cg, ((0, 0), (0, 0), (0, CG_ROWS - NSA_HPG * 3))).reshape(D_MODEL, NSA_GROUPS * CG_ROWS)
    return main, jnp.concatenate([blk(32, 4), blk(39), blk(41), cg], axis=1).T


def _lane_row(vals, offset):
    return jnp.zeros((1, LANES), F32).at[0, offset:offset + vals.shape[0]].set(vals)


def kernel(x, p, rel_bias, norm_mix, w_in, conv_w, gdn_a_log, gdn_dt_bias, gdn_norm, mlstm_b_i, mlstm_b_f,
           mlstm_norm, nsa_q_norm, nsa_k_norm, nsa_cmp_pe, nsa_w_cmp, w_branch, w_out, norm_ffn, w_ffn_in,
           w_ffn_out, norm_ple, w_ple_gate, w_ple_proj):
    bsz, seq, d = x.shape
    assert seq == SEQ and d == D_MODEL
    depth = w_in.shape[0]
    m = bsz * seq
    tables = _bias_tables(rel_bias)
    h2 = x.reshape(m, d)
    for l in range(depth):
        w_main, w_tr = _split_w_mix(_regroup(w_in, l, W_IN_GROUPS, W_IN_MIX_BLOCKS, True))
        w_gate = _regroup(w_in, l, (W_IN_GATES,), W_IN_GATES[2], False)
        gain = norm_mix[l].reshape(1, d)
        y3 = _in_proj(h2, gain, w_main).reshape(bsz, seq, N_COLS)
        yt = _in_proj_t(h2, gain, w_tr, bsz)
        ya = _gdn(y3, conv_w[l], _lane_row(gdn_a_log[l], SM_AA), _lane_row(gdn_dt_bias[l], SM_AA),
                  gdn_norm[l].reshape(1, D_HEAD_REC))
        yb = _mlstm(y3, _lane_row(mlstm_b_i[l], SM_BI), _lane_row(mlstm_b_f[l], SM_BF),
                    mlstm_norm[l].reshape(1, D_HEAD_REC))
        yc = _nsa(y3, yt, nsa_q_norm[l], nsa_k_norm[l], nsa_cmp_pe[l], nsa_w_cmp[l], tables)
        h2 = _merge(h2, gain, ya.reshape(m, BRANCH_W), yb.reshape(m, BRANCH_W),
                    yc.reshape(m, BRANCH_W), w_gate, _bf(w_branch[l]), _bf(w_out[l]))
        h2 = _ffn(h2, norm_ffn[l].reshape(1, d), _bf(w_ffn_in[l]), _bf(w_ffn_out[l]))
        h2 = _ple(h2, norm_ple[l].reshape(1, d), p[l].reshape(m, PLE_DIM), _bf(w_ple_gate[l]),
                  _bf(w_ple_proj[l]))
    return h2.reshape(bsz, seq, d)
```

```python
import functools
import math

import numpy as np
import jax
import jax.numpy as jnp
from jax import lax
from jax.experimental import pallas as pl
from jax.experimental.pallas import tpu as pltpu

D_MODEL = 1024
SEQ = 2048
N_HEADS_REC = 4
D_HEAD_REC = 128
CHUNK = 64
GDN_HPS = 2
GDN_CONV = 4
NSA_HEADS = 8
NSA_GROUPS = 2
NSA_HPG = NSA_HEADS // NSA_GROUPS
NSA_DH = 64
CMP_BLOCK = 32
CMP_STRIDE = 16
SEL_BLOCK = 64
SEL_TOPN = 4
WINDOW = 512
Q_BLOCK = 128
N_QBLK = SEQ // Q_BLOCK
N_SEL = SEQ // SEL_BLOCK
N_SEG = SEQ // CMP_STRIDE
SEL_PAD = Q_BLOCK
FAR_TILE = 4 * Q_BLOCK
CMP_NEAR = 24
REL_BUCKETS = 32
REL_MAX_DIST = 128
N_BRANCH = 3
BRANCH_W = 512
D_FF = 2816
PLE_DIM = 256
EPS = 1e-6
NEG = -1e30
FORCE_SCORE = 1e4

LANES = 128
VMEM_LIMIT = 48 * 1024 * 1024

F32 = jnp.float32
BF16 = jnp.bfloat16
HP = lax.Precision.HIGHEST

CB_A = 0
CB_B = 16
CB_CKC, CB_CVC, CB_CKS, CB_CKW = 32, 33, 34, 35
CB_SMALL = 36
N_COLS = 40 * LANES
TR_Q = 0
TR_VS = NSA_HEADS * NSA_DH
TR_VW = TR_VS + NSA_GROUPS * NSA_DH
TR_CG = TR_VW + NSA_GROUPS * NSA_DH
CG_ROWS = 16
N_TROWS = TR_CG + NSA_GROUPS * CG_ROWS
SM_AA, SM_AB, SM_BI, SM_BF, SM_CG = 0, 4, 8, 12, 16


def _mm(a, b, precision=None):
    return lax.dot_general(a, b, (((1,), (0,)), ((), ())), precision=precision,
                           preferred_element_type=F32)


def _mm_nt(a, b, precision=None):
    return lax.dot_general(a, b, (((1,), (1,)), ((), ())), precision=precision,
                           preferred_element_type=F32)


def _mm_tn(a, b, precision=None):
    return lax.dot_general(a, b, (((0,), (0,)), ((), ())), precision=precision,
                           preferred_element_type=F32)


def _bf(x):
    return x.astype(BF16)


def _sigmoid(x):
    return 1.0 / (1.0 + jnp.exp(-x))


def _softplus(x):
    return jnp.maximum(x, 0.0) + jnp.log1p(jnp.exp(-jnp.abs(x)))


def _rms_rows(x, g):
    return x * lax.rsqrt(jnp.mean(x * x, axis=-1, keepdims=True) + EPS) * g


def _params(*sem):
    return pltpu.CompilerParams(dimension_semantics=sem, vmem_limit_bytes=VMEM_LIMIT)


def _proj_kernel(x_ref, g_ref, w_ref, o_ref, u_ref):
    @pl.when(pl.program_id(1) == 0)
    def _():
        u_ref[...] = _bf(_rms_rows(x_ref[...], g_ref[...]))

    o_ref[...] = _mm(u_ref[...], w_ref[...])


def _in_proj(x2, gain, w):
    m, d = x2.shape
    n = w.shape[1]
    tm, tn = 1024, n // 4
    return pl.pallas_call(
        _proj_kernel,
        grid=(m // tm, n // tn),
        in_specs=[pl.BlockSpec((tm, d), lambda i, j: (i, 0)),
                  pl.BlockSpec((1, d), lambda i, j: (0, 0)),
                  pl.BlockSpec((d, tn), lambda i, j: (0, j))],
        out_specs=pl.BlockSpec((tm, tn), lambda i, j: (i, j)),
        out_shape=jax.ShapeDtypeStruct((m, n), F32),
        scratch_shapes=[pltpu.VMEM((tm, d), BF16)],
        compiler_params=_params("parallel", "arbitrary"),
        name="in_proj",
    )(x2, gain, w)


def _proj_t_kernel(x_ref, g_ref, wt_ref, o_ref):
    o_ref[...] = _mm_nt(wt_ref[...], _bf(_rms_rows(x_ref[...], g_ref[...])))


def _in_proj_t(x2, gain, wt, bsz):
    m, d = x2.shape
    n = wt.shape[0]
    ts = 512
    nt = SEQ // ts
    return pl.pallas_call(
        _proj_t_kernel,
        grid=(bsz, nt),
        in_specs=[pl.BlockSpec((ts, d), lambda b, s: (b * nt + s, 0)),
                  pl.BlockSpec((1, d), lambda b, s: (0, 0)),
                  pl.BlockSpec((n, d), lambda b, s: (0, 0))],
        out_specs=pl.BlockSpec((None, n, ts), lambda b, s: (b, 0, s)),
        out_shape=jax.ShapeDtypeStruct((bsz, n, SEQ), F32),
        compiler_params=_params("parallel", "parallel"),
        name="in_proj_t",
    )(x2, gain, wt)


def _chunk_masks():
    ri = lax.broadcasted_iota(jnp.int32, (CHUNK, CHUNK), 0)
    ci = lax.broadcasted_iota(jnp.int32, (CHUNK, CHUNK), 1)
    return ri, ci


def _chunk_cumsum(x):
    rowi = lax.broadcasted_iota(jnp.int32, x.shape, 0)
    s = 1
    while s < CHUNK:
        x = x + jnp.where(rowi >= s, pltpu.roll(x, s, axis=0), 0.0)
        s *= 2
    return x


def _pick_lane(x, lane_idx):
    lane = lax.broadcasted_iota(jnp.int32, x.shape, 1)
    return jnp.sum(jnp.where(lane == lane_idx, x, 0.0), axis=1, keepdims=True)


def _col_to_row(col, eye):
    return jnp.sum(jnp.where(eye, col, 0.0), axis=0, keepdims=True)


def _bd_groups(pk):
    grp = lax.broadcasted_iota(jnp.int32, pk.shape, 1) // CHUNK
    zero = jnp.zeros_like(pk)
    return jnp.concatenate([jnp.where(grp == j, pk, zero) for j in range(pk.shape[1] // CHUNK)], axis=0)


def _bd_wide(x):
    zero = jnp.zeros((x.shape[0], LANES), x.dtype)
    return jnp.concatenate([jnp.concatenate([x[:, :LANES], zero], axis=1),
                            jnp.concatenate([zero, x[:, LANES:]], axis=1)], axis=0)


def _inv_unit_lower(x, bd, eyef):
    xd = jnp.where(bd, x, 0.0)
    xo = jnp.where(bd, 0.0, x)
    mm = lambda a, b: _mm(_bf(a), _bd_groups(_bf(b)))

    def mm2(a1, a2, b):
        r = mm(jnp.concatenate([a1, a2], axis=0), b)
        return r[:CHUNK], r[CHUNK:]

    x2 = mm(xd, xd)
    p = eyef - xd
    x4, t = mm2(x2, p, x2)
    p = p + t
    x8, t = mm2(x4, p, x4)
    p = p + t
    p = p + mm(p, x8)
    m = mm(p, xo)
    m2 = mm(m, m)
    q = eyef - m
    q = q + mm(q, m2)
    return mm(q, p)


def _gdn_kernel(q_ref, k_ref, v_ref, z_ref, sm_ref, cwq_ref, cwk_ref, cwv_ref, alog_ref, dtb_ref,
                nw_ref, o_ref, qg_s, kd_s, u_s, w_s, at_s, eg_s):
    L = CHUNK
    W = 2 * LANES
    ri = lax.broadcasted_iota(jnp.int32, (L, 4 * L), 0)
    cp = lax.broadcasted_iota(jnp.int32, (L, 4 * L), 1)
    ci = cp & (L - 1)
    first = lax.broadcasted_iota(jnp.int32, (L, 2 * L), 1) < L
    tril = ri >= ci
    strict = ri > ci
    eye = ri == ci
    bd = (ri >> 4) == (ci >> 4)
    eyef = jnp.where(eye, 1.0, 0.0)
    alog = alog_ref[...]
    dtb = dtb_ref[...]
    nw = nw_ref[...]
    cwq, cwk, cwv = cwq_ref[...], cwk_ref[...], cwv_ref[...]
    ha = pl.program_id(1) * 2

    def conv_silu(ref, w, n, r0):
        cur = ref[pl.ds(r0, L), :]
        p0 = pl.multiple_of(jnp.maximum(r0 - 8, 0), 8)
        prev = jnp.where(n > 0, ref[pl.ds(p0, 8), :], 0.0)
        win = jnp.concatenate([prev, cur], axis=0)
        acc = cur * w[GDN_CONV - 1:GDN_CONV, :]
        for s in range(1, GDN_CONV):
            acc = acc + pltpu.roll(win, s, axis=0)[8:, :] * w[GDN_CONV - 1 - s:GDN_CONV - s, :]
        return acc * _sigmoid(acc)

    def wide(col_a, col_b, rows=L):
        return jnp.concatenate([jnp.broadcast_to(col_a, (rows, LANES)),
                                jnp.broadcast_to(col_b, (rows, LANES))], axis=1)

    def l2n(x):
        xx = x * x
        return x * wide(lax.rsqrt(jnp.sum(xx[:, :LANES], axis=-1, keepdims=True) + 1e-6),
                        lax.rsqrt(jnp.sum(xx[:, LANES:], axis=-1, keepdims=True) + 1e-6))

    def chunk_terms(n):
        r0 = pl.multiple_of(n * L, L)
        x = sm_ref[pl.ds(r0, L), :]
        g = _chunk_cumsum(-jnp.exp(alog) * _softplus(x + dtb))
        sx = _sigmoid(x)
        q = l2n(conv_silu(q_ref, cwq, n, r0)) * (D_HEAD_REC ** -0.5)
        k = l2n(conv_silu(k_ref, cwk, n, r0))
        v = conv_silu(v_ref, cwv, n, r0)
        gc_a, gc_b = _pick_lane(g, SM_AA + ha), _pick_lane(g, SM_AA + ha + 1)
        beta = wide(_pick_lane(sx, SM_AB + ha), _pick_lane(sx, SM_AB + ha + 1))
        kb = k * beta
        egc = wide(jnp.exp(gc_a), jnp.exp(gc_b))
        gl_a, gl_b = gc_a[L - 1:L, :], gc_b[L - 1:L, :]
        qg_s[pl.ds(r0, L), :] = _bf(q * egc)
        kd_s[pl.ds(r0, L), :] = _bf(k * wide(jnp.exp(gl_a - gc_a), jnp.exp(gl_b - gc_b)))
        eg_s[pl.ds(pl.multiple_of(n * 8, 8), 8), :] = wide(jnp.exp(gl_a), jnp.exp(gl_b), 8)
        gc_pk = jnp.where(first, gc_a, gc_b)
        return q, _bf(k), kb, _bd_wide(_bf(v * beta)), _bd_wide(_bf(kb * egc)), gc_pk

    zk = jnp.zeros((2 * L, W), BF16)

    def prep(i, carry):
        n0 = 2 * i
        r0 = pl.multiple_of(n0 * L, 2 * L)
        q0, k0, kb0, vb0, ke0, gc0 = chunk_terms(n0)
        q1, k1, kb1, vb1, ke1, gc1 = chunk_terms(n0 + 1)
        gc = jnp.concatenate([gc0, gc1], axis=1)
        diff = gc - jnp.sum(jnp.where(eye, gc, 0.0), axis=0, keepdims=True)
        decay = jnp.where(tril, jnp.exp(jnp.where(tril, diff, 0.0)), 0.0)
        lhs = jnp.concatenate([jnp.concatenate([kb0, kb1], axis=1),
                               jnp.concatenate([q0, q1], axis=1)], axis=0)
        rhs = jnp.concatenate([jnp.concatenate([_bd_wide(k0), zk], axis=1),
                               jnp.concatenate([zk, _bd_wide(k1)], axis=1)], axis=0)
        kq = _mm_nt(_bf(lhs), rhs)
        xm = jnp.where(strict, kq[:L] * decay, 0.0)
        t = _bf(_inv_unit_lower(xm, bd, eyef))
        attn = _bf(kq[L:] * decay)
        uw = _mm(t, jnp.concatenate([jnp.concatenate([vb0, zk, ke0, zk], axis=1),
                                     jnp.concatenate([zk, vb1, zk, ke1], axis=1)], axis=0))
        for j in range(2):
            rows = pl.ds(r0 + j * L, L)
            u_s[rows, :] = uw[:, j * W:(j + 1) * W]
            w_s[rows, :] = _bf(uw[:, (2 + j) * W:(3 + j) * W])
            at_s[rows, :] = attn[:, j * 2 * L:(j + 1) * 2 * L]
        return carry

    lax.fori_loop(0, SEQ // (2 * L), prep, 0, unroll=2)

    zs = jnp.zeros((D_HEAD_REC, D_HEAD_REC), BF16)

    def step(n, states):
        sa, sb = states
        r0 = pl.multiple_of(n * L, L)
        sbd = jnp.concatenate([jnp.concatenate([_bf(sa), zs], axis=1),
                               jnp.concatenate([zs, _bf(sb)], axis=1)], axis=0)
        eg = eg_s[pl.ds(pl.multiple_of(n * 8, 8), 1), :]
        r = _mm(jnp.concatenate([w_s[pl.ds(r0, L), :], qg_s[pl.ds(r0, L), :]], axis=0), sbd)
        vnb = _bf(u_s[pl.ds(r0, L), :] - r[:L])
        o = r[L:] + _mm(at_s[pl.ds(r0, L), :], _bd_wide(vnb))
        upd = _mm_tn(kd_s[pl.ds(r0, L), :], vnb)
        sa = sa * eg[:, :LANES] + upd[:LANES, :LANES]
        sb = sb * eg[:, LANES:] + upd[LANES:, LANES:]
        z = z_ref[pl.ds(r0, L), :]
        on = jnp.concatenate([_rms_rows(o[:, :LANES], nw), _rms_rows(o[:, LANES:], nw)], axis=1)
        o_ref[pl.ds(r0, L), :] = on * (z * _sigmoid(z))
        return sa, sb

    zero = jnp.zeros((D_HEAD_REC, D_HEAD_REC), F32)
    lax.fori_loop(0, SEQ // L, step, (zero, zero))


def _gdn(y3, conv_w, alog_row, dtb_row, norm_w):
    bsz = y3.shape[0]
    H = N_HEADS_REC
    W = 2 * LANES

    def col(off):
        return pl.BlockSpec((None, SEQ, W), lambda b, h, off=off: (b, 0, off // 2 + h))

    def cw(off):
        return pl.BlockSpec((GDN_CONV, W), lambda b, h, off=off: (0, off // 2 + h))

    row = pl.BlockSpec((1, LANES), lambda b, h: (0, 0))
    return pl.pallas_call(
        _gdn_kernel,
        grid=(bsz, H // 2),
        in_specs=[col(CB_A), col(CB_A + H), col(CB_A + 2 * H), col(CB_A + 3 * H),
                  pl.BlockSpec((None, SEQ, LANES), lambda b, h: (b, 0, CB_SMALL)),
                  cw(0), cw(H), cw(2 * H), row, row, row],
        out_specs=pl.BlockSpec((None, SEQ, W), lambda b, h: (b, 0, h)),
        out_shape=jax.ShapeDtypeStruct((bsz, SEQ, H * D_HEAD_REC), F32),
        scratch_shapes=[pltpu.VMEM((SEQ, W), BF16), pltpu.VMEM((SEQ, W), BF16),
                        pltpu.VMEM((SEQ, W), F32), pltpu.VMEM((SEQ, W), BF16),
                        pltpu.VMEM((SEQ, 2 * CHUNK), BF16),
                        pltpu.VMEM((SEQ // CHUNK * 8, W), F32)],
        compiler_params=_params("parallel", "parallel"),
        name="gdn",
    )(y3, y3, y3, y3, y3, conv_w, conv_w, conv_w, alog_row, dtb_row, norm_w)


def _mlstm_kernel(q_ref, k_ref, v_ref, og_ref, sm_ref, bi_ref, bf_ref, nw_ref, o_ref):
    L = CHUNK
    ri = lax.broadcasted_iota(jnp.int32, (L, 2 * L), 0)
    cp = lax.broadcasted_iota(jnp.int32, (L, 2 * L), 1)
    ci = cp & (L - 1)
    first = cp < L
    tril = ri >= ci
    eye = ri == ci
    bi = bi_ref[...]
    bfr = bf_ref[...]
    nw = nw_ref[...]
    ha = pl.program_id(1) * 2
    zs = jnp.zeros((D_HEAD_REC, D_HEAD_REC), BF16)

    def wide(col_a, col_b, rows=L):
        return jnp.concatenate([jnp.broadcast_to(col_a, (rows, LANES)),
                                jnp.broadcast_to(col_b, (rows, LANES))], axis=1)

    def step(n, carry):
        c_a, c_b, n_st, m_a, m_b = carry
        r0 = pl.multiple_of(n * L, L)
        q = q_ref[pl.ds(r0, L), :]
        k = k_ref[pl.ds(r0, L), :] * (D_HEAD_REC ** -0.5)
        vb = _bf(v_ref[pl.ds(r0, L), :])
        x = sm_ref[pl.ds(r0, L), :]
        lfc = _chunk_cumsum(-_softplus(-(x + bfr)))
        itx = x + bi
        b_a, b_b = _pick_lane(lfc, SM_BF + ha), _pick_lane(lfc, SM_BF + ha + 1)
        it_a, it_b = _pick_lane(itx, SM_BI + ha), _pick_lane(itx, SM_BI + ha + 1)
        b_pk = jnp.where(first, b_a, b_b)
        it_pk = jnp.where(first, it_a, it_b)
        b_row = jnp.sum(jnp.where(eye, b_pk, 0.0), axis=0, keepdims=True)
        it_row = jnp.sum(jnp.where(eye, it_pk, 0.0), axis=0, keepdims=True)
        dm = jnp.where(tril, b_pk - b_row + it_row, NEG)
        dmax_a = jnp.max(jnp.where(first, dm, NEG), axis=1, keepdims=True)
        dmax_b = jnp.max(jnp.where(first, NEG, dm), axis=1, keepdims=True)
        qk = _mm_nt(_bf(q), _bd_wide(_bf(k)))
        a_a, a_b = b_a + m_a, b_b + m_b
        mt_a, mt_b = jnp.maximum(a_a, dmax_a), jnp.maximum(a_b, dmax_b)
        sm = jnp.exp(dm - jnp.where(first, mt_a, mt_b)) * qk
        si_a, si_b = jnp.exp(a_a - mt_a), jnp.exp(a_b - mt_b)
        cbd = jnp.concatenate([jnp.concatenate([_bf(c_a), zs], axis=1),
                               jnp.concatenate([zs, _bf(c_b)], axis=1)], axis=0)
        num = _mm(jnp.concatenate([_bf(q * wide(si_a, si_b)), _bf(sm)], axis=1),
                  jnp.concatenate([cbd, _bd_wide(vb)], axis=0))
        qn = q * n_st
        den_a = (si_a * jnp.sum(qn[:, :LANES], axis=1, keepdims=True)
                 + jnp.sum(jnp.where(first, sm, 0.0), axis=1, keepdims=True))
        den_b = (si_b * jnp.sum(qn[:, LANES:], axis=1, keepdims=True)
                 + jnp.sum(jnp.where(first, 0.0, sm), axis=1, keepdims=True))
        hh = num / wide(jnp.maximum(jnp.abs(den_a), jnp.exp(-mt_a)),
                        jnp.maximum(jnp.abs(den_b), jnp.exp(-mt_b)))
        bl_a, bl_b = b_a[L - 1:L, :], b_b[L - 1:L, :]
        ds_a, ds_b = bl_a - b_a + it_a, bl_b - b_b + it_b
        mn_a = jnp.maximum(bl_a + m_a, jnp.max(ds_a, axis=0, keepdims=True))
        mn_b = jnp.maximum(bl_b + m_b, jnp.max(ds_b, axis=0, keepdims=True))
        wk = k * wide(jnp.exp(ds_a - mn_a), jnp.exp(ds_b - mn_b))
        sc_a, sc_b = jnp.exp(bl_a + m_a - mn_a), jnp.exp(bl_b + m_b - mn_b)
        upd = _mm_tn(_bf(wk), vb)
        c_a = sc_a * c_a + upd[:LANES, :LANES]
        c_b = sc_b * c_b + upd[LANES:, LANES:]
        n_st = wide(sc_a, sc_b, 1) * n_st + jnp.sum(wk, axis=0, keepdims=True)
        on = jnp.concatenate([_rms_rows(hh[:, :LANES], nw), _rms_rows(hh[:, LANES:], nw)], axis=1)
        o_ref[pl.ds(r0, L), :] = on * _sigmoid(og_ref[pl.ds(r0, L), :])
        return c_a, c_b, n_st, mn_a, mn_b

    zc = jnp.zeros((D_HEAD_REC, D_HEAD_REC), F32)
    z1 = jnp.zeros((1, 1), F32)
    lax.fori_loop(0, SEQ // L, step, (zc, zc, jnp.zeros((1, 2 * LANES), F32), z1, z1))


def _mlstm(y3, bi_row, bf_row, norm_w):
    bsz = y3.shape[0]
    H = N_HEADS_REC
    W = 2 * LANES

    def col(off):
        return pl.BlockSpec((None, SEQ, W), lambda b, h, off=off: (b, 0, off // 2 + h))

    row = pl.BlockSpec((1, LANES), lambda b, h: (0, 0))
    return pl.pallas_call(
        _mlstm_kernel,
        grid=(bsz, H // 2),
        in_specs=[col(CB_B), col(CB_B + H), col(CB_B + 2 * H), col(CB_B + 3 * H),
                  pl.BlockSpec((None, SEQ, LANES), lambda b, h: (b, 0, CB_SMALL)),
                  row, row, row],
        out_specs=pl.BlockSpec((None, SEQ, W), lambda b, h: (b, 0, h)),
        out_shape=jax.ShapeDtypeStruct((bsz, SEQ, H * D_HEAD_REC), F32),
        compiler_params=_params("parallel", "parallel"),
        name="mlstm",
    )(y3, y3, y3, y3, y3, bi_row, bf_row, norm_w)


def _nsa_prep_kernel(yt_ref, ks_ref, kw_ref, k2_ref, v2_ref, wkl_ref, wkh_ref, wvl_ref, wvh_ref,
                     pekl_ref, pekh_ref, pevl_ref, pevh_ref, qn_ref, kn0_ref, kn1_ref, kn2_ref,
                     qn_out, ks_out, kw_out, ck_out, cv_out, vs_out, vw_out):
    G, K, DH = NSA_GROUPS, NSA_HPG, NSA_DH
    lane = lax.broadcasted_iota(jnp.int32, (Q_BLOCK, LANES), 1)
    lo = lane < DH

    def half_ms(x):
        xx = x * x
        s0 = jnp.sum(jnp.where(lo, xx, 0.0), axis=1, keepdims=True)
        s1 = jnp.sum(jnp.where(lo, 0.0, xx), axis=1, keepdims=True)
        return jnp.where(lo, s0, s1) * (1.0 / DH)

    k2 = k2_ref[...]
    v2 = v2_ref[...]
    ckb = (_mm(_bf(k2 + pekl_ref[...]), wkl_ref[...])
           + pltpu.roll(_mm(_bf(k2 + pekh_ref[...]), wkh_ref[...]), N_SEG - 1, axis=0))
    ckn = ckb * lax.rsqrt(half_ms(ckb) + EPS) * kn0_ref[...]
    ck_out[0] = _bf(jnp.where(lo, ckn, 0.0))
    ck_out[1] = _bf(jnp.where(lo, 0.0, ckn))
    cvb = (_mm_nt(wvl_ref[...], _bf(v2 + pevl_ref[...]))
           + pltpu.roll(_mm_nt(wvh_ref[...], _bf(v2 + pevh_ref[...])), N_SEG - 1, axis=1))

    qg = qn_ref[...] * (DH ** -0.5)
    zq = jnp.zeros((DH, K * Q_BLOCK), F32)
    zv = jnp.zeros((DH, Q_BLOCK), BF16)
    for g in range(G):
        cv_out[g] = _bf(cvb[g * DH:(g + 1) * DH, :])
        for c in range(N_QBLK):
            cs = slice(c * Q_BLOCK, (c + 1) * Q_BLOCK)
            r0 = TR_Q + g * K * DH
            x = jnp.concatenate([yt_ref[r0 + k * DH:r0 + (k + 1) * DH, cs] for k in range(K)], axis=1)
            xn = x * lax.rsqrt(jnp.mean(x * x, axis=0, keepdims=True) + EPS) * qg
            parts = [xn, zq] if g == 0 else [zq, xn]
            qn_out[g, c] = _bf(jnp.concatenate(parts, axis=0))
            vs_out[g, SEL_PAD // Q_BLOCK + c] = _bf(yt_ref[TR_VS + g * DH:TR_VS + (g + 1) * DH, cs])
            vw_out[g, WINDOW // Q_BLOCK + c] = _bf(yt_ref[TR_VW + g * DH:TR_VW + (g + 1) * DH, cs])
        for j in range(SEL_PAD // Q_BLOCK):
            vs_out[g, j] = zv
        for j in range(WINDOW // Q_BLOCK):
            vw_out[g, j] = zv

    def norm_keys(src, dst, gain, pad):
        dst[0:pad, :] = jnp.zeros((pad, LANES), BF16)

        def body(i, carry):
            r0 = pl.multiple_of(i * Q_BLOCK, Q_BLOCK)
            x = src[pl.ds(r0, Q_BLOCK), :]
            dst[pl.ds(pad + r0, Q_BLOCK), :] = _bf(x * lax.rsqrt(half_ms(x) + EPS) * gain)
            return carry
        lax.fori_loop(0, N_QBLK, body, 0)

    norm_keys(ks_ref, ks_out, kn1_ref[...], SEL_PAD)
    norm_keys(kw_ref, kw_out, kn2_ref[...], WINDOW)


def _nsa_prep(y3, yt, k2, v2, consts):
    bsz = y3.shape[0]
    G = NSA_GROUPS
    W4 = NSA_HPG * Q_BLOCK

    def full(a):
        nd = a.ndim
        return pl.BlockSpec(a.shape, lambda b, nd=nd: (0,) * nd)

    def per_b(shape):
        nd = len(shape)
        return pl.BlockSpec((None,) + shape, lambda b, nd=nd: (b,) + (0,) * nd)

    shapes = [((G, N_QBLK, 2 * NSA_DH, W4), BF16),
              ((SEL_PAD + SEQ, LANES), BF16), ((WINDOW + SEQ, LANES), BF16),
              ((G, N_SEG, LANES), BF16), ((G, NSA_DH, N_SEG), BF16),
              ((G, (SEL_PAD + SEQ) // Q_BLOCK, NSA_DH, Q_BLOCK), BF16),
              ((G, (WINDOW + SEQ) // Q_BLOCK, NSA_DH, Q_BLOCK), BF16)]
    return pl.pallas_call(
        _nsa_prep_kernel,
        grid=(bsz,),
        in_specs=[per_b((N_TROWS, SEQ)),
                  pl.BlockSpec((None, SEQ, LANES), lambda b: (b, 0, CB_CKS)),
                  pl.BlockSpec((None, SEQ, LANES), lambda b: (b, 0, CB_CKW)),
                  per_b((N_SEG, CMP_STRIDE * LANES)), per_b((N_SEG, CMP_STRIDE * LANES))]
                 + [full(a) for a in consts],
        out_specs=[per_b(s) for s, _ in shapes],
        out_shape=[jax.ShapeDtypeStruct((bsz,) + s, dt) for s, dt in shapes],
        compiler_params=_params("parallel"),
        name="nsa_prep",
    )(yt, y3, y3, k2, v2, *consts)


def _nsa_kernel(q_ref, ck_ref, cv_ref, ks_ref, vs_ref, kw_ref, vw_ref, bct_ref, bt_ref, bw_ref, b31_ref,
                cg_ref, o_ref, sel_s, far_s, bc_s):
    c = pl.program_id(2)
    W4 = NSA_HPG * Q_BLOCK
    q = q_ref[...]
    b31 = b31_ref[...]

    def block_mask(rows):
        m1 = jnp.concatenate([jnp.broadcast_to(r, (SEL_BLOCK, Q_BLOCK)) for r in rows], axis=0)
        return jnp.concatenate([m1] * NSA_HPG, axis=1) > 0.5

    def v_tiles(ref, first, n):
        return jnp.concatenate([ref[first + j] for j in range(n)], axis=1)

    r0 = pl.multiple_of(c * Q_BLOCK, Q_BLOCK)
    n_w = WINDOW + Q_BLOCK
    qk_cmp = _mm(ck_ref[...], q)
    qk_near = _mm(ks_ref[pl.ds(r0, 2 * Q_BLOCK), :], q)
    qk_win = _mm(kw_ref[pl.ds(r0, n_w), :], q)

    nrow = lax.broadcasted_iota(jnp.int32, (N_SEG + 16, W4), 0) - 16
    near0 = (Q_BLOCK // CMP_STRIDE) * c - 16
    bc_s[...] = jnp.where(nrow < near0, b31, NEG)
    bc_s[pl.ds(pl.multiple_of(near0 + 16, 8), CMP_NEAR), :] = bct_ref[...]
    s = qk_cmp + bc_s[16:, :]
    e = jnp.where(s > 0.1 * NEG, jnp.exp(s - jnp.max(s, axis=0, keepdims=True)), 0.0)
    p = e / jnp.maximum(jnp.sum(e, axis=0, keepdims=True), 1e-30)
    o_cmp = _mm(cv_ref[...], _bf(p))

    psum = (p[:, 0:Q_BLOCK] + p[:, Q_BLOCK:2 * Q_BLOCK] + p[:, 2 * Q_BLOCK:3 * Q_BLOCK]
            + p[:, 3 * Q_BLOCK:4 * Q_BLOCK])
    jj = lax.broadcasted_iota(jnp.int32, (N_SEL, N_SEG), 0)
    nn = lax.broadcasted_iota(jnp.int32, (N_SEL, N_SEG), 1)
    ratio = SEL_BLOCK // CMP_STRIDE
    ov = jnp.where((nn >= ratio * jj - 1) & (nn <= ratio * jj + ratio - 1) & (nn < N_SEG - 1), 1.0, 0.0)
    imp = _mm(ov, psum, HP)

    wrow = lax.broadcasted_iota(jnp.int32, (n_w, W4), 0)
    s = jnp.where(wrow >= WINDOW - c * Q_BLOCK, qk_win + bw_ref[...], NEG)
    pw = jnp.exp(s - jnp.max(s, axis=0, keepdims=True))
    o_win = _mm(v_tiles(vw_ref, c, n_w // Q_BLOCK), _bf(pw)) / jnp.sum(pw, axis=0, keepdims=True)

    jb =lax.broadcasted_iota(jnp.int32, (N_SEL, Q_BLOCK), 0)
    tq = c * Q_BLOCK + lax.broadcasted_iota(jnp.int32, (N_SEL, Q_BLOCK), 1)
    cur = tq >> 6
    imp = jnp.where((jb == 0) | (jb == cur), FORCE_SCORE, jnp.where(jb <= cur, imp, -1.0))
    rank = jnp.zeros((N_SEL, Q_BLOCK), F32)
    for jp in range(N_SEL):
        rowv = imp[jp:jp + 1, :]
        rank = rank + jnp.where(rowv > imp, 1.0, 0.0)
        if jp < N_SEL - 1:
            rank = rank + jnp.where(jb > jp, jnp.where(rowv == imp, 1.0, 0.0), 0.0)
    sel = jnp.where(rank < SEL_TOPN, 1.0, 0.0)
    sel_s[0:8, :] = jnp.zeros((8, Q_BLOCK), F32)
    sel_s[8:, :] = sel
    far_s[...] = jnp.where(jb < 2 * (c - 1), sel, 0.0)

    near = block_mask([sel_s[pl.ds(6 + 2 * c + j, 1), :] for j in range(4)])
    s = jnp.where(near, qk_near + bt_ref[...], NEG)
    m_run = jnp.max(s, axis=0, keepdims=True)
    pe = jnp.exp(s - m_run)
    l_run = jnp.sum(pe, axis=0, keepdims=True)
    acc = _mm(v_tiles(vs_ref, c, 2), _bf(pe))

    def far_step(st, carry):
        m_old, l_old, acc_old = carry
        k0 = pl.multiple_of(SEL_PAD + st * FAR_TILE, Q_BLOCK)
        b0 = pl.multiple_of(st * (FAR_TILE // SEL_BLOCK), 8)
        rows = far_s[pl.ds(b0, FAR_TILE // SEL_BLOCK), :]
        msk = block_mask([rows[j:j + 1, :] for j in range(FAR_TILE // SEL_BLOCK)])
        sc = jnp.where(msk, _mm(ks_ref[pl.ds(k0, FAR_TILE), :], q) + b31, NEG)
        m_new = jnp.maximum(m_old, jnp.max(sc, axis=0, keepdims=True))
        alpha = jnp.exp(m_old - m_new)
        pf = jnp.exp(sc - m_new)
        l_new = alpha * l_old + jnp.sum(pf, axis=0, keepdims=True)
        vt = v_tiles(vs_ref, 1 + st * (FAR_TILE // Q_BLOCK), FAR_TILE // Q_BLOCK)
        return m_new, l_new, alpha * acc_old + _mm(vt, _bf(pf))

    n_far = (c + 2) // (FAR_TILE // Q_BLOCK)
    m_run, l_run, acc = lax.fori_loop(0, n_far, far_step, (m_run, l_run, acc))
    o_sel = acc / l_run

    gates = _sigmoid(cg_ref[...])

    def gate_row(r):
        return jnp.concatenate([gates[k * 3 + r:k * 3 + r + 1, :] for k in range(NSA_HPG)], axis=1)

    res = gate_row(0) * o_cmp + gate_row(1) * o_sel + gate_row(2) * o_win
    o_ref[...] = jnp.concatenate([res[:, k * Q_BLOCK:(k + 1) * Q_BLOCK].T for k in range(NSA_HPG)], axis=1)


def _nsa_kernel2(q_ref, ck_ref, cv_ref, ks_ref, vs_ref, kw_ref, vw_ref, bct_ref, bt_ref, bw_ref, b31_ref,
                 cg_ref, o_ref, sel_s, far_s, bc_s):
    c = pl.program_id(1)
    G = NSA_GROUPS
    W4 = NSA_HPG * Q_BLOCK
    r0 = pl.multiple_of(c * Q_BLOCK, Q_BLOCK)
    n_w = WINDOW + Q_BLOCK
    far_blocks = FAR_TILE // SEL_BLOCK
    far_tiles = FAR_TILE // Q_BLOCK

    def block_mask(rows):
        m1 = jnp.concatenate([jnp.broadcast_to(r, (SEL_BLOCK, Q_BLOCK)) for r in rows], axis=0)
        return jnp.concatenate([m1] * NSA_HPG, axis=1) > 0.5

    def v_tiles(ref, g, first, n):
        return jnp.concatenate([ref[g, first + j] for j in range(n)], axis=1)

    jb = lax.broadcasted_iota(jnp.int32, (N_SEL, Q_BLOCK), 0)
    tq = c * Q_BLOCK + lax.broadcasted_iota(jnp.int32, (N_SEL, Q_BLOCK), 1)
    cur = tq >> 6
    jj = lax.broadcasted_iota(jnp.int32, (N_SEL, N_SEG), 0)
    nn = lax.broadcasted_iota(jnp.int32, (N_SEL, N_SEG), 1)
    ratio = SEL_BLOCK // CMP_STRIDE
    ov = jnp.where((nn >= ratio * jj - 1) & (nn <= ratio * jj + ratio - 1) & (nn < N_SEG - 1), 1.0, 0.0)
    nrow = lax.broadcasted_iota(jnp.int32, (N_SEG + 16, W4), 0) - 16
    near0 = (Q_BLOCK // CMP_STRIDE) * c - 16
    wrow_ok = lax.broadcasted_iota(jnp.int32, (n_w, W4), 0) >= WINDOW - c * Q_BLOCK

    def front(g):
        q = q_ref[g]
        b31 = b31_ref[g]
        qk_cmp = _mm(ck_ref[g], q)
        qk_near = _mm(ks_ref[pl.ds(r0, 2 * Q_BLOCK), :], q)
        qk_win = _mm(kw_ref[pl.ds(r0, n_w), :], q)

        bc_s[g] = jnp.where(nrow < near0, b31, NEG)
        bc_s[g, pl.ds(pl.multiple_of(near0 + 16, 8), CMP_NEAR), :] = bct_ref[g]
        s = qk_cmp + bc_s[g, 16:, :]
        e = jnp.where(s > 0.1 * NEG, jnp.exp(s - jnp.max(s, axis=0, keepdims=True)), 0.0)
        p = e / jnp.maximum(jnp.sum(e, axis=0, keepdims=True), 1e-30)
        o_cmp = _mm(cv_ref[g], _bf(p))

        psum = (p[:, 0:Q_BLOCK] + p[:, Q_BLOCK:2 * Q_BLOCK] + p[:, 2 * Q_BLOCK:3 * Q_BLOCK]
                + p[:, 3 * Q_BLOCK:4 * Q_BLOCK])
        imp = _mm(ov, psum, HP)

        s = jnp.where(wrow_ok, qk_win + bw_ref[g], NEG)
        pw = jnp.exp(s - jnp.max(s, axis=0, keepdims=True))
        o_win = _mm(v_tiles(vw_ref, g, c, n_w // Q_BLOCK), _bf(pw)) / jnp.sum(pw, axis=0, keepdims=True)

        imp = jnp.where((jb == 0) | (jb == cur), FORCE_SCORE, jnp.where(jb <= cur, imp, -1.0))
        rank = jnp.zeros((N_SEL, Q_BLOCK), F32)
        for jp in range(N_SEL):
            rowv = imp[jp:jp + 1, :]
            rank = rank + jnp.where(rowv > imp, 1.0, 0.0)
            if jp < N_SEL - 1:
                rank = rank + jnp.where(jb > jp, jnp.where(rowv == imp, 1.0, 0.0), 0.0)
        sel = jnp.where(rank < SEL_TOPN, 1.0, 0.0)
        sel_s[g, 0:8, :] = jnp.zeros((8, Q_BLOCK), F32)
        sel_s[g, 8:, :] = sel
        far_s[g] = jnp.where(jb < 2 * (c - 1), sel, 0.0)

        near = block_mask([sel_s[g, pl.ds(6 + 2 * c + j, 1), :] for j in range(4)])
        s = jnp.where(near, qk_near + bt_ref[g], NEG)
        m_run = jnp.max(s, axis=0, keepdims=True)
        pe = jnp.exp(s - m_run)
        l_run = jnp.sum(pe, axis=0, keepdims=True)
        acc = _mm(v_tiles(vs_ref, g, c, 2), _bf(pe))
        return (q, b31, o_cmp, o_win), (m_run, l_run, acc)

    fronts = [front(g) for g in range(G)]

    def far_step(st, carry):
        k0 = pl.multiple_of(SEL_PAD + st * FAR_TILE, Q_BLOCK)
        b0 = pl.multiple_of(st * far_blocks, 8)
        kf = ks_ref[pl.ds(k0, FAR_TILE), :]
        out = []
        for g in range(G):
            m_old, l_old, acc_old = carry[g]
            q, b31 = fronts[g][0][:2]
            rows = far_s[g, pl.ds(b0, far_blocks), :]
            msk = block_mask([rows[j:j + 1, :] for j in range(far_blocks)])
            sc = jnp.where(msk, _mm(kf, q) + b31, NEG)
            m_new = jnp.maximum(m_old, jnp.max(sc, axis=0, keepdims=True))
            alpha = jnp.exp(m_old - m_new)
            pf = jnp.exp(sc - m_new)
            l_new = alpha * l_old + jnp.sum(pf, axis=0, keepdims=True)
            vt = v_tiles(vs_ref, g, 1 + st * far_tiles, far_tiles)
            out.append((m_new, l_new, alpha * acc_old + _mm(vt, _bf(pf))))
        return tuple(out)

    n_far = (c + 2) // far_tiles
    finals = lax.fori_loop(0, n_far, far_step, tuple(f[1] for f in fronts))

    gates = _sigmoid(cg_ref[...])
    outs = []
    for g in range(G):
        _, _, o_cmp, o_win = fronts[g][0]
        _, l_run, acc = finals[g]

        def gate_row(r):
            rows = [gates[g * CG_ROWS + k * 3 + r:g * CG_ROWS + k * 3 + r + 1, :] for k in range(NSA_HPG)]
            return jnp.concatenate(rows, axis=1)

        res = gate_row(0) * o_cmp + gate_row(1) * (acc / l_run) + gate_row(2) * o_win
        outs += [res[:, k * Q_BLOCK:(k + 1) * Q_BLOCK].T for k in range(NSA_HPG)]
    o_ref[...] = jnp.concatenate(outs, axis=1)


def _nsa_attend(qn, ck, cv, ksn, vst, kwn, vwt, tables, yt):
    bsz = qn.shape[0]
    G = NSA_GROUPS
    W4 = NSA_HPG * Q_BLOCK
    bct, bt, bw, b31 = tables

    def per_b(a):
        shape = a.shape[1:]
        return pl.BlockSpec((None,) + shape, lambda b, c, n=len(shape): (b,) + (0,) * n)

    def full(a):
        return pl.BlockSpec(a.shape, lambda b, c, n=a.ndim: (0,) * n)

    return pl.pallas_call(
        _nsa_kernel2,
        grid=(bsz, N_QBLK),
        in_specs=[
            pl.BlockSpec((None, G, None, 2 * NSA_DH, W4), lambda b, c: (b, 0, c, 0, 0)),
            per_b(ck), per_b(cv), per_b(ksn), per_b(vst), per_b(kwn), per_b(vwt),
            full(bct), full(bt), full(bw), full(b31),
            pl.BlockSpec((None, G * CG_ROWS, Q_BLOCK), lambda b, c: (b, TR_CG // (G * CG_ROWS), c)),
        ],
        out_specs=pl.BlockSpec((None, Q_BLOCK, NSA_HEADS * NSA_DH), lambda b, c: (b, c, 0)),
        out_shape=jax.ShapeDtypeStruct((bsz, SEQ, NSA_HEADS * NSA_DH), F32),
        scratch_shapes=[pltpu.VMEM((G, 8 + N_SEL, Q_BLOCK), F32), pltpu.VMEM((G, N_SEL, Q_BLOCK), F32),
                        pltpu.VMEM((G, 16 + N_SEG, W4), F32)],
        compiler_params=_params("parallel", "arbitrary"),
        name="nsa_attend",
    )(qn, ck, cv, ksn, vst, kwn, vwt, bct, bt, bw, b31, yt)


def _bucket_starts():
    max_exact = REL_BUCKETS // 2
    d = np.arange(4 * REL_MAX_DIST)
    large = max_exact + (np.log(np.maximum(d, 1) / max_exact) / math.log(REL_MAX_DIST / max_exact)
                         * (REL_BUCKETS - max_exact)).astype(np.int64)
    bucket = np.where(d < max_exact, d, np.minimum(large, REL_BUCKETS - 1))
    assert np.all(np.diff(bucket) >= 0) and bucket[-1] == REL_BUCKETS - 1
    return [int(np.argmax(bucket >= j)) for j in range(REL_BUCKETS)]


BUCKET_STARTS = _bucket_starts()
TABLE_ROWS = 128


def _bias_kernel(rb_ref, bct_ref, bt_ref, bw_ref):
    W4 = NSA_HPG * Q_BLOCK
    rb = rb_ref[...]

    def table(rows, r0, dist_fn, limit):
        r = r0 + lax.broadcasted_iota(jnp.int32, (rows, W4), 0)
        i = lax.broadcasted_iota(jnp.int32, (rows, W4), 1) & (Q_BLOCK - 1)
        dist = dist_fn(r, i)
        acc = jnp.broadcast_to(rb[0:1, :], (rows, W4))
        for j in range(1, REL_BUCKETS):
            acc = jnp.where(dist >= BUCKET_STARTS[j], rb[j:j + 1, :], acc)
        return jnp.where((dist < 0) | (dist >= limit), NEG, acc)

    big = 1 << 30
    bct_ref[...] = table(CMP_NEAR, 0, lambda r, i: i - CMP_STRIDE * r + (16 * CMP_STRIDE - CMP_BLOCK + 1), big)

    def near_rows(k, carry):
        r0 = pl.multiple_of(k * TABLE_ROWS, TABLE_ROWS)
        bt_ref[pl.ds(r0, TABLE_ROWS), :] = table(TABLE_ROWS, r0, lambda r, i: Q_BLOCK + i - r, big)
        return carry

    lax.fori_loop(0, 2 * Q_BLOCK // TABLE_ROWS, near_rows, 0)

    def win_rows(k, carry):
        r0 = pl.multiple_of(k * TABLE_ROWS, TABLE_ROWS)
        bw_ref[pl.ds(r0, TABLE_ROWS), :] = table(TABLE_ROWS, r0, lambda r, i: WINDOW + i - r, WINDOW)
        return carry

    lax.fori_loop(0, (WINDOW + Q_BLOCK) // TABLE_ROWS, win_rows, 0)


def _bias_tables(rel_bias):
    G, K = NSA_GROUPS, NSA_HPG
    W4 = K * Q_BLOCK
    rb = jnp.repeat(rel_bias.reshape(REL_BUCKETS, G, K).transpose(1, 0, 2), Q_BLOCK, axis=2)
    shapes = [(G, CMP_NEAR, W4), (G, 2 * Q_BLOCK, W4), (G, WINDOW + Q_BLOCK, W4)]
    bct, bt, bw = pl.pallas_call(
        _bias_kernel,
        grid=(G,),
        in_specs=[pl.BlockSpec((None, REL_BUCKETS, W4), lambda g: (g, 0, 0))],
        out_specs=[pl.BlockSpec((None,) + s[1:], lambda g: (g, 0, 0)) for s in shapes],
        out_shape=[jax.ShapeDtypeStruct(s, F32) for s in shapes],
        compiler_params=_params("parallel"),
        name="bias_tables",
    )(rb)
    return bct, bt, bw, rb[:, REL_BUCKETS - 1:, :]


def _nsa(y3, yt, q_norm, k_norm, cmp_pe, w_cmp, tables):
    bsz = y3.shape[0]
    G, DH = NSA_GROUPS, NSA_DH

    def segments(cb):
        return y3[:, :, cb * LANES:(cb + 1) * LANES].reshape(bsz, N_SEG, CMP_STRIDE * LANES)

    def both_groups(w):
        return jnp.einsum('lde,gh->lgdhe', w, jnp.eye(G, dtype=w.dtype)).reshape(CMP_STRIDE * LANES, LANES)

    def pe_row(pe):
        return jnp.tile(pe[:, None, :], (1, G, 1)).reshape(1, CMP_STRIDE * LANES)

    wk = w_cmp[0].reshape(2, CMP_STRIDE, DH, DH)
    wv = w_cmp[1].reshape(2, CMP_STRIDE, DH, DH)
    pe = cmp_pe.reshape(2, 2, CMP_STRIDE, DH)
    gain2 = lambda g: jnp.tile(g, G).reshape(1, LANES)
    consts = [_bf(both_groups(wk[0])), _bf(both_groups(wk[1])),
              _bf(both_groups(wv[0]).T), _bf(both_groups(wv[1]).T),
              pe_row(pe[0, 0]), pe_row(pe[0, 1]), pe_row(pe[1, 0]), pe_row(pe[1, 1]),
              q_norm.reshape(DH, 1), gain2(k_norm[0]), gain2(k_norm[1]), gain2(k_norm[2])]
    qn, ksn, kwn, ck, cv, vst, vwt = _nsa_prep(y3, yt, segments(CB_CKC), segments(CB_CVC), consts)
    return _nsa_attend(qn, ck, cv, ksn, vst, kwn, vwt, tables, yt)


def _merge_kernel(h_ref, g_ref, ya_ref, yb_ref, yc_ref, wg_ref, wb_ref, wo_ref, o_ref):
    x = h_ref[...]
    u = _bf(_rms_rows(x, g_ref[...]))
    acc = None
    for n, y_ref in enumerate((ya_ref, yb_ref, yc_ref)):
        t = _sigmoid(_mm(u, wg_ref[:, n * D_MODEL:(n + 1) * D_MODEL])) * _mm(_bf(y_ref[...]), wb_ref[n])
        acc = t if acc is None else acc + t
    o_ref[...] = x + _mm(_bf(acc), wo_ref[...])


def _merge(h2, gain, ya, yb, yc, wgt, wb, wo):
    m, d = h2.shape
    tm = 512
    row = lambda w: pl.BlockSpec((tm, w), lambda i: (i, 0))
    return pl.pallas_call(
        _merge_kernel,
        grid=(m // tm,),
        in_specs=[row(d), pl.BlockSpec((1, d), lambda i: (0, 0)),
                  row(BRANCH_W), row(BRANCH_W), row(BRANCH_W),
                  pl.BlockSpec((d, N_BRANCH * d), lambda i: (0, 0)),
                  pl.BlockSpec((N_BRANCH, BRANCH_W, d), lambda i: (0, 0, 0)),
                  pl.BlockSpec((d, d), lambda i: (0, 0))],
        out_specs=row(d),
        out_shape=jax.ShapeDtypeStruct((m, d), F32),
        compiler_params=_params("parallel"),
        name="merge",
    )(h2, gain, ya, yb, yc, wgt, wb, wo)


def _ffn_kernel(h_ref, g_ref, wg_ref, wu_ref, wo_ref, o_ref, u_ref, acc_ref):
    f = pl.program_id(1)

    @pl.when(f == 0)
    def _():
        x = h_ref[...]
        u_ref[...] = _bf(_rms_rows(x, g_ref[...]))
        acc_ref[...] = x

    u = u_ref[...]
    gt = _mm(u, wg_ref[...])
    up = _mm(u, wu_ref[...])
    acc_ref[...] += _mm(_bf(gt * _sigmoid(gt) * up), wo_ref[...])

    @pl.when(f == pl.num_programs(1) - 1)
    def _():
        o_ref[...] = acc_ref[...]


def _ffn(h2, gain, w_in, w_out):
    m, d = h2.shape
    tm, nf = 512, 2
    tf = D_FF // nf
    return pl.pallas_call(
        _ffn_kernel,
        grid=(m // tm, nf),
        in_specs=[pl.BlockSpec((tm, d), lambda i, f: (i, 0)),
                  pl.BlockSpec((1, d), lambda i, f: (0, 0)),
                  pl.BlockSpec((d, tf), lambda i, f: (0, f)),
                  pl.BlockSpec((d, tf), lambda i, f: (0, f + nf)),
                  pl.BlockSpec((tf, d), lambda i, f: (f, 0))],
        out_specs=pl.BlockSpec((tm, d), lambda i, f: (i, 0)),
        out_shape=jax.ShapeDtypeStruct((m, d), F32),
        scratch_shapes=[pltpu.VMEM((tm, d), BF16), pltpu.VMEM((tm, d), F32)],
        compiler_params=_params("parallel", "arbitrary"),
        name="ffn",
    )(h2, gain, w_in, w_in, w_out)


def _ple_kernel(h_ref, g_ref, p_ref, wg_ref, wp_ref, o_ref):
    x = h_ref[...]
    gate = _sigmoid(_mm(_bf(_rms_rows(x, g_ref[...])), wg_ref[...]))
    o_ref[...] = x + gate * _mm(_bf(p_ref[...]), wp_ref[...])


def _ple(h2, gain, p2, wg, wp):
    m, d = h2.shape
    tm = 512
    return pl.pallas_call(
        _ple_kernel,
        grid=(m // tm,),
        in_specs=[pl.BlockSpec((tm, d), lambda i: (i, 0)),
                  pl.BlockSpec((1, d), lambda i: (0, 0)),
                  pl.BlockSpec((tm, PLE_DIM), lambda i: (i, 0)),
                  pl.BlockSpec((d, d), lambda i: (0, 0)),
                  pl.BlockSpec((PLE_DIM, d), lambda i: (0, 0))],
        out_specs=pl.BlockSpec((tm, d), lambda i: (i, 0)),
        out_shape=jax.ShapeDtypeStruct((m, d), F32),
        compiler_params=_params("parallel"),
        name="ple",
    )(h2, gain, p2, wg, wp)


W_IN_GROUPS = ((0, 0, 16), (16, 8, 16), (32, 16, 10))
W_IN_GATES = (42, 40, 24)
W_IN_MIX_BLOCKS = 44
W_IN_SCALAR_BLOCKS = (16, 32, 42)
W_IN_SCALAR_LANES = (8, 16, 40)


def _regroup_kernel(src_ref, off_ref, a_ref, b_ref, s0_ref, s1_ref, s2_ref, o_ref, *, offsets, n_main):
    j = pl.program_id(0)
    x = jnp.concatenate([a_ref[...], b_ref[...]], axis=1)
    for off in offsets:
        @pl.when((off_ref[j] == off) & (j < n_main))
        def _(off=off):
            o_ref[...] = _bf(x[:, off:off + LANES])

    @pl.when(j == n_main)
    def _():
        lane = lax.broadcasted_iota(jnp.int32, (D_MODEL, LANES), 1)
        l0, l1, l2 = W_IN_SCALAR_LANES
        sc = jnp.where(lane < l0, s0_ref[...], jnp.where(lane < l1, s1_ref[...], s2_ref[...]))
        o_ref[...] = _bf(jnp.where(lane < l2, sc, 0.0))

    @pl.when(j > n_main)
    def _():
        o_ref[...] = jnp.zeros(o_ref.shape, BF16)


def _regroup(w_in, layer, groups, n_out, with_scalars):
    src = np.concatenate([blk + np.arange(n) for blk, _, n in groups])
    off = np.concatenate([np.full(n, o) for _, o, n in groups])
    n_main = len(src)
    pad = n_out - n_main
    src = np.concatenate([src, np.zeros(pad, np.int64)]).astype(np.int32)
    off = np.concatenate([off, np.zeros(pad, np.int64)]).astype(np.int32)
    blk = lambda f: pl.BlockSpec((None, D_MODEL, LANES), f)
    fixed = [blk(lambda j, s, o, b=b: (layer, 0, b)) for b in W_IN_SCALAR_BLOCKS]
    return pl.pallas_call(
        functools.partial(_regroup_kernel, offsets=tuple(sorted({o for _, o, _ in groups})),
                          n_main=n_main if with_scalars else n_out),
        grid_spec=pltpu.PrefetchScalarGridSpec(
            num_scalar_prefetch=2,
            grid=(n_out,),
            in_specs=[blk(lambda j, s, o: (layer, 0, s[j])), blk(lambda j, s, o: (layer, 0, s[j] + 1))] + fixed,
            out_specs=pl.BlockSpec((D_MODEL, LANES), lambda j, s, o: (0, j))),
        out_shape=jax.ShapeDtypeStruct((D_MODEL, n_out * LANES), BF16),
        compiler_params=_params("arbitrary"),
        name="regroup_w_in",
    )(jnp.asarray(src), jnp.asarray(off), w_in, w_in, w_in, w_in, w_in)


def _split_w_mix(w_mix):
    blk = lambda b, n=1: w_mix[:, b * LANES:(b + n) * LANES]
    main = jnp.concatenate([blk(0, 32), blk(36, 3), blk(40), blk(42, 2)], axis=1)
    main = jnp.pad(main, ((0, 0), (0, N_COLS - main.shape[1])))
    cg = blk(42)[:, SM_CG:SM_CG + NSA_HEADS * 3].reshape(D_MODEL, NSA_GROUPS, NSA_HPG * 3)
    cg = jnp.pad(cg, ((0, 0), (0, 0), (0, CG_ROWS - NSA_HPG * 3))).reshape(D_MODEL, NSA_GROUPS * CG_ROWS)
    return main, jnp.concatenate([blk(32, 4), blk(39), blk(41), cg], axis=1).T


def _lane_row(vals, offset):
    return jnp.zeros((1, LANES), F32).at[0, offset:offset + vals.shape[0]].set(vals)


def kernel(x, p, rel_bias, norm_mix, w_in, conv_w, gdn_a_log, gdn_dt_bias, gdn_norm, mlstm_b_i, mlstm_b_f,
           mlstm_norm, nsa_q_norm, nsa_k_norm, nsa_cmp_pe, nsa_w_cmp, w_branch, w_out, norm_ffn, w_ffn_in,
           w_ffn_out, norm_ple, w_ple_gate, w_ple_proj):
    bsz, seq, d = x.shape
    assert seq == SEQ and d == D_MODEL
    depth = w_in.shape[0]
    m = bsz * seq
    tables = _bias_tables(rel_bias)
    h2 = x.reshape(m, d)
    for l in range(depth):
        w_main, w_tr = _split_w_mix(_regroup(w_in, l, W_IN_GROUPS, W_IN_MIX_BLOCKS, True))
        w_gate = _regroup(w_in, l, (W_IN_GATES,), W_IN_GATES[2], False)
        gain = norm_mix[l].reshape(1, d)
        y3 = _in_proj(h2, gain, w_main).reshape(bsz, seq, N_COLS)
        yt = _in_proj_t(h2, gain, w_tr, bsz)
        ya = _gdn(y3, conv_w[l], _lane_row(gdn_a_log[l], SM_AA), _lane_row(gdn_dt_bias[l], SM_AA),
                  gdn_norm[l].reshape(1, D_HEAD_REC))
        yb = _mlstm(y3, _lane_row(mlstm_b_i[l], SM_BI), _lane_row(mlstm_b_f[l], SM_BF),
                    mlstm_norm[l].reshape(1, D_HEAD_REC))
        yc = _nsa(y3, yt, nsa_q_norm[l], nsa_k_norm[l], nsa_cmp_pe[l], nsa_w_cmp[l], tables)
        h2 = _merge(h2, gain, ya.reshape(m, BRANCH_W), yb.reshape(m, BRANCH_W),
                    yc.reshape(m, BRANCH_W), w_gate, _bf(w_branch[l]), _bf(w_out[l]))
        h2 = _ffn(h2, norm_ffn[l].reshape(1, d), _bf(w_ffn_in[l]), _bf(w_ffn_out[l]))
        h2 = _ple(h2, norm_ple[l].reshape(1, d), p[l].reshape(m, PLE_DIM), _bf(w_ple_gate[l]),
                  _bf(w_ple_proj[l]))
    return h2.reshape(bsz, seq, d)
```

```python
import functools
import math

import numpy as np
import jax
import jax.numpy as jnp
from jax import lax
from jax.experimental import pallas as pl
from jax.experimental.pallas import tpu as pltpu

D_MODEL = 1024
SEQ = 2048
N_HEADS_REC = 4
D_HEAD_REC = 128
CHUNK = 64
GDN_HPS = 2
GDN_CONV = 4
NSA_HEADS = 8
NSA_GROUPS = 2
NSA_HPG = NSA_HEADS // NSA_GROUPS
NSA_DH = 64
CMP_BLOCK = 32
CMP_STRIDE = 16
SEL_BLOCK = 64
SEL_TOPN = 4
WINDOW = 512
Q_BLOCK = 128
N_QBLK = SEQ // Q_BLOCK
N_SEL = SEQ // SEL_BLOCK
N_SEG = SEQ // CMP_STRIDE
SEL_PAD = Q_BLOCK
FAR_TILE = 4 * Q_BLOCK
CMP_NEAR = 24
REL_BUCKETS = 32
REL_MAX_DIST = 128
N_BRANCH = 3
BRANCH_W = 512
D_FF = 2816
PLE_DIM = 256
EPS = 1e-6
NEG = -1e30
FORCE_SCORE = 1e4
LOG2E = math.log2(math.e)

LANES = 128
VMEM_LIMIT = 48 * 1024 * 1024

F32 = jnp.float32
BF16 = jnp.bfloat16
HP = lax.Precision.HIGHEST

CB_A = 0
CB_B = 16
CB_CKC, CB_CVC, CB_CKS, CB_CKW = 32, 33, 34, 35
CB_SMALL = 36
N_COLS = 40 * LANES
TR_Q = 0
TR_VS = NSA_HEADS * NSA_DH
TR_VW = TR_VS + NSA_GROUPS * NSA_DH
TR_CG = TR_VW + NSA_GROUPS * NSA_DH
CG_ROWS = 16
N_TROWS = TR_CG + NSA_GROUPS * CG_ROWS
SM_AA, SM_AB, SM_BI, SM_BF, SM_CG = 0, 4, 8, 12, 16


def _mm(a, b, precision=None):
    return lax.dot_general(a, b, (((1,), (0,)), ((), ())), precision=precision,
                           preferred_element_type=F32)


def _mm_nt(a, b, precision=None):
    return lax.dot_general(a, b, (((1,), (1,)), ((), ())), precision=precision,
                           preferred_element_type=F32)


def _mm_tn(a, b, precision=None):
    return lax.dot_general(a, b, (((0,), (0,)), ((), ())), precision=precision,
                           preferred_element_type=F32)


def _bf(x):
    return x.astype(BF16)


def _sigmoid(x):
    return 1.0 / (1.0 + jnp.exp(-x))


def _softplus(x):
    return jnp.maximum(x, 0.0) + jnp.log1p(jnp.exp(-jnp.abs(x)))


def _rms_rows(x, g):
    return x * lax.rsqrt(jnp.mean(x * x, axis=-1, keepdims=True) + EPS) * g


def _params(*sem):
    return pltpu.CompilerParams(dimension_semantics=sem, vmem_limit_bytes=VMEM_LIMIT)


def _proj_kernel(x_ref, g_ref, w_ref, o_ref, u_ref):
    @pl.when(pl.program_id(1) == 0)
    def _():
        u_ref[...] = _bf(_rms_rows(x_ref[...], g_ref[...]))

    o_ref[...] = _mm(u_ref[...], w_ref[...])


def _in_proj(x2, gain, w):
    m, d = x2.shape
    n = w.shape[1]
    tm, tn = 1024, n // 4
    return pl.pallas_call(
        _proj_kernel,
        grid=(m // tm, n // tn),
        in_specs=[pl.BlockSpec((tm, d), lambda i, j: (i, 0)),
                  pl.BlockSpec((1, d), lambda i, j: (0, 0)),
                  pl.BlockSpec((d, tn), lambda i, j: (0, j))],
        out_specs=pl.BlockSpec((tm, tn), lambda i, j: (i, j)),
        out_shape=jax.ShapeDtypeStruct((m, n), F32),
        scratch_shapes=[pltpu.VMEM((tm, d), BF16)],
        compiler_params=_params("parallel", "arbitrary"),
        name="in_proj",
    )(x2, gain, w)


def _proj_t_kernel(x_ref, g_ref, wt_ref, o_ref):
    o_ref[...] = _mm_nt(wt_ref[...], _bf(_rms_rows(x_ref[...], g_ref[...])))


def _in_proj_t(x2, gain, wt, bsz):
    m, d = x2.shape
    n = wt.shape[0]
    ts = 512
    nt = SEQ // ts
    return pl.pallas_call(
        _proj_t_kernel,
        grid=(bsz, nt),
        in_specs=[pl.BlockSpec((ts, d), lambda b, s: (b * nt + s, 0)),
                  pl.BlockSpec((1, d), lambda b, s: (0, 0)),
                  pl.BlockSpec((n, d), lambda b, s: (0, 0))],
        out_specs=pl.BlockSpec((None, n, ts), lambda b, s: (b, 0, s)),
        out_shape=jax.ShapeDtypeStruct((bsz, n, SEQ), F32),
        compiler_params=_params("parallel", "parallel"),
        name="in_proj_t",
    )(x2, gain, wt)


def _chunk_masks():
    ri = lax.broadcasted_iota(jnp.int32, (CHUNK, CHUNK), 0)
    ci = lax.broadcasted_iota(jnp.int32, (CHUNK, CHUNK), 1)
    return ri, ci


def _chunk_cumsum(x):
    rowi = lax.broadcasted_iota(jnp.int32, x.shape, 0)
    s = 1
    while s < CHUNK:
        x = x + jnp.where(rowi >= s, pltpu.roll(x, s, axis=0), 0.0)
        s *= 2
    return x


def _pick_lane(x, lane_idx):
    lane = lax.broadcasted_iota(jnp.int32, x.shape, 1)
    return jnp.sum(jnp.where(lane == lane_idx, x, 0.0), axis=1, keepdims=True)


def _col_to_row(col, eye):
    return jnp.sum(jnp.where(eye, col, 0.0), axis=0, keepdims=True)


def _bd_groups(pk):
    grp = lax.broadcasted_iota(jnp.int32, pk.shape, 1) // CHUNK
    zero = jnp.zeros_like(pk)
    return jnp.concatenate([jnp.where(grp == j, pk, zero) for j in range(pk.shape[1] // CHUNK)], axis=0)


def _bd_wide(x):
    zero = jnp.zeros((x.shape[0], LANES), x.dtype)
    return jnp.concatenate([jnp.concatenate([x[:, :LANES], zero], axis=1),
                            jnp.concatenate([zero, x[:, LANES:]], axis=1)], axis=0)


def _inv_unit_lower(x, bd, eyef):
    xd = jnp.where(bd, x, 0.0)
    xo = jnp.where(bd, 0.0, x)
    mm = lambda a, b: _mm(_bf(a), _bd_groups(_bf(b)))

    def mm2(a1, a2, b):
        r = mm(jnp.concatenate([a1, a2], axis=0), b)
        return r[:CHUNK], r[CHUNK:]

    x2 = mm(xd, xd)
    p = eyef - xd
    x4, t = mm2(x2, p, x2)
    p = p + t
    x8, t = mm2(x4, p, x4)
    p = p + t
    p = p + mm(p, x8)
    m = mm(p, xo)
    m2 = mm(m, m)
    q = eyef - m
    q = q + mm(q, m2)
    return mm(q, p)


def _gdn_kernel(q_ref, k_ref, v_ref, z_ref, sm_ref, cwq_ref, cwk_ref, cwv_ref, alog_ref, dtb_ref,
                nw_ref, o_ref, qg_s, kd_s, u_s, w_s, at_s, eg_s):
    L = CHUNK
    W = 2 * LANES
    ri = lax.broadcasted_iota(jnp.int32, (L, 4 * L), 0)
    cp = lax.broadcasted_iota(jnp.int32, (L, 4 * L), 1)
    ci = cp & (L - 1)
    first = lax.broadcasted_iota(jnp.int32, (L, 2 * L), 1) < L
    tril = ri >= ci
    strict = ri > ci
    eye = ri == ci
    bd = (ri >> 4) == (ci >> 4)
    eyef = jnp.where(eye, 1.0, 0.0)
    alog = alog_ref[...]
    dtb = dtb_ref[...]
    nw = nw_ref[...]
    cwq, cwk, cwv = cwq_ref[...], cwk_ref[...], cwv_ref[...]
    ha = pl.program_id(1) * 2

    def conv_silu(ref, w, n, r0):
        cur = ref[pl.ds(r0, L), :]
        p0 = pl.multiple_of(jnp.maximum(r0 - 8, 0), 8)
        prev = jnp.where(n > 0, ref[pl.ds(p0, 8), :], 0.0)
        win = jnp.concatenate([prev, cur], axis=0)
        acc = cur * w[GDN_CONV - 1:GDN_CONV, :]
        for s in range(1, GDN_CONV):
            acc = acc + pltpu.roll(win, s, axis=0)[8:, :] * w[GDN_CONV - 1 - s:GDN_CONV - s, :]
        return acc * _sigmoid(acc)

    def wide(col_a, col_b, rows=L):
        return jnp.concatenate([jnp.broadcast_to(col_a, (rows, LANES)),
                                jnp.broadcast_to(col_b, (rows, LANES))], axis=1)

    def l2n(x):
        xx = x * x
        return x * wide(lax.rsqrt(jnp.sum(xx[:, :LANES], axis=-1, keepdims=True) + 1e-6),
                        lax.rsqrt(jnp.sum(xx[:, LANES:], axis=-1, keepdims=True) + 1e-6))

    def chunk_terms(n):
        r0 = pl.multiple_of(n * L, L)
        x = sm_ref[pl.ds(r0, L), :]
        g = _chunk_cumsum(-jnp.exp(alog) * _softplus(x + dtb))
        sx = _sigmoid(x)
        q = l2n(conv_silu(q_ref, cwq, n, r0)) * (D_HEAD_REC ** -0.5)
        k = l2n(conv_silu(k_ref, cwk, n, r0))
        v = conv_silu(v_ref, cwv, n, r0)
        gc_a, gc_b = _pick_lane(g, SM_AA + ha), _pick_lane(g, SM_AA + ha + 1)
        beta = wide(_pick_lane(sx, SM_AB + ha), _pick_lane(sx, SM_AB + ha + 1))
        kb = k * beta
        egc = wide(jnp.exp(gc_a), jnp.exp(gc_b))
        gl_a, gl_b = gc_a[L - 1:L, :], gc_b[L - 1:L, :]
        qg_s[pl.ds(r0, L), :] = _bf(q * egc)
        kd_s[pl.ds(r0, L), :] = _bf(k * wide(jnp.exp(gl_a - gc_a), jnp.exp(gl_b - gc_b)))
        eg_s[pl.ds(pl.multiple_of(n * 8, 8), 8), :] = wide(jnp.exp(gl_a), jnp.exp(gl_b), 8)
        gc_pk = jnp.where(first, gc_a, gc_b)
        return q, _bf(k), kb, _bd_wide(_bf(v * beta)), _bd_wide(_bf(kb * egc)), gc_pk

    zk = jnp.zeros((2 * L, W), BF16)

    def prep(i, carry):
        n0 = 2 * i
        r0 = pl.multiple_of(n0 * L, 2 * L)
        q0, k0, kb0, vb0, ke0, gc0 = chunk_terms(n0)
        q1, k1, kb1, vb1, ke1, gc1 = chunk_terms(n0 + 1)
        gc = jnp.concatenate([gc0, gc1], axis=1)
        diff = gc - jnp.sum(jnp.where(eye, gc, 0.0), axis=0, keepdims=True)
        decay = jnp.where(tril, jnp.exp(jnp.where(tril, diff, 0.0)), 0.0)
        lhs = jnp.concatenate([jnp.concatenate([kb0, kb1], axis=1),
                               jnp.concatenate([q0, q1], axis=1)], axis=0)
        rhs = jnp.concatenate([jnp.concatenate([_bd_wide(k0), zk], axis=1),
                               jnp.concatenate([zk, _bd_wide(k1)], axis=1)], axis=0)
        kq = _mm_nt(_bf(lhs), rhs)
        xm = jnp.where(strict, kq[:L] * decay, 0.0)
        t = _bf(_inv_unit_lower(xm, bd, eyef))
        attn = _bf(kq[L:] * decay)
        uw = _mm(t, jnp.concatenate([jnp.concatenate([vb0, zk, ke0, zk], axis=1),
                                     jnp.concatenate([zk, vb1, zk, ke1], axis=1)], axis=0))
        for j in range(2):
            rows = pl.ds(r0 + j * L, L)
            u_s[rows, :] = uw[:, j * W:(j + 1) * W]
            w_s[rows, :] = _bf(uw[:, (2 + j) * W:(3 + j) * W])
            at_s[rows, :] = attn[:, j * 2 * L:(j + 1) * 2 * L]
        return carry

    zs = jnp.zeros((D_HEAD_REC, D_HEAD_REC), BF16)

    def step(n, states):
        sa, sb = states
        r0 = pl.multiple_of(n * L, L)
        sbd = jnp.concatenate([jnp.concatenate([_bf(sa), zs], axis=1),
                               jnp.concatenate([zs, _bf(sb)], axis=1)], axis=0)
        eg = eg_s[pl.ds(pl.multiple_of(n * 8, 8), 1), :]
        r = _mm(jnp.concatenate([w_s[pl.ds(r0, L), :], qg_s[pl.ds(r0, L), :]], axis=0), sbd)
        vnb = _bf(u_s[pl.ds(r0, L), :] - r[:L])
        o = r[L:] + _mm(at_s[pl.ds(r0, L), :], _bd_wide(vnb))
        upd = _mm_tn(kd_s[pl.ds(r0, L), :], vnb)
        sa = sa * eg[:, :LANES] + upd[:LANES, :LANES]
        sb = sb * eg[:, LANES:] + upd[LANES:, LANES:]
        z = z_ref[pl.ds(r0, L), :]
        on = jnp.concatenate([_rms_rows(o[:, :LANES], nw), _rms_rows(o[:, LANES:], nw)], axis=1)
        o_ref[pl.ds(r0, L), :] = on * (z * _sigmoid(z))
        return sa, sb

    n_pairs = SEQ // (2 * L)

    def pair(i, states):
        states = step(2 * i + 1, step(2 * i, states))
        prep(i + 1, 0)
        return states

    zero = jnp.zeros((D_HEAD_REC, D_HEAD_REC), F32)
    prep(0, 0)
    states = lax.fori_loop(0, n_pairs - 1, pair, (zero, zero))
    step(2 * n_pairs - 1, step(2 * n_pairs - 2, states))


def _gdn(y3, conv_w, alog_row, dtb_row, norm_w):
    bsz = y3.shape[0]
    H = N_HEADS_REC
    W = 2 * LANES

    def col(off):
        return pl.BlockSpec((None, SEQ, W), lambda b, h, off=off: (b, 0, off // 2 + h))

    def cw(off):
        return pl.BlockSpec((GDN_CONV, W), lambda b, h, off=off: (0, off // 2 + h))

    row = pl.BlockSpec((1, LANES), lambda b, h: (0, 0))
    return pl.pallas_call(
        _gdn_kernel,
        grid=(bsz, H // 2),
        in_specs=[col(CB_A), col(CB_A + H), col(CB_A + 2 * H), col(CB_A + 3 * H),
                  pl.BlockSpec((None, SEQ, LANES), lambda b, h: (b, 0, CB_SMALL)),
                  cw(0), cw(H), cw(2 * H), row, row, row],
        out_specs=pl.BlockSpec((None, SEQ, W), lambda b, h: (b, 0, h)),
        out_shape=jax.ShapeDtypeStruct((bsz, SEQ, H * D_HEAD_REC), F32),
        scratch_shapes=[pltpu.VMEM((SEQ, W), BF16), pltpu.VMEM((SEQ, W), BF16),
                        pltpu.VMEM((SEQ, W), F32), pltpu.VMEM((SEQ, W), BF16),
                        pltpu.VMEM((SEQ, 2 * CHUNK), BF16),
                        pltpu.VMEM((SEQ // CHUNK * 8, W), F32)],
        compiler_params=_params("parallel", "parallel"),
        name="gdn",
    )(y3, y3, y3, y3, y3, conv_w, conv_w, conv_w, alog_row, dtb_row, norm_w)


def _mlstm_kernel(q_ref, k_ref, v_ref, og_ref, sm_ref, bi_ref, bf_ref, nw_ref, o_ref):
    L = CHUNK
    ri = lax.broadcasted_iota(jnp.int32, (L, 2 * L), 0)
    cp = lax.broadcasted_iota(jnp.int32, (L, 2 * L), 1)
    ci = cp & (L - 1)
    first = cp < L
    tril = ri >= ci
    eye = ri == ci
    bi = bi_ref[...]
    bfr = bf_ref[...]
    nw = nw_ref[...]
    ha = pl.program_id(1) * 2
    zs = jnp.zeros((D_HEAD_REC, D_HEAD_REC), BF16)

    def wide(col_a, col_b, rows=L):
        return jnp.concatenate([jnp.broadcast_to(col_a, (rows, LANES)),
                                jnp.broadcast_to(col_b, (rows, LANES))], axis=1)

    def step(n, carry):
        c_a, c_b, n_st, m_a, m_b = carry
        r0 = pl.multiple_of(n * L, L)
        q = q_ref[pl.ds(r0, L), :]
        k = k_ref[pl.ds(r0, L), :] * (D_HEAD_REC ** -0.5)
        vb = _bf(v_ref[pl.ds(r0, L), :])
        x = sm_ref[pl.ds(r0, L), :]
        lfc = _chunk_cumsum(-_softplus(-(x + bfr)))
        itx = x + bi
        b_a, b_b = _pick_lane(lfc, SM_BF + ha), _pick_lane(lfc, SM_BF + ha + 1)
        it_a, it_b = _pick_lane(itx, SM_BI + ha), _pick_lane(itx, SM_BI + ha + 1)
        b_pk = jnp.where(first, b_a, b_b)
        it_pk = jnp.where(first, it_a, it_b)
        b_row = jnp.sum(jnp.where(eye, b_pk, 0.0), axis=0, keepdims=True)
        it_row = jnp.sum(jnp.where(eye, it_pk, 0.0), axis=0, keepdims=True)
        dm = jnp.where(tril, b_pk - b_row + it_row, NEG)
        dmax_a = jnp.max(jnp.where(first, dm, NEG), axis=1, keepdims=True)
        dmax_b = jnp.max(jnp.where(first, NEG, dm), axis=1, keepdims=True)
        qk = _mm_nt(_bf(q), _bd_wide(_bf(k)))
        a_a, a_b = b_a + m_a, b_b + m_b
        mt_a, mt_b = jnp.maximum(a_a, dmax_a), jnp.maximum(a_b, dmax_b)
        sm = jnp.exp(dm - jnp.where(first, mt_a, mt_b)) * qk
        si_a, si_b = jnp.exp(a_a - mt_a), jnp.exp(a_b - mt_b)
        cbd = jnp.concatenate([jnp.concatenate([_bf(c_a), zs], axis=1),
                               jnp.concatenate([zs, _bf(c_b)], axis=1)], axis=0)
        num = _mm(jnp.concatenate([_bf(q * wide(si_a, si_b)), _bf(sm)], axis=1),
                  jnp.concatenate([cbd, _bd_wide(vb)], axis=0))
        qn = q * n_st
        den_a = (si_a * jnp.sum(qn[:, :LANES], axis=1, keepdims=True)
                 + jnp.sum(jnp.where(first, sm, 0.0), axis=1, keepdims=True))
        den_b = (si_b * jnp.sum(qn[:, LANES:], axis=1, keepdims=True)
                 + jnp.sum(jnp.where(first, 0.0, sm), axis=1, keepdims=True))
        hh = num / wide(jnp.maximum(jnp.abs(den_a), jnp.exp(-mt_a)),
                        jnp.maximum(jnp.abs(den_b), jnp.exp(-mt_b)))
        bl_a, bl_b = b_a[L - 1:L, :], b_b[L - 1:L, :]
        ds_a, ds_b = bl_a - b_a + it_a, bl_b - b_b + it_b
        mn_a = jnp.maximum(bl_a + m_a, jnp.max(ds_a, axis=0, keepdims=True))
        mn_b = jnp.maximum(bl_b + m_b, jnp.max(ds_b, axis=0, keepdims=True))
        wk = k * wide(jnp.exp(ds_a - mn_a), jnp.exp(ds_b - mn_b))
        sc_a, sc_b = jnp.exp(bl_a + m_a - mn_a), jnp.exp(bl_b + m_b - mn_b)
        upd = _mm_tn(_bf(wk), vb)
        c_a = sc_a * c_a + upd[:LANES, :LANES]
        c_b = sc_b * c_b + upd[LANES:, LANES:]
        n_st = wide(sc_a, sc_b, 1) * n_st + jnp.sum(wk, axis=0, keepdims=True)
        on = jnp.concatenate([_rms_rows(hh[:, :LANES], nw), _rms_rows(hh[:, LANES:], nw)], axis=1)
        o_ref[pl.ds(r0, L), :] = on * _sigmoid(og_ref[pl.ds(r0, L), :])
        return c_a, c_b, n_st, mn_a, mn_b

    zc = jnp.zeros((D_HEAD_REC, D_HEAD_REC), F32)
    z1 = jnp.zeros((1, 1), F32)
    lax.fori_loop(0, SEQ // L, step, (zc, zc, jnp.zeros((1, 2 * LANES), F32), z1, z1), unroll=2)


def _mlstm(y3, bi_row, bf_row, norm_w):
    bsz = y3.shape[0]
    H = N_HEADS_REC
    W = 2 * LANES

    def col(off):
        return pl.BlockSpec((None, SEQ, W), lambda b, h, off=off: (b, 0, off // 2 + h))

    row = pl.BlockSpec((1, LANES), lambda b, h: (0, 0))
    return pl.pallas_call(
        _mlstm_kernel,
        grid=(bsz, H // 2),
        in_specs=[col(CB_B), col(CB_B + H), col(CB_B + 2 * H), col(CB_B + 3 * H),
                  pl.BlockSpec((None, SEQ, LANES), lambda b, h: (b, 0, CB_SMALL)),
                  row, row, row],
        out_specs=pl.BlockSpec((None, SEQ, W), lambda b, h: (b, 0, h)),
        out_shape=jax.ShapeDtypeStruct((bsz, SEQ, H * D_HEAD_REC), F32),
        compiler_params=_params("parallel", "parallel"),
        name="mlstm",
    )(y3, y3, y3, y3, y3, bi_row, bf_row, norm_w)


def _nsa_prep_kernel(yt_ref, ks_ref, kw_ref, k2_ref, v2_ref, wkl_ref, wkh_ref, wvl_ref, wvh_ref,
                     pekl_ref, pekh_ref, pevl_ref, pevh_ref, qn_ref, kn0_ref, kn1_ref, kn2_ref,
                     qn_out, ks_out, kw_out, ck_out, cv_out, vs_out, vw_out):
    G, K, DH = NSA_GROUPS, NSA_HPG, NSA_DH
    lane = lax.broadcasted_iota(jnp.int32, (Q_BLOCK, LANES), 1)
    lo = lane < DH

    def half_ms(x):
        xx = x * x
        s0 = jnp.sum(jnp.where(lo, xx, 0.0), axis=1, keepdims=True)
        s1 = jnp.sum(jnp.where(lo, 0.0, xx), axis=1, keepdims=True)
        return jnp.where(lo, s0, s1) * (1.0 / DH)

    k2 = k2_ref[...]
    v2 = v2_ref[...]
    ckb = (_mm(_bf(k2 + pekl_ref[...]), wkl_ref[...])
           + pltpu.roll(_mm(_bf(k2 + pekh_ref[...]), wkh_ref[...]), N_SEG - 1, axis=0))
    ckn = ckb * lax.rsqrt(half_ms(ckb) + EPS) * kn0_ref[...]
    ck_out[0] = _bf(jnp.where(lo, ckn, 0.0))
    ck_out[1] = _bf(jnp.where(lo, 0.0, ckn))
    cvb = (_mm_nt(wvl_ref[...], _bf(v2 + pevl_ref[...]))
           + pltpu.roll(_mm_nt(wvh_ref[...], _bf(v2 + pevh_ref[...])), N_SEG - 1, axis=1))

    qg = qn_ref[...] * (DH ** -0.5 * LOG2E)
    zq = jnp.zeros((DH, K * Q_BLOCK), F32)
    zv = jnp.zeros((DH, Q_BLOCK), BF16)
    for g in range(G):
        cv_out[g] = _bf(cvb[g * DH:(g + 1) * DH, :])
        for c in range(N_QBLK):
            cs = slice(c * Q_BLOCK, (c + 1) * Q_BLOCK)
            r0 = TR_Q + g * K * DH
            x = jnp.concatenate([yt_ref[r0 + k * DH:r0 + (k + 1) * DH, cs] for k in range(K)], axis=1)
            xn = x * lax.rsqrt(jnp.mean(x * x, axis=0, keepdims=True) + EPS) * qg
            parts = [xn, zq] if g == 0 else [zq, xn]
            qn_out[g, c] = _bf(jnp.concatenate(parts, axis=0))
            vs_out[g, SEL_PAD // Q_BLOCK + c] = _bf(yt_ref[TR_VS + g * DH:TR_VS + (g + 1) * DH, cs])
            vw_out[g, WINDOW // Q_BLOCK + c] = _bf(yt_ref[TR_VW + g * DH:TR_VW + (g + 1) * DH, cs])
        for j in range(SEL_PAD // Q_BLOCK):
            vs_out[g, j] = zv
        for j in range(WINDOW // Q_BLOCK):
            vw_out[g, j] = zv

    def norm_keys(src, dst, gain, pad):
        dst[0:pad, :] = jnp.zeros((pad, LANES), BF16)

        def body(i, carry):
            r0 = pl.multiple_of(i * Q_BLOCK, Q_BLOCK)
            x = src[pl.ds(r0, Q_BLOCK), :]
            dst[pl.ds(pad + r0, Q_BLOCK), :] = _bf(x * lax.rsqrt(half_ms(x) + EPS) * gain)
            return carry
        lax.fori_loop(0, N_QBLK, body, 0)

    norm_keys(ks_ref, ks_out, kn1_ref[...], SEL_PAD)
    norm_keys(kw_ref, kw_out, kn2_ref[...], WINDOW)


def _nsa_prep(y3, yt, k2, v2, consts):
    bsz = y3.shape[0]
    G = NSA_GROUPS
    W4 = NSA_HPG * Q_BLOCK

    def full(a):
        nd = a.ndim
        return pl.BlockSpec(a.shape, lambda b, nd=nd: (0,) * nd)

    def per_b(shape):
        nd = len(shape)
        return pl.BlockSpec((None,) + shape, lambda b, nd=nd: (b,) + (0,) * nd)

    shapes = [((G, N_QBLK, 2 * NSA_DH, W4), BF16),
              ((SEL_PAD + SEQ, LANES), BF16), ((WINDOW + SEQ, LANES), BF16),
              ((G, N_SEG, LANES), BF16), ((G, NSA_DH, N_SEG), BF16),
              ((G, (SEL_PAD + SEQ) // Q_BLOCK, NSA_DH, Q_BLOCK), BF16),
              ((G, (WINDOW + SEQ) // Q_BLOCK, NSA_DH, Q_BLOCK), BF16)]
    return pl.pallas_call(
        _nsa_prep_kernel,
        grid=(bsz,),
        in_specs=[per_b((N_TROWS, SEQ)),
                  pl.BlockSpec((None, SEQ, LANES), lambda b: (b, 0, CB_CKS)),
                  pl.BlockSpec((None, SEQ, LANES), lambda b: (b, 0, CB_CKW)),
                  per_b((N_SEG, CMP_STRIDE * LANES)), per_b((N_SEG, CMP_STRIDE * LANES))]
                 + [full(a) for a in consts],
        out_specs=[per_b(s) for s, _ in shapes],
        out_shape=[jax.ShapeDtypeStruct((bsz,) + s, dt) for s, dt in shapes],
        compiler_params=_params("parallel"),
        name="nsa_prep",
    )(yt, y3, y3, k2, v2, *consts)


def _nsa_kernel(q_ref, ck_ref, cv_ref, ks_ref, vs_ref, kw_ref, vw_ref, bct_ref, bt_ref, bw_ref, b31_ref,
                cg_ref, o_ref, sel_s, far_s, bc_s):
    c = pl.program_id(2)
    W4 = NSA_HPG * Q_BLOCK
    q = q_ref[...]
    b31 = b31_ref[...]

    def block_mask(rows):
        m1 = jnp.concatenate([jnp.broadcast_to(r, (SEL_BLOCK, Q_BLOCK)) for r in rows], axis=0)
        return jnp.concatenate([m1] * NSA_HPG, axis=1) > 0.5

    def v_tiles(ref, first, n):
        return jnp.concatenate([ref[first + j] for j in range(n)], axis=1)

    r0 = pl.multiple_of(c * Q_BLOCK, Q_BLOCK)
    n_w = WINDOW + Q_BLOCK
    qk_cmp = _mm(ck_ref[...], q)
    qk_near = _mm(ks_ref[pl.ds(r0, 2 * Q_BLOCK), :], q)
    qk_win = _mm(kw_ref[pl.ds(r0, n_w), :], q)

    nrow = lax.broadcasted_iota(jnp.int32, (N_SEG + 16, W4), 0) - 16
    near0 = (Q_BLOCK // CMP_STRIDE) * c - 16
    bc_s[...] = jnp.where(nrow < near0, b31, NEG)
    bc_s[pl.ds(pl.multiple_of(near0 + 16, 8), CMP_NEAR), :] = bct_ref[...]
    s = qk_cmp + bc_s[16:, :]
    e = jnp.where(s > 0.1 * NEG, jnp.exp(s - jnp.max(s, axis=0, keepdims=True)), 0.0)
    p = e / jnp.maximum(jnp.sum(e, axis=0, keepdims=True), 1e-30)
    o_cmp = _mm(cv_ref[...], _bf(p))

    psum = (p[:, 0:Q_BLOCK] + p[:, Q_BLOCK:2 * Q_BLOCK] + p[:, 2 * Q_BLOCK:3 * Q_BLOCK]
            + p[:, 3 * Q_BLOCK:4 * Q_BLOCK])
    jj = lax.broadcasted_iota(jnp.int32, (N_SEL, N_SEG), 0)
    nn = lax.broadcasted_iota(jnp.int32, (N_SEL, N_SEG), 1)
    ratio = SEL_BLOCK // CMP_STRIDE
    ov = jnp.where((nn >= ratio * jj - 1) & (nn <= ratio * jj + ratio - 1) & (nn < N_SEG - 1), 1.0, 0.0)
    imp = _mm(ov, psum, HP)

    wrow = lax.broadcasted_iota(jnp.int32, (n_w, W4), 0)
    s = jnp.where(wrow >= WINDOW - c * Q_BLOCK, qk_win + bw_ref[...], NEG)
    pw = jnp.exp(s - jnp.max(s, axis=0, keepdims=True))
    o_win = _mm(v_tiles(vw_ref, c, n_w // Q_BLOCK), _bf(pw)) / jnp.sum(pw, axis=0, keepdims=True)

    jb =lax.broadcasted_iota(jnp.int32, (N_SEL, Q_BLOCK), 0)
    tq = c * Q_BLOCK + lax.broadcasted_iota(jnp.int32, (N_SEL, Q_BLOCK), 1)
    cur = tq >> 6
    imp = jnp.where((jb == 0) | (jb == cur), FORCE_SCORE, jnp.where(jb <= cur, imp, -1.0))
    rank = jnp.zeros((N_SEL, Q_BLOCK), F32)
    for jp in range(N_SEL):
        rowv = imp[jp:jp + 1, :]
        rank = rank + jnp.where(rowv > imp, 1.0, 0.0)
        if jp < N_SEL - 1:
            rank = rank + jnp.where(jb > jp, jnp.where(rowv == imp, 1.0, 0.0), 0.0)
    sel = jnp.where(rank < SEL_TOPN, 1.0, 0.0)
    sel_s[0:8, :] = jnp.zeros((8, Q_BLOCK), F32)
    sel_s[8:, :] = sel
    far_s[...] = jnp.where(jb < 2 * (c - 1), sel, 0.0)

    near = block_mask([sel_s[pl.ds(6 + 2 * c + j, 1), :] for j in range(4)])
    s = jnp.where(near, qk_near + bt_ref[...], NEG)
    m_run = jnp.max(s, axis=0, keepdims=True)
    pe = jnp.exp(s - m_run)
    l_run = jnp.sum(pe, axis=0, keepdims=True)
    acc = _mm(v_tiles(vs_ref, c, 2), _bf(pe))

    def far_step(st, carry):
        m_old, l_old, acc_old = carry
        k0 = pl.multiple_of(SEL_PAD + st * FAR_TILE, Q_BLOCK)
        b0 = pl.multiple_of(st * (FAR_TILE // SEL_BLOCK), 8)
        rows = far_s[pl.ds(b0, FAR_TILE // SEL_BLOCK), :]
        msk = block_mask([rows[j:j + 1, :] for j in range(FAR_TILE // SEL_BLOCK)])
        sc = jnp.where(msk, _mm(ks_ref[pl.ds(k0, FAR_TILE), :], q) + b31, NEG)
        m_new = jnp.maximum(m_old, jnp.max(sc, axis=0, keepdims=True))
        alpha = jnp.exp(m_old - m_new)
        pf = jnp.exp(sc - m_new)
        l_new = alpha * l_old + jnp.sum(pf, axis=0, keepdims=True)
        vt = v_tiles(vs_ref, 1 + st * (FAR_TILE // Q_BLOCK), FAR_TILE // Q_BLOCK)
        return m_new, l_new, alpha * acc_old + _mm(vt, _bf(pf))

    n_far = (c + 2) // (FAR_TILE // Q_BLOCK)
    m_run, l_run, acc = lax.fori_loop(0, n_far, far_step, (m_run, l_run, acc))
    o_sel = acc / l_run

    gates = _sigmoid(cg_ref[...])

    def gate_row(r):
        return jnp.concatenate([gates[k * 3 + r:k * 3 + r + 1, :] for k in range(NSA_HPG)], axis=1)

    res = gate_row(0) * o_cmp + gate_row(1) * o_sel + gate_row(2) * o_win
    o_ref[...] = jnp.concatenate([res[:, k * Q_BLOCK:(k + 1) * Q_BLOCK].T for k in range(NSA_HPG)], axis=1)


def _nsa_kernel2(q_ref, ck_ref, cv_ref, ks_ref, vs_ref, kw_ref, vw_ref, bct_ref, bt_ref, bw_ref, b31_ref,
                 cg_ref, o_ref, sel_s, far_s, bc_s):
    c = pl.program_id(1)
    G = NSA_GROUPS
    W4 = NSA_HPG * Q_BLOCK
    r0 = pl.multiple_of(c * Q_BLOCK, Q_BLOCK)
    n_w = WINDOW + Q_BLOCK
    far_blocks = FAR_TILE // SEL_BLOCK
    far_tiles = FAR_TILE // Q_BLOCK

    def block_mask(rows):
        m1 = jnp.concatenate([jnp.broadcast_to(r, (SEL_BLOCK, Q_BLOCK)) for r in rows], axis=0)
        return jnp.concatenate([m1] * NSA_HPG, axis=1) > 0.5

    def v_tiles(ref, g, first, n):
        return jnp.concatenate([ref[g, first + j] for j in range(n)], axis=1)

    jb = lax.broadcasted_iota(jnp.int32, (N_SEL, Q_BLOCK), 0)
    tq = c * Q_BLOCK + lax.broadcasted_iota(jnp.int32, (N_SEL, Q_BLOCK), 1)
    cur = tq >> 6
    jj = lax.broadcasted_iota(jnp.int32, (N_SEL, N_SEG), 0)
    nn = lax.broadcasted_iota(jnp.int32, (N_SEL, N_SEG), 1)
    ratio = SEL_BLOCK // CMP_STRIDE
    ov = jnp.where((nn >= ratio * jj - 1) & (nn <= ratio * jj + ratio - 1) & (nn < N_SEG - 1), 1.0, 0.0)
    nrow = lax.broadcasted_iota(jnp.int32, (N_SEG + 16, W4), 0) - 16
    near0 = (Q_BLOCK // CMP_STRIDE) * c - 16
    wrow_ok = lax.broadcasted_iota(jnp.int32, (n_w, W4), 0) >= WINDOW - c * Q_BLOCK

    def front(g):
        q = q_ref[g]
        b31 = b31_ref[g]
        qk_cmp = _mm(ck_ref[g], q)
        qk_near = _mm(ks_ref[pl.ds(r0, 2 * Q_BLOCK), :], q)
        qk_win = _mm(kw_ref[pl.ds(r0, n_w), :], q)

        bc_s[g] = jnp.where(nrow < near0, b31, NEG)
        bc_s[g, pl.ds(pl.multiple_of(near0 + 16, 8), CMP_NEAR), :] = bct_ref[g]
        s = qk_cmp + bc_s[g, 16:, :]
        e = jnp.where(s > 0.1 * NEG, jnp.exp2(s - jnp.max(s, axis=0, keepdims=True)), 0.0)
        p = e / jnp.maximum(jnp.sum(e, axis=0, keepdims=True), 1e-30)
        o_cmp = _mm(cv_ref[g], _bf(p))

        psum = (p[:, 0:Q_BLOCK] + p[:, Q_BLOCK:2 * Q_BLOCK] + p[:, 2 * Q_BLOCK:3 * Q_BLOCK]
                + p[:, 3 * Q_BLOCK:4 * Q_BLOCK])
        imp = _mm(ov, psum, HP)

        s = jnp.where(wrow_ok, qk_win + bw_ref[g], NEG)
        pw = jnp.exp2(s - jnp.max(s, axis=0, keepdims=True))
        o_win = _mm(v_tiles(vw_ref, g, c, n_w // Q_BLOCK), _bf(pw)) / jnp.sum(pw, axis=0, keepdims=True)

        imp = jnp.where((jb == 0) | (jb == cur), FORCE_SCORE, jnp.where(jb <= cur, imp, -1.0))
        rank = jnp.zeros((N_SEL, Q_BLOCK), F32)
        for jp in range(N_SEL):
            rowv = imp[jp:jp + 1, :]
            rank = rank + jnp.where(rowv > imp, 1.0, 0.0)
            if jp < N_SEL - 1:
                rank = rank + jnp.where(jb > jp, jnp.where(rowv == imp, 1.0, 0.0), 0.0)
        sel = jnp.where(rank < SEL_TOPN, 1.0, 0.0)
        sel_s[g, 0:8, :] = jnp.zeros((8, Q_BLOCK), F32)
        sel_s[g, 8:, :] = sel
        far_sel = jnp.concatenate([jnp.where(jb < 2 * (c - 1), sel, 0.0)] * NSA_HPG, axis=1)
        far_s[g] = jnp.where(far_sel > 0.5, b31, NEG)

        near = block_mask([sel_s[g, pl.ds(6 + 2 * c + j, 1), :] for j in range(4)])
        s = jnp.where(near, qk_near + bt_ref[g], NEG)
        m_run = jnp.max(s, axis=0, keepdims=True)
        pe = jnp.exp2(s - m_run)
        l_run = jnp.sum(pe, axis=0, keepdims=True)
        acc = _mm(v_tiles(vs_ref, g, c, 2), _bf(pe))
        return (q, b31, o_cmp, o_win), (m_run, l_run, acc)

    fronts = [front(g) for g in range(G)]

    def far_step(st, carry):
        k0 = pl.multiple_of(SEL_PAD + st * FAR_TILE, Q_BLOCK)
        b0 = pl.multiple_of(st * far_blocks, 8)
        kf = ks_ref[pl.ds(k0, FAR_TILE), :]
        out = []
        for g in range(G):
            m_old, l_old, acc_old = carry[g]
            q, b31 = fronts[g][0][:2]
            rows = far_s[g, pl.ds(b0, far_blocks), :]
            add = jnp.concatenate([jnp.broadcast_to(rows[j:j + 1, :], (SEL_BLOCK, W4))
                                   for j in range(far_blocks)], axis=0)
            sc = _mm(kf, q) + add
            m_new = jnp.maximum(m_old, jnp.max(sc, axis=0, keepdims=True))
            alpha = jnp.exp2(m_old - m_new)
            pf = jnp.exp2(sc - m_new)
            l_new = alpha * l_old + jnp.sum(pf, axis=0, keepdims=True)
            vt = v_tiles(vs_ref, g, 1 + st * far_tiles, far_tiles)
            out.append((m_new, l_new, alpha * acc_old + _mm(vt, _bf(pf))))
        return tuple(out)

    n_far = (c + 2) // far_tiles
    finals = lax.fori_loop(0, n_far, far_step, tuple(f[1] for f in fronts))

    gates = _sigmoid(cg_ref[...])
    outs = []
    for g in range(G):
        _, _, o_cmp, o_win = fronts[g][0]
        _, l_run, acc = finals[g]

        def gate_row(r):
            rows = [gates[g * CG_ROWS + k * 3 + r:g * CG_ROWS + k * 3 + r + 1, :] for k in range(NSA_HPG)]
            return jnp.concatenate(rows, axis=1)

        res = gate_row(0) * o_cmp + gate_row(1) * (acc / l_run) + gate_row(2) * o_win
        outs += [res[:, k * Q_BLOCK:(k + 1) * Q_BLOCK].T for k in range(NSA_HPG)]
    o_ref[...] = jnp.concatenate(outs, axis=1)


def _nsa_attend(qn, ck, cv, ksn, vst, kwn, vwt, tables, yt):
    bsz = qn.shape[0]
    G = NSA_GROUPS
    W4 = NSA_HPG * Q_BLOCK
    bct, bt, bw, b31 = tables

    def per_b(a):
        shape = a.shape[1:]
        return pl.BlockSpec((None,) + shape, lambda b, c, n=len(shape): (b,) + (0,) * n)

    def full(a):
        return pl.BlockSpec(a.shape, lambda b, c, n=a.ndim: (0,) * n)

    return pl.pallas_call(
        _nsa_kernel2,
        grid=(bsz, N_QBLK),
        in_specs=[
            pl.BlockSpec((None, G, None, 2 * NSA_DH, W4), lambda b, c: (b, 0, c, 0, 0)),
            per_b(ck), per_b(cv), per_b(ksn), per_b(vst), per_b(kwn), per_b(vwt),
            full(bct), full(bt), full(bw), full(b31),
            pl.BlockSpec((None, G * CG_ROWS, Q_BLOCK), lambda b, c: (b, TR_CG // (G * CG_ROWS), c)),
        ],
        out_specs=pl.BlockSpec((None, Q_BLOCK, NSA_HEADS * NSA_DH), lambda b, c: (b, c, 0)),
        out_shape=jax.ShapeDtypeStruct((bsz, SEQ, NSA_HEADS * NSA_DH), F32),
        scratch_shapes=[pltpu.VMEM((G, 8 + N_SEL, Q_BLOCK), F32), pltpu.VMEM((G, N_SEL, W4), F32),
                        pltpu.VMEM((G, 16 + N_SEG, W4), F32)],
        compiler_params=_params("parallel", "arbitrary"),
        name="nsa_attend",
    )(qn, ck, cv, ksn, vst, kwn, vwt, bct, bt, bw, b31, yt)


def _bucket_starts():
    max_exact = REL_BUCKETS // 2
    d = np.arange(4 * REL_MAX_DIST)
    large = max_exact + (np.log(np.maximum(d, 1) / max_exact) / math.log(REL_MAX_DIST / max_exact)
                         * (REL_BUCKETS - max_exact)).astype(np.int64)
    bucket = np.where(d < max_exact, d, np.minimum(large, REL_BUCKETS - 1))
    assert np.all(np.diff(bucket) >= 0) and bucket[-1] == REL_BUCKETS - 1
    return [int(np.argmax(bucket >= j)) for j in range(REL_BUCKETS)]


BUCKET_STARTS = _bucket_starts()
TABLE_ROWS = 128


def _bias_kernel(rb_ref, bct_ref, bt_ref, bw_ref):
    W4 = NSA_HPG * Q_BLOCK
    rb = rb_ref[...] * LOG2E

    def table(rows, r0, dist_fn, limit):
        r = r0 + lax.broadcasted_iota(jnp.int32, (rows, W4), 0)
        i = lax.broadcasted_iota(jnp.int32, (rows, W4), 1) & (Q_BLOCK - 1)
        dist = dist_fn(r, i)
        acc = jnp.broadcast_to(rb[0:1, :], (rows, W4))
        for j in range(1, REL_BUCKETS):
            acc = jnp.where(dist >= BUCKET_STARTS[j], rb[j:j + 1, :], acc)
        return jnp.where((dist < 0) | (dist >= limit), NEG, acc)

    big = 1 << 30
    bct_ref[...] = table(CMP_NEAR, 0, lambda r, i: i - CMP_STRIDE * r + (16 * CMP_STRIDE - CMP_BLOCK + 1), big)

    def near_rows(k, carry):
        r0 = pl.multiple_of(k * TABLE_ROWS, TABLE_ROWS)
        bt_ref[pl.ds(r0, TABLE_ROWS), :] = table(TABLE_ROWS, r0, lambda r, i: Q_BLOCK + i - r, big)
        return carry

    lax.fori_loop(0, 2 * Q_BLOCK // TABLE_ROWS, near_rows, 0)

    def win_rows(k, carry):
        r0 = pl.multiple_of(k * TABLE_ROWS, TABLE_ROWS)
        bw_ref[pl.ds(r0, TABLE_ROWS), :] = table(TABLE_ROWS, r0, lambda r, i: WINDOW + i - r, WINDOW)
        return carry

    lax.fori_loop(0, (WINDOW + Q_BLOCK) // TABLE_ROWS, win_rows, 0)


def _bias_tables(rel_bias):
    G, K = NSA_GROUPS, NSA_HPG
    W4 = K * Q_BLOCK
    rb = jnp.repeat(rel_bias.reshape(REL_BUCKETS, G, K).transpose(1, 0, 2), Q_BLOCK, axis=2)
    shapes = [(G, CMP_NEAR, W4), (G, 2 * Q_BLOCK, W4), (G, WINDOW + Q_BLOCK, W4)]
    bct, bt, bw = pl.pallas_call(
        _bias_kernel,
        grid=(G,),
        in_specs=[pl.BlockSpec((None, REL_BUCKETS, W4), lambda g: (g, 0, 0))],
        out_specs=[pl.BlockSpec((None,) + s[1:], lambda g: (g, 0, 0)) for s in shapes],
        out_shape=[jax.ShapeDtypeStruct(s, F32) for s in shapes],
        compiler_params=_params("parallel"),
        name="bias_tables",
    )(rb)
    return bct, bt, bw, rb[:, REL_BUCKETS - 1:, :] * LOG2E


def _nsa(y3, yt, q_norm, k_norm, cmp_pe, w_cmp, tables):
    bsz = y3.shape[0]
    G, DH = NSA_GROUPS, NSA_DH

    def segments(cb):
        return y3[:, :, cb * LANES:(cb + 1) * LANES].reshape(bsz, N_SEG, CMP_STRIDE * LANES)

    def both_groups(w):
        return jnp.einsum('lde,gh->lgdhe', w, jnp.eye(G, dtype=w.dtype)).reshape(CMP_STRIDE * LANES, LANES)

    def pe_row(pe):
        return jnp.tile(pe[:, None, :], (1, G, 1)).reshape(1, CMP_STRIDE * LANES)

    wk = w_cmp[0].reshape(2, CMP_STRIDE, DH, DH)
    wv = w_cmp[1].reshape(2, CMP_STRIDE, DH, DH)
    pe = cmp_pe.reshape(2, 2, CMP_STRIDE, DH)
    gain2 = lambda g: jnp.tile(g, G).reshape(1, LANES)
    consts = [_bf(both_groups(wk[0])), _bf(both_groups(wk[1])),
              _bf(both_groups(wv[0]).T), _bf(both_groups(wv[1]).T),
              pe_row(pe[0, 0]), pe_row(pe[0, 1]), pe_row(pe[1, 0]), pe_row(pe[1, 1]),
              q_norm.reshape(DH, 1), gain2(k_norm[0]), gain2(k_norm[1]), gain2(k_norm[2])]
    qn, ksn, kwn, ck, cv, vst, vwt = _nsa_prep(y3, yt, segments(CB_CKC), segments(CB_CVC), consts)
    return _nsa_attend(qn, ck, cv, ksn, vst, kwn, vwt, tables, yt)


def _merge_kernel(h_ref, g_ref, ya_ref, yb_ref, yc_ref, wg_ref, wb_ref, wo_ref, o_ref):
    x = h_ref[...]
    u = _bf(_rms_rows(x, g_ref[...]))
    acc = None
    for n, y_ref in enumerate((ya_ref, yb_ref, yc_ref)):
        t = _sigmoid(_mm(u, wg_ref[:, n * D_MODEL:(n + 1) * D_MODEL])) * _mm(_bf(y_ref[...]), wb_ref[n])
        acc = t if acc is None else acc + t
    o_ref[...] = x + _mm(_bf(acc), wo_ref[...])


def _merge(h2, gain, ya, yb, yc, wgt, wb, wo):
    m, d = h2.shape
    tm = 512
    row = lambda w: pl.BlockSpec((tm, w), lambda i: (i, 0))
    return pl.pallas_call(
        _merge_kernel,
        grid=(m // tm,),
        in_specs=[row(d), pl.BlockSpec((1, d), lambda i: (0, 0)),
                  row(BRANCH_W), row(BRANCH_W), row(BRANCH_W),
                  pl.BlockSpec((d, N_BRANCH * d), lambda i: (0, 0)),
                  pl.BlockSpec((N_BRANCH, BRANCH_W, d), lambda i: (0, 0, 0)),
                  pl.BlockSpec((d, d), lambda i: (0, 0))],
        out_specs=row(d),
        out_shape=jax.ShapeDtypeStruct((m, d), F32),
        compiler_params=_params("parallel"),
        name="merge",
    )(h2, gain, ya, yb, yc, wgt, wb, wo)


def _ffn_kernel(h_ref, g_ref, wg_ref, wu_ref, wo_ref, o_ref, u_ref, acc_ref):
    f = pl.program_id(1)

    @pl.when(f == 0)
    def _():
        x = h_ref[...]
        u_ref[...] = _bf(_rms_rows(x, g_ref[...]))
        acc_ref[...] = x

    u = u_ref[...]
    gt = _mm(u, wg_ref[...])
    up = _mm(u, wu_ref[...])
    acc_ref[...] += _mm(_bf(gt * _sigmoid(gt) * up), wo_ref[...])

    @pl.when(f == pl.num_programs(1) - 1)
    def _():
        o_ref[...] = acc_ref[...]


def _ffn(h2, gain, w_in, w_out):
    m, d = h2.shape
    tm, nf = 512, 2
    tf = D_FF // nf
    return pl.pallas_call(
        _ffn_kernel,
        grid=(m // tm, nf),
        in_specs=[pl.BlockSpec((tm, d), lambda i, f: (i, 0)),
                  pl.BlockSpec((1, d), lambda i, f: (0, 0)),
                  pl.BlockSpec((d, tf), lambda i, f: (0, f)),
                  pl.BlockSpec((d, tf), lambda i, f: (0, f + nf)),
                  pl.BlockSpec((tf, d), lambda i, f: (f, 0))],
        out_specs=pl.BlockSpec((tm, d), lambda i, f: (i, 0)),
        out_shape=jax.ShapeDtypeStruct((m, d), F32),
        scratch_shapes=[pltpu.VMEM((tm, d), BF16), pltpu.VMEM((tm, d), F32)],
        compiler_params=_params("parallel", "arbitrary"),
        name="ffn",
    )(h2, gain, w_in, w_in, w_out)


def _ple_kernel(h_ref, g_ref, p_ref, wg_ref, wp_ref, o_ref):
    x = h_ref[...]
    gate = _sigmoid(_mm(_bf(_rms_rows(x, g_ref[...])), wg_ref[...]))
    o_ref[...] = x + gate * _mm(_bf(p_ref[...]), wp_ref[...])


def _ple(h2, gain, p2, wg, wp):
    m, d = h2.shape
    tm = 512
    return pl.pallas_call(
        _ple_kernel,
        grid=(m // tm,),
        in_specs=[pl.BlockSpec((tm, d), lambda i: (i, 0)),
                  pl.BlockSpec((1, d), lambda i: (0, 0)),
                  pl.BlockSpec((tm, PLE_DIM), lambda i: (i, 0)),
                  pl.BlockSpec((d, d), lambda i: (0, 0)),
                  pl.BlockSpec((PLE_DIM, d), lambda i: (0, 0))],
        out_specs=pl.BlockSpec((tm, d), lambda i: (i, 0)),
        out_shape=jax.ShapeDtypeStruct((m, d), F32),
        compiler_params=_params("parallel"),
        name="ple",
    )(h2, gain, p2, wg, wp)


W_IN_GROUPS = ((0, 0, 16), (16, 8, 16), (32, 16, 10))
W_IN_GATES = (42, 40, 24)
W_IN_MIX_BLOCKS = 44
W_IN_SCALAR_BLOCKS = (16, 32, 42)
W_IN_SCALAR_LANES = (8, 16, 40)


def _regroup_kernel(src_ref, off_ref, a_ref, b_ref, s0_ref, s1_ref, s2_ref, o_ref, *, offsets, n_main):
    j = pl.program_id(0)
    x = jnp.concatenate([a_ref[...], b_ref[...]], axis=1)
    for off in offsets:
        @pl.when((off_ref[j] == off) & (j < n_main))
        def _(off=off):
            o_ref[...] = _bf(x[:, off:off + LANES])

    @pl.when(j == n_main)
    def _():
        lane = lax.broadcasted_iota(jnp.int32, (D_MODEL, LANES), 1)
        l0, l1, l2 = W_IN_SCALAR_LANES
        sc = jnp.where(lane < l0, s0_ref[...], jnp.where(lane < l1, s1_ref[...], s2_ref[...]))
        o_ref[...] = _bf(jnp.where(lane < l2, sc, 0.0))

    @pl.when(j > n_main)
    def _():
        o_ref[...] = jnp.zeros(o_ref.shape, BF16)


def _regroup(w_in, layer, groups, n_out, with_scalars):
    src = np.concatenate([blk + np.arange(n) for blk, _, n in groups])
    off = np.concatenate([np.full(n, o) for _, o, n in groups])
    n_main = len(src)
    pad = n_out - n_main
    src = np.concatenate([src, np.zeros(pad, np.int64)]).astype(np.int32)
    off = np.concatenate([off, np.zeros(pad, np.int64)]).astype(np.int32)
    blk = lambda f: pl.BlockSpec((None, D_MODEL, LANES), f)
    fixed = [blk(lambda j, s, o, b=b: (layer, 0, b)) for b in W_IN_SCALAR_BLOCKS]
    return pl.pallas_call(
        functools.partial(_regroup_kernel, offsets=tuple(sorted({o for _, o, _ in groups})),
                          n_main=n_main if with_scalars else n_out),
        grid_spec=pltpu.PrefetchScalarGridSpec(
            num_scalar_prefetch=2,
            grid=(n_out,),
            in_specs=[blk(lambda j, s, o: (layer, 0, s[j])), blk(lambda j, s, o: (layer, 0, s[j] + 1))] + fixed,
            out_specs=pl.BlockSpec((D_MODEL, LANES), lambda j, s, o: (0, j))),
        out_shape=jax.ShapeDtypeStruct((D_MODEL, n_out * LANES), BF16),
        compiler_params=_params("arbitrary"),
        name="regroup_w_in",
    )(jnp.asarray(src), jnp.asarray(off), w_in, w_in, w_in, w_in, w_in)


def _split_w_mix(w_mix):
    blk = lambda b, n=1: w_mix[:, b * LANES:(b + n) * LANES]
    main = jnp.concatenate([blk(0, 32), blk(36, 3), blk(40), blk(42, 2)], axis=1)
    main = jnp.pad(main, ((0, 0), (0, N_COLS - main.shape[1])))
    cg = blk(42)[:, SM_CG:SM_CG + NSA_HEADS * 3].reshape(D_MODEL, NSA_GROUPS, NSA_HPG * 3)
    cg = jnp.pad(cg, ((0, 0), (0, 0), (0, CG_ROWS - NSA_HPG * 3))).reshape(D_MODEL, NSA_GROUPS * CG_ROWS)
    return main, jnp.concatenate([blk(32, 4), blk(39), blk(41), cg], axis=1).T


def _lane_row(vals, offset):
    return jnp.zeros((1, LANES), F32).at[0, offset:offset + vals.shape[0]].set(vals)


def kernel(x, p, rel_bias, norm_mix, w_in, conv_w, gdn_a_log, gdn_dt_bias, gdn_norm, mlstm_b_i, mlstm_b_f,
           mlstm_norm, nsa_q_norm, nsa_k_norm, nsa_cmp_pe, nsa_w_cmp, w_branch, w_out, norm_ffn, w_ffn_in,
           w_ffn_out, norm_ple, w_ple_gate, w_ple_proj):
    bsz, seq, d = x.shape
    assert seq == SEQ and d == D_MODEL
    depth = w_in.shape[0]
    m = bsz * seq
    tables = _bias_tables(rel_bias)
    h2 = x.reshape(m, d)
    for l in range(depth):
        w_main, w_tr = _split_w_mix(_regroup(w_in, l, W_IN_GROUPS, W_IN_MIX_BLOCKS, True))
        w_gate = _regroup(w_in, l, (W_IN_GATES,), W_IN_GATES[2], False)
        gain = norm_mix[l].reshape(1, d)
        y3 = _in_proj(h2, gain, w_main).reshape(bsz, seq, N_COLS)
        yt = _in_proj_t(h2, gain, w_tr, bsz)
        ya = _gdn(y3, conv_w[l], _lane_row(gdn_a_log[l], SM_AA), _lane_row(gdn_dt_bias[l], SM_AA),
                  gdn_norm[l].reshape(1, D_HEAD_REC))
        yb = _mlstm(y3, _lane_row(mlstm_b_i[l], SM_BI), _lane_row(mlstm_b_f[l], SM_BF),
                    mlstm_norm[l].reshape(1, D_HEAD_REC))
        yc = _nsa(y3, yt, nsa_q_norm[l], nsa_k_norm[l], nsa_cmp_pe[l], nsa_w_cmp[l], tables)
        h2 = _merge(h2, gain, ya.reshape(m, BRANCH_W), yb.reshape(m, BRANCH_W),
                    yc.reshape(m, BRANCH_W), w_gate, _bf(w_branch[l]), _bf(w_out[l]))
        h2 = _ffn(h2, norm_ffn[l].reshape(1, d), _bf(w_ffn_in[l]), _bf(w_ffn_out[l]))
        h2 = _ple(h2, norm_ple[l].reshape(1, d), p[l].reshape(m, PLE_DIM), _bf(w_ple_gate[l]),
                  _bf(w_ple_proj[l]))
    return h2.reshape(bsz, seq, d)
```

```python
import functools
import math

import numpy as np
import jax
import jax.numpy as jnp
from jax import lax
from jax.experimental import pallas as pl
from jax.experimental.pallas import tpu as pltpu

D_MODEL = 1024
SEQ = 2048
N_HEADS_REC = 4
D_HEAD_REC = 128
CHUNK = 64
GDN_HPS = 2
GDN_CONV = 4
NSA_HEADS = 8
NSA_GROUPS = 2
NSA_HPG = NSA_HEADS // NSA_GROUPS
NSA_DH = 64
CMP_BLOCK = 32
CMP_STRIDE = 16
SEL_BLOCK = 64
SEL_TOPN = 4
WINDOW = 512
Q_BLOCK = 128
N_QBLK = SEQ // Q_BLOCK
N_SEL = SEQ // SEL_BLOCK
N_SEG = SEQ // CMP_STRIDE
SEL_PAD = Q_BLOCK
FAR_TILE = 4 * Q_BLOCK
CMP_NEAR = 24
REL_BUCKETS = 32
REL_MAX_DIST = 128
N_BRANCH = 3
BRANCH_W = 512
D_FF = 2816
PLE_DIM = 256
EPS = 1e-6
NEG = -1e30
FORCE_SCORE = 1e4
LOG2E = math.log2(math.e)

LANES = 128
VMEM_LIMIT = 48 * 1024 * 1024

F32 = jnp.float32
BF16 = jnp.bfloat16
HP = lax.Precision.HIGHEST

CB_A = 0
CB_B = 16
CB_CKC, CB_CVC, CB_CKS, CB_CKW = 32, 33, 34, 35
CB_SMALL = 36
N_COLS = 40 * LANES
TR_Q = 0
TR_VS = NSA_HEADS * NSA_DH
TR_VW = TR_VS + NSA_GROUPS * NSA_DH
TR_CG = TR_VW + NSA_GROUPS * NSA_DH
CG_ROWS = 16
N_TROWS = TR_CG + NSA_GROUPS * CG_ROWS
SM_AA, SM_AB, SM_BI, SM_BF, SM_CG = 0, 4, 8, 12, 16


def _mm(a, b, precision=None):
    return lax.dot_general(a, b, (((1,), (0,)), ((), ())), precision=precision,
                           preferred_element_type=F32)


def _mm_nt(a, b, precision=None):
    return lax.dot_general(a, b, (((1,), (1,)), ((), ())), precision=precision,
                           preferred_element_type=F32)


def _mm_tn(a, b, precision=None):
    return lax.dot_general(a, b, (((0,), (0,)), ((), ())), precision=precision,
                           preferred_element_type=F32)


def _bf(x):
    return x.astype(BF16)


def _sigmoid(x):
    return 1.0 / (1.0 + jnp.exp(-x))


def _softplus(x):
    return jnp.maximum(x, 0.0) + jnp.log1p(jnp.exp(-jnp.abs(x)))


def _rms_rows(x, g):
    return x * lax.rsqrt(jnp.mean(x * x, axis=-1, keepdims=True) + EPS) * g


def _params(*sem):
    return pltpu.CompilerParams(dimension_semantics=sem, vmem_limit_bytes=VMEM_LIMIT)


def _proj_kernel(x_ref, g_ref, w_ref, wt_ref, o_ref, ot_ref):
    u = _bf(_rms_rows(x_ref[...], g_ref[...]))
    o_ref[...] = _mm(u, w_ref[...])
    ot_ref[...] = _mm_nt(wt_ref[...], u)


def _in_proj(x2, gain, w, wt, bsz):
    m, d = x2.shape
    n, nt = w.shape[1], wt.shape[0]
    tm = 256
    per_seq = SEQ // tm
    return pl.pallas_call(
        _proj_kernel,
        grid=(m // tm,),
        in_specs=[pl.BlockSpec((tm, d), lambda i: (i, 0)),
                  pl.BlockSpec((1, d), lambda i: (0, 0)),
                  pl.BlockSpec((d, n), lambda i: (0, 0)),
                  pl.BlockSpec((nt, d), lambda i: (0, 0))],
        out_specs=[pl.BlockSpec((tm, n), lambda i: (i, 0)),
                   pl.BlockSpec((None, nt, tm), lambda i: (i // per_seq, 0, i % per_seq))],
        out_shape=[jax.ShapeDtypeStruct((m, n), F32), jax.ShapeDtypeStruct((bsz, nt, SEQ), F32)],
        compiler_params=_params("parallel"),
        name="in_proj",
    )(x2, gain, w, wt)


def _chunk_masks():
    ri = lax.broadcasted_iota(jnp.int32, (CHUNK, CHUNK), 0)
    ci = lax.broadcasted_iota(jnp.int32, (CHUNK, CHUNK), 1)
    return ri, ci


def _chunk_cumsum(x):
    rowi = lax.broadcasted_iota(jnp.int32, x.shape, 0)
    s = 1
    while s < CHUNK:
        x = x + jnp.where(rowi >= s, pltpu.roll(x, s, axis=0), 0.0)
        s *= 2
    return x


def _pick_lane(x, lane_idx):
    lane = lax.broadcasted_iota(jnp.int32, x.shape, 1)
    return jnp.sum(jnp.where(lane == lane_idx, x, 0.0), axis=1, keepdims=True)


def _col_to_row(col, eye):
    return jnp.sum(jnp.where(eye, col, 0.0), axis=0, keepdims=True)


def _bd_groups(pk):
    grp = lax.broadcasted_iota(jnp.int32, pk.shape, 1) // CHUNK
    zero = jnp.zeros_like(pk)
    return jnp.concatenate([jnp.where(grp == j, pk, zero) for j in range(pk.shape[1] // CHUNK)], axis=0)


def _bd_wide(x):
    zero = jnp.zeros((x.shape[0], LANES), x.dtype)
    return jnp.concatenate([jnp.concatenate([x[:, :LANES], zero], axis=1),
                            jnp.concatenate([zero, x[:, LANES:]], axis=1)], axis=0)


def _inv_unit_lower(x, bd, eyef):
    xd = jnp.where(bd, x, 0.0)
    xo = jnp.where(bd, 0.0, x)
    mm = lambda a, b: _mm(_bf(a), _bd_groups(_bf(b)))

    def mm2(a1, a2, b):
        r = mm(jnp.concatenate([a1, a2], axis=0), b)
        return r[:CHUNK], r[CHUNK:]

    x2 = mm(xd, xd)
    p = eyef - xd
    x4, t = mm2(x2, p, x2)
    p = p + t
    x8, t = mm2(x4, p, x4)
    p = p + t
    p = p + mm(p, x8)
    m = mm(p, xo)
    m2 = mm(m, m)
    q = eyef - m
    q = q + mm(q, m2)
    return mm(q, p)


def _gdn_kernel(q_ref, k_ref, v_ref, z_ref, sm_ref, cwq_ref, cwk_ref, cwv_ref, alog_ref, dtb_ref,
                nw_ref, o_ref, qg_s, kd_s, u_s, w_s, at_s, eg_s):
    L = CHUNK
    W = 2 * LANES
    ri = lax.broadcasted_iota(jnp.int32, (L, 4 * L), 0)
    cp = lax.broadcasted_iota(jnp.int32, (L, 4 * L), 1)
    ci = cp & (L - 1)
    first = lax.broadcasted_iota(jnp.int32, (L, 2 * L), 1) < L
    tril = ri >= ci
    strict = ri > ci
    eye = ri == ci
    bd = (ri >> 4) == (ci >> 4)
    eyef = jnp.where(eye, 1.0, 0.0)
    alog = alog_ref[...]
    dtb = dtb_ref[...]
    nw = nw_ref[...]
    cwq, cwk, cwv = cwq_ref[...], cwk_ref[...], cwv_ref[...]
    ha = pl.program_id(1) * 2

    def conv_silu(ref, w, n, r0):
        cur = ref[pl.ds(r0, L), :]
        p0 = pl.multiple_of(jnp.maximum(r0 - 8, 0), 8)
        prev = jnp.where(n > 0, ref[pl.ds(p0, 8), :], 0.0)
        win = jnp.concatenate([prev, cur], axis=0)
        acc = cur * w[GDN_CONV - 1:GDN_CONV, :]
        for s in range(1, GDN_CONV):
            acc = acc + pltpu.roll(win, s, axis=0)[8:, :] * w[GDN_CONV - 1 - s:GDN_CONV - s, :]
        return acc * _sigmoid(acc)

    def wide(col_a, col_b, rows=L):
        return jnp.concatenate([jnp.broadcast_to(col_a, (rows, LANES)),
                                jnp.broadcast_to(col_b, (rows, LANES))], axis=1)

    def l2n(x):
        xx = x * x
        return x * wide(lax.rsqrt(jnp.sum(xx[:, :LANES], axis=-1, keepdims=True) + 1e-6),
                        lax.rsqrt(jnp.sum(xx[:, LANES:], axis=-1, keepdims=True) + 1e-6))

    def chunk_terms(n):
        r0 = pl.multiple_of(n * L, L)
        x = sm_ref[pl.ds(r0, L), :]
        g = _chunk_cumsum(-jnp.exp(alog) * _softplus(x + dtb))
        sx = _sigmoid(x)
        q = l2n(conv_silu(q_ref, cwq, n, r0)) * (D_HEAD_REC ** -0.5)
        k = l2n(conv_silu(k_ref, cwk, n, r0))
        v = conv_silu(v_ref, cwv, n, r0)
        gc_a, gc_b = _pick_lane(g, SM_AA + ha), _pick_lane(g, SM_AA + ha + 1)
        beta = wide(_pick_lane(sx, SM_AB + ha), _pick_lane(sx, SM_AB + ha + 1))
        kb = k * beta
        egc = wide(jnp.exp(gc_a), jnp.exp(gc_b))
        gl_a, gl_b = gc_a[L - 1:L, :], gc_b[L - 1:L, :]
        qg_s[pl.ds(r0, L), :] = _bf(q * egc)
        kd_s[pl.ds(r0, L), :] = _bf(k * wide(jnp.exp(gl_a - gc_a), jnp.exp(gl_b - gc_b)))
        eg_s[pl.ds(pl.multiple_of(n * 8, 8), 8), :] = wide(jnp.exp(gl_a), jnp.exp(gl_b), 8)
        gc_pk = jnp.where(first, gc_a, gc_b)
        return q, _bf(k), kb, _bd_wide(_bf(v * beta)), _bd_wide(_bf(kb * egc)), gc_pk

    zk = jnp.zeros((2 * L, W), BF16)

    def prep(i, carry):
        n0 = 2 * i
        r0 = pl.multiple_of(n0 * L, 2 * L)
        q0, k0, kb0, vb0, ke0, gc0 = chunk_terms(n0)
        q1, k1, kb1, vb1, ke1, gc1 = chunk_terms(n0 + 1)
        gc = jnp.concatenate([gc0, gc1], axis=1)
        diff = gc - jnp.sum(jnp.where(eye, gc, 0.0), axis=0, keepdims=True)
        decay = jnp.where(tril, jnp.exp(jnp.where(tril, diff, 0.0)), 0.0)
        lhs = jnp.concatenate([jnp.concatenate([kb0, kb1], axis=1),
                               jnp.concatenate([q0, q1], axis=1)], axis=0)
        rhs = jnp.concatenate([jnp.concatenate([_bd_wide(k0), zk], axis=1),
                               jnp.concatenate([zk, _bd_wide(k1)], axis=1)], axis=0)
        kq = _mm_nt(_bf(lhs), rhs)
        xm = jnp.where(strict, kq[:L] * decay, 0.0)
        t = _bf(_inv_unit_lower(xm, bd, eyef))
        attn = _bf(kq[L:] * decay)
        uw = _mm(t, jnp.concatenate([jnp.concatenate([vb0, zk, ke0, zk], axis=1),
                                     jnp.concatenate([zk, vb1, zk, ke1], axis=1)], axis=0))
        for j in range(2):
            rows = pl.ds(r0 + j * L, L)
            u_s[rows, :] = uw[:, j * W:(j + 1) * W]
            w_s[rows, :] = _bf(uw[:, (2 + j) * W:(3 + j) * W])
            at_s[rows, :] = attn[:, j * 2 * L:(j + 1) * 2 * L]
        return carry

    zs = jnp.zeros((D_HEAD_REC, D_HEAD_REC), BF16)

    def step(n, states):
        sa, sb = states
        r0 = pl.multiple_of(n * L, L)
        sbd = jnp.concatenate([jnp.concatenate([_bf(sa), zs], axis=1),
                               jnp.concatenate([zs, _bf(sb)], axis=1)], axis=0)
        eg = eg_s[pl.ds(pl.multiple_of(n * 8, 8), 1), :]
        r = _mm(jnp.concatenate([w_s[pl.ds(r0, L), :], qg_s[pl.ds(r0, L), :]], axis=0), sbd)
        vnb = _bf(u_s[pl.ds(r0, L), :] - r[:L])
        o = r[L:] + _mm(at_s[pl.ds(r0, L), :], _bd_wide(vnb))
        upd = _mm_tn(kd_s[pl.ds(r0, L), :], vnb)
        sa = sa * eg[:, :LANES] + upd[:LANES, :LANES]
        sb = sb * eg[:, LANES:] + upd[LANES:, LANES:]
        z = z_ref[pl.ds(r0, L), :]
        on = jnp.concatenate([_rms_rows(o[:, :LANES], nw), _rms_rows(o[:, LANES:], nw)], axis=1)
        o_ref[pl.ds(r0, L), :] = on * (z * _sigmoid(z))
        return sa, sb

    n_pairs = SEQ // (2 * L)

    def pair(i, states):
        states = step(2 * i + 1, step(2 * i, states))
        prep(i + 1, 0)
        return states

    zero = jnp.zeros((D_HEAD_REC, D_HEAD_REC), F32)
    prep(0, 0)
    states = lax.fori_loop(0, n_pairs - 1, pair, (zero, zero))
    step(2 * n_pairs - 1, step(2 * n_pairs - 2, states))


def _gdn(y3, conv_w, alog_row, dtb_row, norm_w):
    bsz = y3.shape[0]
    H = N_HEADS_REC
    W = 2 * LANES

    def col(off):
        return pl.BlockSpec((None, SEQ, W), lambda b, h, off=off: (b, 0, off // 2 + h))

    def cw(off):
        return pl.BlockSpec((GDN_CONV, W), lambda b, h, off=off: (0, off // 2 + h))

    row = pl.BlockSpec((1, LANES), lambda b, h: (0, 0))
    return pl.pallas_call(
        _gdn_kernel,
        grid=(bsz, H // 2),
        in_specs=[col(CB_A), col(CB_A + H), col(CB_A + 2 * H), col(CB_A + 3 * H),
                  pl.BlockSpec((None, SEQ, LANES), lambda b, h: (b, 0, CB_SMALL)),
                  cw(0), cw(H), cw(2 * H), row, row, row],
        out_specs=pl.BlockSpec((None, SEQ, W), lambda b, h: (b, 0, h)),
        out_shape=jax.ShapeDtypeStruct((bsz, SEQ, H * D_HEAD_REC), F32),
        scratch_shapes=[pltpu.VMEM((SEQ, W), BF16), pltpu.VMEM((SEQ, W), BF16),
                        pltpu.VMEM((SEQ, W), F32), pltpu.VMEM((SEQ, W), BF16),
                        pltpu.VMEM((SEQ, 2 * CHUNK), BF16),
                        pltpu.VMEM((SEQ // CHUNK * 8, W), F32)],
        compiler_params=_params("parallel", "parallel"),
        name="gdn",
    )(y3, y3, y3, y3, y3, conv_w, conv_w, conv_w, alog_row, dtb_row, norm_w)


def _mlstm_kernel(q_ref, k_ref, v_ref, og_ref, sm_ref, bi_ref, bf_ref, nw_ref, o_ref):
    L = CHUNK
    ri = lax.broadcasted_iota(jnp.int32, (L, 2 * L), 0)
    cp = lax.broadcasted_iota(jnp.int32, (L, 2 * L), 1)
    ci = cp & (L - 1)
    first = cp < L
    tril = ri >= ci
    eye = ri == ci
    bi = bi_ref[...]
    bfr = bf_ref[...]
    nw = nw_ref[...]
    ha = pl.program_id(1) * 2
    zs = jnp.zeros((D_HEAD_REC, D_HEAD_REC), BF16)

    def wide(col_a, col_b, rows=L):
        return jnp.concatenate([jnp.broadcast_to(col_a, (rows, LANES)),
                                jnp.broadcast_to(col_b, (rows, LANES))], axis=1)

    def step(n, carry):
        c_a, c_b, n_st, m_a, m_b = carry
        r0 = pl.multiple_of(n * L, L)
        q = q_ref[pl.ds(r0, L), :]
        k = k_ref[pl.ds(r0, L), :] * (D_HEAD_REC ** -0.5)
        vb = _bf(v_ref[pl.ds(r0, L), :])
        x = sm_ref[pl.ds(r0, L), :]
        lfc = _chunk_cumsum(-_softplus(-(x + bfr)))
        itx = x + bi
        b_a, b_b = _pick_lane(lfc, SM_BF + ha), _pick_lane(lfc, SM_BF + ha + 1)
        it_a, it_b = _pick_lane(itx, SM_BI + ha), _pick_lane(itx, SM_BI + ha + 1)
        b_pk = jnp.where(first, b_a, b_b)
        it_pk = jnp.where(first, it_a, it_b)
        b_row = jnp.sum(jnp.where(eye, b_pk, 0.0), axis=0, keepdims=True)
        it_row = jnp.sum(jnp.where(eye, it_pk, 0.0), axis=0, keepdims=True)
        dm = jnp.where(tril, b_pk - b_row + it_row, NEG)
        dmax_a = jnp.max(jnp.where(first, dm, NEG), axis=1, keepdims=True)
        dmax_b = jnp.max(jnp.where(first, NEG, dm), axis=1, keepdims=True)
        qk = _mm_nt(_bf(q), _bd_wide(_bf(k)))
        a_a, a_b = b_a + m_a, b_b + m_b
        mt_a, mt_b = jnp.maximum(a_a, dmax_a), jnp.maximum(a_b, dmax_b)
        sm = jnp.exp(dm - jnp.where(first, mt_a, mt_b)) * qk
        si_a, si_b = jnp.exp(a_a - mt_a), jnp.exp(a_b - mt_b)
        cbd = jnp.concatenate([jnp.concatenate([_bf(c_a), zs], axis=1),
                               jnp.concatenate([zs, _bf(c_b)], axis=1)], axis=0)
        num = _mm(jnp.concatenate([_bf(q * wide(si_a, si_b)), _bf(sm)], axis=1),
                  jnp.concatenate([cbd, _bd_wide(vb)], axis=0))
        qn = q * n_st
        den_a = (si_a * jnp.sum(qn[:, :LANES], axis=1, keepdims=True)
                 + jnp.sum(jnp.where(first, sm, 0.0), axis=1, keepdims=True))
        den_b = (si_b * jnp.sum(qn[:, LANES:], axis=1, keepdims=True)
                 + jnp.sum(jnp.where(first, 0.0, sm), axis=1, keepdims=True))
        hh = num / wide(jnp.maximum(jnp.abs(den_a), jnp.exp(-mt_a)),
                        jnp.maximum(jnp.abs(den_b), jnp.exp(-mt_b)))
        bl_a, bl_b = b_a[L - 1:L, :], b_b[L - 1:L, :]
        ds_a, ds_b = bl_a - b_a + it_a, bl_b - b_b + it_b
        mn_a = jnp.maximum(bl_a + m_a, jnp.max(ds_a, axis=0, keepdims=True))
        mn_b = jnp.maximum(bl_b + m_b, jnp.max(ds_b, axis=0, keepdims=True))
        wk = k * wide(jnp.exp(ds_a - mn_a), jnp.exp(ds_b - mn_b))
        sc_a, sc_b = jnp.exp(bl_a + m_a - mn_a), jnp.exp(bl_b + m_b - mn_b)
        upd = _mm_tn(_bf(wk), vb)
        c_a = sc_a * c_a + upd[:LANES, :LANES]
        c_b = sc_b * c_b + upd[LANES:, LANES:]
        n_st = wide(sc_a, sc_b, 1) * n_st + jnp.sum(wk, axis=0, keepdims=True)
        on = jnp.concatenate([_rms_rows(hh[:, :LANES], nw), _rms_rows(hh[:, LANES:], nw)], axis=1)
        o_ref[pl.ds(r0, L), :] = on * _sigmoid(og_ref[pl.ds(r0, L), :])
        return c_a, c_b, n_st, mn_a, mn_b

    zc = jnp.zeros((D_HEAD_REC, D_HEAD_REC), F32)
    z1 = jnp.zeros((1, 1), F32)
    lax.fori_loop(0, SEQ // L, step, (zc, zc, jnp.zeros((1, 2 * LANES), F32), z1, z1), unroll=4)


def _mlstm(y3, bi_row, bf_row, norm_w):
    bsz = y3.shape[0]
    H = N_HEADS_REC
    W = 2 * LANES

    def col(off):
        return pl.BlockSpec((None, SEQ, W), lambda b, h, off=off: (b, 0, off // 2 + h))

    row = pl.BlockSpec((1, LANES), lambda b, h: (0, 0))
    return pl.pallas_call(
        _mlstm_kernel,
        grid=(bsz, H // 2),
        in_specs=[col(CB_B), col(CB_B + H), col(CB_B + 2 * H), col(CB_B + 3 * H),
                  pl.BlockSpec((None, SEQ, LANES), lambda b, h: (b, 0, CB_SMALL)),
                  row, row, row],
        out_specs=pl.BlockSpec((None, SEQ, W), lambda b, h: (b, 0, h)),
        out_shape=jax.ShapeDtypeStruct((bsz, SEQ, H * D_HEAD_REC), F32),
        compiler_params=_params("parallel", "parallel"),
        name="mlstm",
    )(y3, y3, y3, y3, y3, bi_row, bf_row, norm_w)


def _nsa_prep_kernel(yt_ref, ks_ref, kw_ref, k2_ref, v2_ref, wkl_ref, wkh_ref, wvl_ref, wvh_ref,
                     pekl_ref, pekh_ref, pevl_ref, pevh_ref, qn_ref, kn0_ref, kn1_ref, kn2_ref,
                     qn_out, ks_out, kw_out, ck_out, cv_out, vs_out, vw_out):
    G, K, DH = NSA_GROUPS, NSA_HPG, NSA_DH
    lane = lax.broadcasted_iota(jnp.int32, (Q_BLOCK, LANES), 1)
    lo = lane < DH

    def half_ms(x):
        xx = x * x
        s0 = jnp.sum(jnp.where(lo, xx, 0.0), axis=1, keepdims=True)
        s1 = jnp.sum(jnp.where(lo, 0.0, xx), axis=1, keepdims=True)
        return jnp.where(lo, s0, s1) * (1.0 / DH)

    k2 = k2_ref[...]
    v2 = v2_ref[...]
    ckb = (_mm(_bf(k2 + pekl_ref[...]), wkl_ref[...])
           + pltpu.roll(_mm(_bf(k2 + pekh_ref[...]), wkh_ref[...]), N_SEG - 1, axis=0))
    ckn = ckb * lax.rsqrt(half_ms(ckb) + EPS) * kn0_ref[...]
    ck_out[0] = _bf(jnp.where(lo, ckn, 0.0))
    ck_out[1] = _bf(jnp.where(lo, 0.0, ckn))
    cvb = (_mm_nt(wvl_ref[...], _bf(v2 + pevl_ref[...]))
           + pltpu.roll(_mm_nt(wvh_ref[...], _bf(v2 + pevh_ref[...])), N_SEG - 1, axis=1))

    qg = qn_ref[...] * (DH ** -0.5 * LOG2E)
    zq = jnp.zeros((DH, K * Q_BLOCK), F32)
    zv = jnp.zeros((DH, Q_BLOCK), BF16)
    for g in range(G):
        cv_out[g] = _bf(cvb[g * DH:(g + 1) * DH, :])
        for c in range(N_QBLK):
            cs = slice(c * Q_BLOCK, (c + 1) * Q_BLOCK)
            r0 = TR_Q + g * K * DH
            x = jnp.concatenate([yt_ref[r0 + k * DH:r0 + (k + 1) * DH, cs] for k in range(K)], axis=1)
            xn = x * lax.rsqrt(jnp.mean(x * x, axis=0, keepdims=True) + EPS) * qg
            parts = [xn, zq] if g == 0 else [zq, xn]
            qn_out[g, c] = _bf(jnp.concatenate(parts, axis=0))
            vs_out[g, SEL_PAD // Q_BLOCK + c] = _bf(yt_ref[TR_VS + g * DH:TR_VS + (g + 1) * DH, cs])
            vw_out[g, WINDOW // Q_BLOCK + c] = _bf(yt_ref[TR_VW + g * DH:TR_VW + (g + 1) * DH, cs])
        for j in range(SEL_PAD // Q_BLOCK):
            vs_out[g, j] = zv
        for j in range(WINDOW // Q_BLOCK):
            vw_out[g, j] = zv

    def norm_keys(src, dst, gain, pad):
        dst[0:pad, :] = jnp.zeros((pad, LANES), BF16)

        def body(i, carry):
            r0 = pl.multiple_of(i * Q_BLOCK, Q_BLOCK)
            x = src[pl.ds(r0, Q_BLOCK), :]
            dst[pl.ds(pad + r0, Q_BLOCK), :] = _bf(x * lax.rsqrt(half_ms(x) + EPS) * gain)
            return carry
        lax.fori_loop(0, N_QBLK, body, 0)

    norm_keys(ks_ref, ks_out, kn1_ref[...], SEL_PAD)
    norm_keys(kw_ref, kw_out, kn2_ref[...], WINDOW)


def _nsa_prep(y3, yt, k2, v2, consts):
    bsz = y3.shape[0]
    G = NSA_GROUPS
    W4 = NSA_HPG * Q_BLOCK

    def full(a):
        nd = a.ndim
        return pl.BlockSpec(a.shape, lambda b, nd=nd: (0,) * nd)

    def per_b(shape):
        nd = len(shape)
        return pl.BlockSpec((None,) + shape, lambda b, nd=nd: (b,) + (0,) * nd)

    shapes = [((G, N_QBLK, 2 * NSA_DH, W4), BF16),
              ((SEL_PAD + SEQ, LANES), BF16), ((WINDOW + SEQ, LANES), BF16),
              ((G, N_SEG, LANES), BF16), ((G, NSA_DH, N_SEG), BF16),
              ((G, (SEL_PAD + SEQ) // Q_BLOCK, NSA_DH, Q_BLOCK), BF16),
              ((G, (WINDOW + SEQ) // Q_BLOCK, NSA_DH, Q_BLOCK), BF16)]
    return pl.pallas_call(
        _nsa_prep_kernel,
        grid=(bsz,),
        in_specs=[per_b((N_TROWS, SEQ)),
                  pl.BlockSpec((None, SEQ, LANES), lambda b: (b, 0, CB_CKS)),
                  pl.BlockSpec((None, SEQ, LANES), lambda b: (b, 0, CB_CKW)),
                  per_b((N_SEG, CMP_STRIDE * LANES)), per_b((N_SEG, CMP_STRIDE * LANES))]
                 + [full(a) for a in consts],
        out_specs=[per_b(s) for s, _ in shapes],
        out_shape=[jax.ShapeDtypeStruct((bsz,) + s, dt) for s, dt in shapes],
        compiler_params=_params("parallel"),
        name="nsa_prep",
    )(yt, y3, y3, k2, v2, *consts)


def _nsa_kernel(q_ref, ck_ref, cv_ref, ks_ref, vs_ref, kw_ref, vw_ref, bct_ref, bt_ref, bw_ref, b31_ref,
                cg_ref, o_ref, sel_s, far_s, bc_s):
    c = pl.program_id(2)
    W4 = NSA_HPG * Q_BLOCK
    q = q_ref[...]
    b31 = b31_ref[...]

    def block_mask(rows):
        m1 = jnp.concatenate([jnp.broadcast_to(r, (SEL_BLOCK, Q_BLOCK)) for r in rows], axis=0)
        return jnp.concatenate([m1] * NSA_HPG, axis=1) > 0.5

    def v_tiles(ref, first, n):
        return jnp.concatenate([ref[first + j] for j in range(n)], axis=1)

    r0 = pl.multiple_of(c * Q_BLOCK, Q_BLOCK)
    n_w = WINDOW + Q_BLOCK
    qk_cmp = _mm(ck_ref[...], q)
    qk_near = _mm(ks_ref[pl.ds(r0, 2 * Q_BLOCK), :], q)
    qk_win = _mm(kw_ref[pl.ds(r0, n_w), :], q)

    nrow = lax.broadcasted_iota(jnp.int32, (N_SEG + 16, W4), 0) - 16
    near0 = (Q_BLOCK // CMP_STRIDE) * c - 16
    bc_s[...] = jnp.where(nrow < near0, b31, NEG)
    bc_s[pl.ds(pl.multiple_of(near0 + 16, 8), CMP_NEAR), :] = bct_ref[...]
    s = qk_cmp + bc_s[16:, :]
    e = jnp.where(s > 0.1 * NEG, jnp.exp(s - jnp.max(s, axis=0, keepdims=True)), 0.0)
    p = e / jnp.maximum(jnp.sum(e, axis=0, keepdims=True), 1e-30)
    o_cmp = _mm(cv_ref[...], _bf(p))

    psum = (p[:, 0:Q_BLOCK] + p[:, Q_BLOCK:2 * Q_BLOCK] + p[:, 2 * Q_BLOCK:3 * Q_BLOCK]
            + p[:, 3 * Q_BLOCK:4 * Q_BLOCK])
    jj = lax.broadcasted_iota(jnp.int32, (N_SEL, N_SEG), 0)
    nn = lax.broadcasted_iota(jnp.int32, (N_SEL, N_SEG), 1)
    ratio = SEL_BLOCK // CMP_STRIDE
    ov = jnp.where((nn >= ratio * jj - 1) & (nn <= ratio * jj + ratio - 1) & (nn < N_SEG - 1), 1.0, 0.0)
    imp = _mm(ov, psum, HP)

    wrow = lax.broadcasted_iota(jnp.int32, (n_w, W4), 0)
    s = jnp.where(wrow >= WINDOW - c * Q_BLOCK, qk_win + bw_ref[...], NEG)
    pw = jnp.exp(s - jnp.max(s, axis=0, keepdims=True))
    o_win = _mm(v_tiles(vw_ref, c, n_w // Q_BLOCK), _bf(pw)) / jnp.sum(pw, axis=0, keepdims=True)

    jb =lax.broadcasted_iota(jnp.int32, (N_SEL, Q_BLOCK), 0)
    tq = c * Q_BLOCK + lax.broadcasted_iota(jnp.int32, (N_SEL, Q_BLOCK), 1)
    cur = tq >> 6
    imp = jnp.where((jb == 0) | (jb == cur), FORCE_SCORE, jnp.where(jb <= cur, imp, -1.0))
    rank = jnp.zeros((N_SEL, Q_BLOCK), F32)
    for jp in range(N_SEL):
        rowv = imp[jp:jp + 1, :]
        rank = rank + jnp.where(rowv > imp, 1.0, 0.0)
        if jp < N_SEL - 1:
            rank = rank + jnp.where(jb > jp, jnp.where(rowv == imp, 1.0, 0.0), 0.0)
    sel = jnp.where(rank < SEL_TOPN, 1.0, 0.0)
    sel_s[0:8, :] = jnp.zeros((8, Q_BLOCK), F32)
    sel_s[8:, :] = sel
    far_s[...] = jnp.where(jb < 2 * (c - 1), sel, 0.0)

    near = block_mask([sel_s[pl.ds(6 + 2 * c + j, 1), :] for j in range(4)])
    s = jnp.where(near, qk_near + bt_ref[...], NEG)
    m_run = jnp.max(s, axis=0, keepdims=True)
    pe = jnp.exp(s - m_run)
    l_run = jnp.sum(pe, axis=0, keepdims=True)
    acc = _mm(v_tiles(vs_ref, c, 2), _bf(pe))

    def far_step(st, carry):
        m_old, l_old, acc_old = carry
        k0 = pl.multiple_of(SEL_PAD + st * FAR_TILE, Q_BLOCK)
        b0 = pl.multiple_of(st * (FAR_TILE // SEL_BLOCK), 8)
        rows = far_s[pl.ds(b0, FAR_TILE // SEL_BLOCK), :]
        msk = block_mask([rows[j:j + 1, :] for j in range(FAR_TILE // SEL_BLOCK)])
        sc = jnp.where(msk, _mm(ks_ref[pl.ds(k0, FAR_TILE), :], q) + b31, NEG)
        m_new = jnp.maximum(m_old, jnp.max(sc, axis=0, keepdims=True))
        alpha = jnp.exp(m_old - m_new)
        pf = jnp.exp(sc - m_new)
        l_new = alpha * l_old + jnp.sum(pf, axis=0, keepdims=True)
        vt = v_tiles(vs_ref, 1 + st * (FAR_TILE // Q_BLOCK), FAR_TILE // Q_BLOCK)
        return m_new, l_new, alpha * acc_old + _mm(vt, _bf(pf))

    n_far = (c + 2) // (FAR_TILE // Q_BLOCK)
    m_run, l_run, acc = lax.fori_loop(0, n_far, far_step, (m_run, l_run, acc))
    o_sel = acc / l_run

    gates = _sigmoid(cg_ref[...])

    def gate_row(r):
        return jnp.concatenate([gates[k * 3 + r:k * 3 + r + 1, :] for k in range(NSA_HPG)], axis=1)

    res = gate_row(0) * o_cmp + gate_row(1) * o_sel + gate_row(2) * o_win
    o_ref[...] = jnp.concatenate([res[:, k * Q_BLOCK:(k + 1) * Q_BLOCK].T for k in range(NSA_HPG)], axis=1)


def _nsa_kernel2(q_ref, ck_ref, cv_ref, ks_ref, vs_ref, kw_ref, vw_ref, bct_ref, bt_ref, bw_ref, b31_ref,
                 cg_ref, o_ref, sel_s, far_s, bc_s):
    c = pl.program_id(1)
    G = NSA_GROUPS
    W4 = NSA_HPG * Q_BLOCK
    r0 = pl.multiple_of(c * Q_BLOCK, Q_BLOCK)
    n_w = WINDOW + Q_BLOCK
    far_blocks = FAR_TILE // SEL_BLOCK
    far_tiles = FAR_TILE // Q_BLOCK

    def block_mask(rows):
        m1 = jnp.concatenate([jnp.broadcast_to(r, (SEL_BLOCK, Q_BLOCK)) for r in rows], axis=0)
        return jnp.concatenate([m1] * NSA_HPG, axis=1) > 0.5

    def v_tiles(ref, g, first, n):
        return jnp.concatenate([ref[g, first + j] for j in range(n)], axis=1)

    jb = lax.broadcasted_iota(jnp.int32, (N_SEL, Q_BLOCK), 0)
    tq = c * Q_BLOCK + lax.broadcasted_iota(jnp.int32, (N_SEL, Q_BLOCK), 1)
    cur = tq >> 6
    jj = lax.broadcasted_iota(jnp.int32, (N_SEL, N_SEG), 0)
    nn = lax.broadcasted_iota(jnp.int32, (N_SEL, N_SEG), 1)
    ratio = SEL_BLOCK // CMP_STRIDE
    ov = jnp.where((nn >= ratio * jj - 1) & (nn <= ratio * jj + ratio - 1) & (nn < N_SEG - 1), 1.0, 0.0)
    nrow = lax.broadcasted_iota(jnp.int32, (N_SEG + 16, W4), 0) - 16
    near0 = (Q_BLOCK // CMP_STRIDE) * c - 16
    wrow_ok = lax.broadcasted_iota(jnp.int32, (n_w, W4), 0) >= WINDOW - c * Q_BLOCK

    def front(g):
        q = q_ref[g]
        b31 = b31_ref[g]
        qk_cmp = _mm(ck_ref[g], q)
        qk_near = _mm(ks_ref[pl.ds(r0, 2 * Q_BLOCK), :], q)
        qk_win = _mm(kw_ref[pl.ds(r0, n_w), :], q)

        bc_s[g] = jnp.where(nrow < near0, b31, NEG)
        bc_s[g, pl.ds(pl.multiple_of(near0 + 16, 8), CMP_NEAR), :] = bct_ref[g]
        s = qk_cmp + bc_s[g, 16:, :]
        e = jnp.where(s > 0.1 * NEG, jnp.exp2(s - jnp.max(s, axis=0, keepdims=True)), 0.0)
        p = e / jnp.maximum(jnp.sum(e, axis=0, keepdims=True), 1e-30)
        o_cmp = _mm(cv_ref[g], _bf(p))

        psum = (p[:, 0:Q_BLOCK] + p[:, Q_BLOCK:2 * Q_BLOCK] + p[:, 2 * Q_BLOCK:3 * Q_BLOCK]
                + p[:, 3 * Q_BLOCK:4 * Q_BLOCK])
        imp = _mm(ov, psum, HP)

        s = jnp.where(wrow_ok, qk_win + bw_ref[g], NEG)
        pw = jnp.exp2(s - jnp.max(s, axis=0, keepdims=True))
        o_win = _mm(v_tiles(vw_ref, g, c, n_w // Q_BLOCK), _bf(pw)) / jnp.sum(pw, axis=0, keepdims=True)

        imp = jnp.where((jb == 0) | (jb == cur), FORCE_SCORE, jnp.where(jb <= cur, imp, -1.0))
        rank = jnp.zeros((N_SEL, Q_BLOCK), F32)
        for jp in range(N_SEL):
            rowv = imp[jp:jp + 1, :]
            rank = rank + jnp.where(rowv > imp, 1.0, 0.0)
            if jp < N_SEL - 1:
                rank = rank + jnp.where(jb > jp, jnp.where(rowv == imp, 1.0, 0.0), 0.0)
        sel = jnp.where(rank < SEL_TOPN, 1.0, 0.0)
        sel_s[g, 0:8, :] = jnp.zeros((8, Q_BLOCK), F32)
        sel_s[g, 8:, :] = sel
        far_sel = jnp.concatenate([jnp.where(jb < 2 * (c - 1), sel, 0.0)] * NSA_HPG, axis=1)
        far_s[g] = jnp.where(far_sel > 0.5, b31, NEG)

        near = block_mask([sel_s[g, pl.ds(6 + 2 * c + j, 1), :] for j in range(4)])
        s = jnp.where(near, qk_near + bt_ref[g], NEG)
        m_run = jnp.max(s, axis=0, keepdims=True)
        pe = jnp.exp2(s - m_run)
        l_run = jnp.sum(pe, axis=0, keepdims=True)
        acc = _mm(v_tiles(vs_ref, g, c, 2), _bf(pe))
        return (q, b31, o_cmp, o_win), (m_run, l_run, acc)

    fronts = [front(g) for g in range(G)]

    def far_step(st, carry):
        k0 = pl.multiple_of(SEL_PAD + st * FAR_TILE, Q_BLOCK)
        b0 = pl.multiple_of(st * far_blocks, 8)
        kf = ks_ref[pl.ds(k0, FAR_TILE), :]
        out = []
        for g in range(G):
            m_old, l_old, acc_old = carry[g]
            q, b31 = fronts[g][0][:2]
            rows = far_s[g, pl.ds(b0, far_blocks), :]
            add = jnp.concatenate([jnp.broadcast_to(rows[j:j + 1, :], (SEL_BLOCK, W4))
                                   for j in range(far_blocks)], axis=0)
            sc = _mm(kf, q) + add
            m_new = jnp.maximum(m_old, jnp.max(sc, axis=0, keepdims=True))
            alpha = jnp.exp2(m_old - m_new)
            pf = jnp.exp2(sc - m_new)
            l_new = alpha * l_old + jnp.sum(pf, axis=0, keepdims=True)
            vt = v_tiles(vs_ref, g, 1 + st * far_tiles, far_tiles)
            out.append((m_new, l_new, alpha * acc_old + _mm(vt, _bf(pf))))
        return tuple(out)

    n_far = (c + 2) // far_tiles
    finals = lax.fori_loop(0, n_far, far_step, tuple(f[1] for f in fronts))

    gates = _sigmoid(cg_ref[...])
    outs = []
    for g in range(G):
        _, _, o_cmp, o_win = fronts[g][0]
        _, l_run, acc = finals[g]

        def gate_row(r):
            rows = [gates[g * CG_ROWS + k * 3 + r:g * CG_ROWS + k * 3 + r + 1, :] for k in range(NSA_HPG)]
            return jnp.concatenate(rows, axis=1)

        res = gate_row(0) * o_cmp + gate_row(1) * (acc / l_run) + gate_row(2) * o_win
        outs += [res[:, k * Q_BLOCK:(k + 1) * Q_BLOCK].T for k in range(NSA_HPG)]
    o_ref[...] = jnp.concatenate(outs, axis=1)


def _nsa_attend(qn, ck, cv, ksn, vst, kwn, vwt, tables, yt):
    bsz = qn.shape[0]
    G = NSA_GROUPS
    W4 = NSA_HPG * Q_BLOCK
    bct, bt, bw, b31 = tables

    def per_b(a):
        shape = a.shape[1:]
        return pl.BlockSpec((None,) + shape, lambda b, c, n=len(shape): (b,) + (0,) * n)

    def full(a):
        return pl.BlockSpec(a.shape, lambda b, c, n=a.ndim: (0,) * n)

    return pl.pallas_call(
        _nsa_kernel2,
        grid=(bsz, N_QBLK),
        in_specs=[
            pl.BlockSpec((None, G, None, 2 * NSA_DH, W4), lambda b, c: (b, 0, c, 0, 0)),
            per_b(ck), per_b(cv), per_b(ksn), per_b(vst), per_b(kwn), per_b(vwt),
            full(bct), full(bt), full(bw), full(b31),
            pl.BlockSpec((None, G * CG_ROWS, Q_BLOCK), lambda b, c: (b, TR_CG // (G * CG_ROWS), c)),
        ],
        out_specs=pl.BlockSpec((None, Q_BLOCK, NSA_HEADS * NSA_DH), lambda b, c: (b, c, 0)),
        out_shape=jax.ShapeDtypeStruct((bsz, SEQ, NSA_HEADS * NSA_DH), F32),
        scratch_shapes=[pltpu.VMEM((G, 8 + N_SEL, Q_BLOCK), F32), pltpu.VMEM((G, N_SEL, W4), F32),
                        pltpu.VMEM((G, 16 + N_SEG, W4), F32)],
        compiler_params=_params("parallel", "arbitrary"),
        name="nsa_attend",
    )(qn, ck, cv, ksn, vst, kwn, vwt, bct, bt, bw, b31, yt)


def _bucket_starts():
    max_exact = REL_BUCKETS // 2
    d = np.arange(4 * REL_MAX_DIST)
    large = max_exact + (np.log(np.maximum(d, 1) / max_exact) / math.log(REL_MAX_DIST / max_exact)
                         * (REL_BUCKETS - max_exact)).astype(np.int64)
    bucket = np.where(d < max_exact, d, np.minimum(large, REL_BUCKETS - 1))
    assert np.all(np.diff(bucket) >= 0) and bucket[-1] == REL_BUCKETS - 1
    return [int(np.argmax(bucket >= j)) for j in range(REL_BUCKETS)]


BUCKET_STARTS = _bucket_starts()
TABLE_ROWS = 128


def _bias_kernel(rb_ref, bct_ref, bt_ref, bw_ref):
    W4 = NSA_HPG * Q_BLOCK
    rb = rb_ref[...] * LOG2E

    def table(rows, r0, dist_fn, limit):
        r = r0 + lax.broadcasted_iota(jnp.int32, (rows, W4), 0)
        i = lax.broadcasted_iota(jnp.int32, (rows, W4), 1) & (Q_BLOCK - 1)
        dist = dist_fn(r, i)
        acc = jnp.broadcast_to(rb[0:1, :], (rows, W4))
        for j in range(1, REL_BUCKETS):
            acc = jnp.where(dist >= BUCKET_STARTS[j], rb[j:j + 1, :], acc)
        return jnp.where((dist < 0) | (dist >= limit), NEG, acc)

    big = 1 << 30
    bct_ref[...] = table(CMP_NEAR, 0, lambda r, i: i - CMP_STRIDE * r + (16 * CMP_STRIDE - CMP_BLOCK + 1), big)

    def near_rows(k, carry):
        r0 = pl.multiple_of(k * TABLE_ROWS, TABLE_ROWS)
        bt_ref[pl.ds(r0, TABLE_ROWS), :] = table(TABLE_ROWS, r0, lambda r, i: Q_BLOCK + i - r, big)
        return carry

    lax.fori_loop(0, 2 * Q_BLOCK // TABLE_ROWS, near_rows, 0)

    def win_rows(k, carry):
        r0 = pl.multiple_of(k * TABLE_ROWS, TABLE_ROWS)
        bw_ref[pl.ds(r0, TABLE_ROWS), :] = table(TABLE_ROWS, r0, lambda r, i: WINDOW + i - r, WINDOW)
        return carry

    lax.fori_loop(0, (WINDOW + Q_BLOCK) // TABLE_ROWS, win_rows, 0)


def _bias_tables(rel_bias):
    G, K = NSA_GROUPS, NSA_HPG
    W4 = K * Q_BLOCK
    rb = jnp.repeat(rel_bias.reshape(REL_BUCKETS, G, K).transpose(1, 0, 2), Q_BLOCK, axis=2)
    shapes = [(G, CMP_NEAR, W4), (G, 2 * Q_BLOCK, W4), (G, WINDOW + Q_BLOCK, W4)]
    bct, bt, bw = pl.pallas_call(
        _bias_kernel,
        grid=(G,),
        in_specs=[pl.BlockSpec((None, REL_BUCKETS, W4), lambda g: (g, 0, 0))],
        out_specs=[pl.BlockSpec((None,) + s[1:], lambda g: (g, 0, 0)) for s in shapes],
        out_shape=[jax.ShapeDtypeStruct(s, F32) for s in shapes],
        compiler_params=_params("parallel"),
        name="bias_tables",
    )(rb)
    return bct, bt, bw, rb[:, REL_BUCKETS - 1:, :] * LOG2E


def _nsa(y3, yt, q_norm, k_norm, cmp_pe, w_cmp, tables):
    bsz = y3.shape[0]
    G, DH = NSA_GROUPS, NSA_DH

    def segments(cb):
        return y3[:, :, cb * LANES:(cb + 1) * LANES].reshape(bsz, N_SEG, CMP_STRIDE * LANES)

    def both_groups(w):
        return jnp.einsum('lde,gh->lgdhe', w, jnp.eye(G, dtype=w.dtype)).reshape(CMP_STRIDE * LANES, LANES)

    def pe_row(pe):
        return jnp.tile(pe[:, None, :], (1, G, 1)).reshape(1, CMP_STRIDE * LANES)

    wk = w_cmp[0].reshape(2, CMP_STRIDE, DH, DH)
    wv = w_cmp[1].reshape(2, CMP_STRIDE, DH, DH)
    pe = cmp_pe.reshape(2, 2, CMP_STRIDE, DH)
    gain2 = lambda g: jnp.tile(g, G).reshape(1, LANES)
    consts = [_bf(both_groups(wk[0])), _bf(both_groups(wk[1])),
              _bf(both_groups(wv[0]).T), _bf(both_groups(wv[1]).T),
              pe_row(pe[0, 0]), pe_row(pe[0, 1]), pe_row(pe[1, 0]), pe_row(pe[1, 1]),
              q_norm.reshape(DH, 1), gain2(k_norm[0]), gain2(k_norm[1]), gain2(k_norm[2])]
    qn, ksn, kwn, ck, cv, vst, vwt = _nsa_prep(y3, yt, segments(CB_CKC), segments(CB_CVC), consts)
    return _nsa_attend(qn, ck, cv, ksn, vst, kwn, vwt, tables, yt)


def _merge_kernel(h_ref, g_ref, ya_ref, yb_ref, yc_ref, wg_ref, wb_ref, wo_ref, o_ref):
    x = h_ref[...]
    u = _bf(_rms_rows(x, g_ref[...]))
    acc = None
    for n, y_ref in enumerate((ya_ref, yb_ref, yc_ref)):
        t = _sigmoid(_mm(u, wg_ref[:, n * D_MODEL:(n + 1) * D_MODEL])) * _mm(_bf(y_ref[...]), wb_ref[n])
        acc = t if acc is None else acc + t
    o_ref[...] = x + _mm(_bf(acc), wo_ref[...])


def _merge(h2, gain, ya, yb, yc, wgt, wb, wo):
    m, d = h2.shape
    tm = 512
    row = lambda w: pl.BlockSpec((tm, w), lambda i: (i, 0))
    return pl.pallas_call(
        _merge_kernel,
        grid=(m // tm,),
        in_specs=[row(d), pl.BlockSpec((1, d), lambda i: (0, 0)),
                  row(BRANCH_W), row(BRANCH_W), row(BRANCH_W),
                  pl.BlockSpec((d, N_BRANCH * d), lambda i: (0, 0)),
                  pl.BlockSpec((N_BRANCH, BRANCH_W, d), lambda i: (0, 0, 0)),
                  pl.BlockSpec((d, d), lambda i: (0, 0))],
        out_specs=row(d),
        out_shape=jax.ShapeDtypeStruct((m, d), F32),
        compiler_params=_params("parallel"),
        name="merge",
    )(h2, gain, ya, yb, yc, wgt, wb, wo)


def _ffn_kernel(h_ref, g_ref, wg_ref, wu_ref, wo_ref, o_ref, u_ref, acc_ref):
    f = pl.program_id(1)

    @pl.when(f == 0)
    def _():
        x = h_ref[...]
        u_ref[...] = _bf(_rms_rows(x, g_ref[...]))
        acc_ref[...] = x

    u = u_ref[...]
    gt = _mm(u, wg_ref[...])
    up = _mm(u, wu_ref[...])
    acc_ref[...] += _mm(_bf(gt * _sigmoid(gt) * up), wo_ref[...])

    @pl.when(f == pl.num_programs(1) - 1)
    def _():
        o_ref[...] = acc_ref[...]


def _ffn(h2, gain, w_in, w_out):
    m, d = h2.shape
    tm, nf = 512, 2
    tf = D_FF // nf
    return pl.pallas_call(
        _ffn_kernel,
        grid=(m // tm, nf),
        in_specs=[pl.BlockSpec((tm, d), lambda i, f: (i, 0)),
                  pl.BlockSpec((1, d), lambda i, f: (0, 0)),
                  pl.BlockSpec((d, tf), lambda i, f: (0, f)),
                  pl.BlockSpec((d, tf), lambda i, f: (0, f + nf)),
                  pl.BlockSpec((tf, d), lambda i, f: (f, 0))],
        out_specs=pl.BlockSpec((tm, d), lambda i, f: (i, 0)),
        out_shape=jax.ShapeDtypeStruct((m, d), F32),
        scratch_shapes=[pltpu.VMEM((tm, d), BF16), pltpu.VMEM((tm, d), F32)],
        compiler_params=_params("parallel", "arbitrary"),
        name="ffn",
    )(h2, gain, w_in, w_in, w_out)


def _ple_kernel(h_ref, g_ref, p_ref, wg_ref, wp_ref, o_ref):
    x = h_ref[...]
    gate = _sigmoid(_mm(_bf(_rms_rows(x, g_ref[...])), wg_ref[...]))
    o_ref[...] = x + gate * _mm(_bf(p_ref[...]), wp_ref[...])


def _ple(h2, gain, p2, wg, wp):
    m, d = h2.shape
    tm = 512
    return pl.pallas_call(
        _ple_kernel,
        grid=(m // tm,),
        in_specs=[pl.BlockSpec((tm, d), lambda i: (i, 0)),
                  pl.BlockSpec((1, d), lambda i: (0, 0)),
                  pl.BlockSpec((tm, PLE_DIM), lambda i: (i, 0)),
                  pl.BlockSpec((d, d), lambda i: (0, 0)),
                  pl.BlockSpec((PLE_DIM, d), lambda i: (0, 0))],
        out_specs=pl.BlockSpec((tm, d), lambda i: (i, 0)),
        out_shape=jax.ShapeDtypeStruct((m, d), F32),
        compiler_params=_params("parallel"),
        name="ple",
    )(h2, gain, p2, wg, wp)


W_IN_GROUPS = ((0, 0, 16), (16, 8, 16), (32, 16, 10))
W_IN_GATES = (42, 40, 24)
W_IN_MIX_BLOCKS = 44
W_IN_SCALAR_BLOCKS = (16, 32, 42)
W_IN_SCALAR_LANES = (8, 16, 40)


def _regroup_kernel(src_ref, off_ref, a_ref, b_ref, s0_ref, s1_ref, s2_ref, o_ref, *, offsets, n_main):
    j = pl.program_id(0)
    x = jnp.concatenate([a_ref[...], b_ref[...]], axis=1)
    for off in offsets:
        @pl.when((off_ref[j] == off) & (j < n_main))
        def _(off=off):
            o_ref[...] = _bf(x[:, off:off + LANES])

    @pl.when(j == n_main)
    def _():
        lane = lax.broadcasted_iota(jnp.int32, (D_MODEL, LANES), 1)
        l0, l1, l2 = W_IN_SCALAR_LANES
        sc = jnp.where(lane < l0, s0_ref[...], jnp.where(lane < l1, s1_ref[...], s2_ref[...]))
        o_ref[...] = _bf(jnp.where(lane < l2, sc, 0.0))

    @pl.when(j > n_main)
    def _():
        o_ref[...] = jnp.zeros(o_ref.shape, BF16)


def _regroup(w_in, layer, groups, n_out, with_scalars):
    src = np.concatenate([blk + np.arange(n) for blk, _, n in groups])
    off = np.concatenate([np.full(n, o) for _, o, n in groups])
    n_main = len(src)
    pad = n_out - n_main
    src = np.concatenate([src, np.zeros(pad, np.int64)]).astype(np.int32)
    off = np.concatenate([off, np.zeros(pad, np.int64)]).astype(np.int32)
    blk = lambda f: pl.BlockSpec((None, D_MODEL, LANES), f)
    fixed = [blk(lambda j, s, o, b=b: (layer, 0, b)) for b in W_IN_SCALAR_BLOCKS]
    return pl.pallas_call(
        functools.partial(_regroup_kernel, offsets=tuple(sorted({o for _, o, _ in groups})),
                          n_main=n_main if with_scalars else n_out),
        grid_spec=pltpu.PrefetchScalarGridSpec(
            num_scalar_prefetch=2,
            grid=(n_out,),
            in_specs=[blk(lambda j, s, o: (layer, 0, s[j])), blk(lambda j, s, o: (layer, 0, s[j] + 1))] + fixed,
            out_specs=pl.BlockSpec((D_MODEL, LANES), lambda j, s, o: (0, j))),
        out_shape=jax.ShapeDtypeStruct((D_MODEL, n_out * LANES), BF16),
        compiler_params=_params("arbitrary"),
        name="regroup_w_in",
    )(jnp.asarray(src), jnp.asarray(off), w_in, w_in, w_in, w_in, w_in)


def _split_w_mix(w_mix):
    blk = lambda b, n=1: w_mix[:, b * LANES:(b + n) * LANES]
    main = jnp.concatenate([blk(0, 32), blk(36, 3), blk(40), blk(42, 2)], axis=1)
    main = jnp.pad(main, ((0, 0), (0, N_COLS - main.shape[1])))
    cg = blk(42)[:, SM_CG:SM_CG + NSA_HEADS * 3].reshape(D_MODEL, NSA_GROUPS, NSA_HPG * 3)
    cg = jnp.pad(cg, ((0, 0), (0, 0), (0, CG_ROWS - NSA_HPG * 3))).reshape(D_MODEL, NSA_GROUPS * CG_ROWS)
    return main, jnp.concatenate([blk(32, 4), blk(39), blk(41), cg], axis=1).T


def _lane_row(vals, offset):
    return jnp.zeros((1, LANES), F32).at[0, offset:offset + vals.shape[0]].set(vals)


def kernel(x, p, rel_bias, norm_mix, w_in, conv_w, gdn_a_log, gdn_dt_bias, gdn_norm, mlstm_b_i, mlstm_b_f,
           mlstm_norm, nsa_q_norm, nsa_k_norm, nsa_cmp_pe, nsa_w_cmp, w_branch, w_out, norm_ffn, w_ffn_in,
           w_ffn_out, norm_ple, w_ple_gate, w_ple_proj):
    bsz, seq, d = x.shape
    assert seq == SEQ and d == D_MODEL
    depth = w_in.shape[0]
    m = bsz * seq
    tables = _bias_tables(rel_bias)
    h2 = x.reshape(m, d)
    for l in range(depth):
        w_main, w_tr = _split_w_mix(_regroup(w_in, l, W_IN_GROUPS, W_IN_MIX_BLOCKS, True))
        w_gate = _regroup(w_in, l, (W_IN_GATES,), W_IN_GATES[2], False)
        gain = norm_mix[l].reshape(1, d)
        y2, yt = _in_proj(h2, gain, w_main, w_tr, bsz)
        y3 = y2.reshape(bsz, seq, N_COLS)
        ya = _gdn(y3, conv_w[l], _lane_row(gdn_a_log[l], SM_AA), _lane_row(gdn_dt_bias[l], SM_AA),
                  gdn_norm[l].reshape(1, D_HEAD_REC))
        yb = _mlstm(y3, _lane_row(mlstm_b_i[l], SM_BI), _lane_row(mlstm_b_f[l], SM_BF),
                    mlstm_norm[l].reshape(1, D_HEAD_REC))
        yc = _nsa(y3, yt, nsa_q_norm[l], nsa_k_norm[l], nsa_cmp_pe[l], nsa_w_cmp[l], tables)
        h2 = _merge(h2, gain, ya.reshape(m, BRANCH_W), yb.reshape(m, BRANCH_W),
                    yc.reshape(m, BRANCH_W), w_gate, _bf(w_branch[l]), _bf(w_out[l]))
        h2 = _ffn(h2, norm_ffn[l].reshape(1, d), _bf(w_ffn_in[l]), _bf(w_ffn_out[l]))
        h2 = _ple(h2, norm_ple[l].reshape(1, d), p[l].reshape(m, PLE_DIM), _bf(w_ple_gate[l]),
                  _bf(w_ple_proj[l]))
    return h2.reshape(bsz, seq, d)
```

```python
import functools
import math

import numpy as np
import jax
import jax.numpy as jnp
from jax import lax
from jax.experimental import pallas as pl
from jax.experimental.pallas import tpu as pltpu

D_MODEL = 1024
SEQ = 2048
N_HEADS_REC = 4
D_HEAD_REC = 128
CHUNK = 64
GDN_CONV = 4
NSA_HEADS = 8
NSA_GROUPS = 2
NSA_HPG = NSA_HEADS // NSA_GROUPS
NSA_DH = 64
CMP_BLOCK = 32
CMP_STRIDE = 16
SEL_BLOCK = 64
SEL_TOPN = 4
WINDOW = 512
Q_BLOCK = 128
N_QBLK = SEQ // Q_BLOCK
N_SEL = SEQ // SEL_BLOCK
N_SEG = SEQ // CMP_STRIDE
SEL_PAD = Q_BLOCK
FAR_TILE = 4 * Q_BLOCK
CMP_NEAR = 24
REL_BUCKETS = 32
REL_MAX_DIST = 128
N_BRANCH = 3
BRANCH_W = 512
D_FF = 2816
PLE_DIM = 256
EPS = 1e-6
NEG = -1e30
FORCE_SCORE = 1e4
LOG2E = math.log2(math.e)

LANES = 128
VMEM_LIMIT = 48 * 1024 * 1024

F32 = jnp.float32
BF16 = jnp.bfloat16
HP = lax.Precision.HIGHEST

CB_A = 0
CB_B = 16
CB_CKC, CB_CVC, CB_CKS, CB_CKW = 32, 33, 34, 35
CB_SMALL = 36
N_COLS = 40 * LANES
TR_Q = 0
TR_VS = NSA_HEADS * NSA_DH
TR_VW = TR_VS + NSA_GROUPS * NSA_DH
TR_CG = TR_VW + NSA_GROUPS * NSA_DH
CG_ROWS = 16
N_TROWS = TR_CG + NSA_GROUPS * CG_ROWS
SM_AA, SM_AB, SM_BI, SM_BF, SM_CG = 0, 4, 8, 12, 16


def _mm(a, b, precision=None):
    return lax.dot_general(a, b, (((1,), (0,)), ((), ())), precision=precision,
                           preferred_element_type=F32)


def _mm_nt(a, b, precision=None):
    return lax.dot_general(a, b, (((1,), (1,)), ((), ())), precision=precision,
                           preferred_element_type=F32)


def _mm_tn(a, b, precision=None):
    return lax.dot_general(a, b, (((0,), (0,)), ((), ())), precision=precision,
                           preferred_element_type=F32)


def _bf(x):
    return x.astype(BF16)


def _sigmoid(x):
    return 1.0 / (1.0 + jnp.exp(-x))


def _softplus(x):
    return jnp.maximum(x, 0.0) + jnp.log1p(jnp.exp(-jnp.abs(x)))


def _rms_rows(x, g):
    return x * lax.rsqrt(jnp.mean(x * x, axis=-1, keepdims=True) + EPS) * g


def _params(*sem):
    return pltpu.CompilerParams(dimension_semantics=sem, vmem_limit_bytes=VMEM_LIMIT)


def _proj_kernel(x_ref, g_ref, w_ref, wt_ref, o_ref, ot_ref):
    u = _bf(_rms_rows(x_ref[...], g_ref[...]))
    o_ref[...] = _mm(u, w_ref[...])
    ot_ref[...] = _mm_nt(wt_ref[...], u)


def _in_proj(x2, gain, w, wt, bsz):
    m, d = x2.shape
    n, nt = w.shape[1], wt.shape[0]
    tm = 256
    per_seq = SEQ // tm
    return pl.pallas_call(
        _proj_kernel,
        grid=(m // tm,),
        in_specs=[pl.BlockSpec((tm, d), lambda i: (i, 0)),
                  pl.BlockSpec((1, d), lambda i: (0, 0)),
                  pl.BlockSpec((d, n), lambda i: (0, 0)),
                  pl.BlockSpec((nt, d), lambda i: (0, 0))],
        out_specs=[pl.BlockSpec((tm, n), lambda i: (i, 0)),
                   pl.BlockSpec((None, nt, tm), lambda i: (i // per_seq, 0, i % per_seq))],
        out_shape=[jax.ShapeDtypeStruct((m, n), F32), jax.ShapeDtypeStruct((bsz, nt, SEQ), F32)],
        compiler_params=_params("parallel"),
        name="in_proj",
    )(x2, gain, w, wt)


def _chunk_cumsum(x):
    rowi = lax.broadcasted_iota(jnp.int32, x.shape, 0)
    s = 1
    while s < CHUNK:
        x = x + jnp.where(rowi >= s, pltpu.roll(x, s, axis=0), 0.0)
        s *= 2
    return x


def _pick_lane(x, lane_idx):
    lane = lax.broadcasted_iota(jnp.int32, x.shape, 1)
    return jnp.sum(jnp.where(lane == lane_idx, x, 0.0), axis=1, keepdims=True)


def _bd_groups(pk):
    grp = lax.broadcasted_iota(jnp.int32, pk.shape, 1) // CHUNK
    zero = jnp.zeros_like(pk)
    return jnp.concatenate([jnp.where(grp == j, pk, zero) for j in range(pk.shape[1] // CHUNK)], axis=0)


def _bd_wide(x):
    zero = jnp.zeros((x.shape[0], LANES), x.dtype)
    return jnp.concatenate([jnp.concatenate([x[:, :LANES], zero], axis=1),
                            jnp.concatenate([zero, x[:, LANES:]], axis=1)], axis=0)


def _inv_unit_lower(x, bd, eyef):
    xd = jnp.where(bd, x, 0.0)
    xo = jnp.where(bd, 0.0, x)
    mm = lambda a, b: _mm(_bf(a), _bd_groups(_bf(b)))

    def mm2(a1, a2, b):
        r = mm(jnp.concatenate([a1, a2], axis=0), b)
        return r[:CHUNK], r[CHUNK:]

    x2 = mm(xd, xd)
    p = eyef - xd
    x4, t = mm2(x2, p, x2)
    p = p + t
    x8, t = mm2(x4, p, x4)
    p = p + t
    p = p + mm(p, x8)
    m = mm(p, xo)
    m2 = mm(m, m)
    q = eyef - m
    q = q + mm(q, m2)
    return mm(q, p)


def _gdn_kernel(q_ref, k_ref, v_ref, z_ref, sm_ref, cwq_ref, cwk_ref, cwv_ref, alog_ref, dtb_ref,
                nw_ref, o_ref, qg_s, kd_s, u_s, w_s, at_s, eg_s):
    L = CHUNK
    W = 2 * LANES
    ri = lax.broadcasted_iota(jnp.int32, (L, 4 * L), 0)
    cp = lax.broadcasted_iota(jnp.int32, (L, 4 * L), 1)
    ci = cp & (L - 1)
    first = lax.broadcasted_iota(jnp.int32, (L, 2 * L), 1) < L
    tril = ri >= ci
    strict = ri > ci
    eye = ri == ci
    bd = (ri >> 4) == (ci >> 4)
    eyef = jnp.where(eye, 1.0, 0.0)
    alog = alog_ref[...]
    dtb = dtb_ref[...]
    nw = nw_ref[...]
    cwq, cwk, cwv = cwq_ref[...], cwk_ref[...], cwv_ref[...]
    ha = pl.program_id(1) * 2

    def conv_silu(ref, w, n, r0):
        cur = ref[pl.ds(r0, L), :]
        p0 = pl.multiple_of(jnp.maximum(r0 - 8, 0), 8)
        prev = jnp.where(n > 0, ref[pl.ds(p0, 8), :], 0.0)
        win = jnp.concatenate([prev, cur], axis=0)
        acc = cur * w[GDN_CONV - 1:GDN_CONV, :]
        for s in range(1, GDN_CONV):
            acc = acc + pltpu.roll(win, s, axis=0)[8:, :] * w[GDN_CONV - 1 - s:GDN_CONV - s, :]
        return acc * _sigmoid(acc)

    def wide(col_a, col_b, rows=L):
        return jnp.concatenate([jnp.broadcast_to(col_a, (rows, LANES)),
                                jnp.broadcast_to(col_b, (rows, LANES))], axis=1)

    def l2n(x):
        xx = x * x
        return x * wide(lax.rsqrt(jnp.sum(xx[:, :LANES], axis=-1, keepdims=True) + 1e-6),
                        lax.rsqrt(jnp.sum(xx[:, LANES:], axis=-1, keepdims=True) + 1e-6))

    def chunk_terms(n):
        r0 = pl.multiple_of(n * L, L)
        x = sm_ref[pl.ds(r0, L), :]
        g = _chunk_cumsum(-jnp.exp(alog) * _softplus(x + dtb))
        sx = _sigmoid(x)
        q = l2n(conv_silu(q_ref, cwq, n, r0)) * (D_HEAD_REC ** -0.5)
        k = l2n(conv_silu(k_ref, cwk, n, r0))
        v = conv_silu(v_ref, cwv, n, r0)
        gc_a, gc_b = _pick_lane(g, SM_AA + ha), _pick_lane(g, SM_AA + ha + 1)
        beta = wide(_pick_lane(sx, SM_AB + ha), _pick_lane(sx, SM_AB + ha + 1))
        kb = k * beta
        egc = wide(jnp.exp(gc_a), jnp.exp(gc_b))
        gl_a, gl_b = gc_a[L - 1:L, :], gc_b[L - 1:L, :]
        qg_s[pl.ds(r0, L), :] = _bf(q * egc)
        kd_s[pl.ds(r0, L), :] = _bf(k * wide(jnp.exp(gl_a - gc_a), jnp.exp(gl_b - gc_b)))
        eg_s[pl.ds(pl.multiple_of(n * 8, 8), 8), :] = wide(jnp.exp(gl_a), jnp.exp(gl_b), 8)
        gc_pk = jnp.where(first, gc_a, gc_b)
        return q, _bf(k), kb, _bd_wide(_bf(v * beta)), _bd_wide(_bf(kb * egc)), gc_pk

    zk = jnp.zeros((2 * L, W), BF16)

    def prep(i, carry):
        n0 = 2 * i
        r0 = pl.multiple_of(n0 * L, 2 * L)
        q0, k0, kb0, vb0, ke0, gc0 = chunk_terms(n0)
        q1, k1, kb1, vb1, ke1, gc1 = chunk_terms(n0 + 1)
        gc = jnp.concatenate([gc0, gc1], axis=1)
        diff = gc - jnp.sum(jnp.where(eye, gc, 0.0), axis=0, keepdims=True)
        decay = jnp.where(tril, jnp.exp(jnp.where(tril, diff, 0.0)), 0.0)
        lhs = jnp.concatenate([jnp.concatenate([kb0, kb1], axis=1),
                               jnp.concatenate([q0, q1], axis=1)], axis=0)
        rhs = jnp.concatenate([jnp.concatenate([_bd_wide(k0), zk], axis=1),
                               jnp.concatenate([zk, _bd_wide(k1)], axis=1)], axis=0)
        kq = _mm_nt(_bf(lhs), rhs)
        xm = jnp.where(strict, kq[:L] * decay, 0.0)
        t = _bf(_inv_unit_lower(xm, bd, eyef))
        attn = _bf(kq[L:] * decay)
        uw = _mm(t, jnp.concatenate([jnp.concatenate([vb0, zk, ke0, zk], axis=1),
                                     jnp.concatenate([zk, vb1, zk, ke1], axis=1)], axis=0))
        for j in range(2):
            rows = pl.ds(r0 + j * L, L)
            u_s[rows, :] = uw[:, j * W:(j + 1) * W]
            w_s[rows, :] = _bf(uw[:, (2 + j) * W:(3 + j) * W])
            at_s[rows, :] = attn[:, j * 2 * L:(j + 1) * 2 * L]
        return carry

    zs = jnp.zeros((D_HEAD_REC, D_HEAD_REC), BF16)

    def step(n, states):
        sa, sb = states
        r0 = pl.multiple_of(n * L, L)
        sbd = jnp.concatenate([jnp.concatenate([_bf(sa), zs], axis=1),
                               jnp.concatenate([zs, _bf(sb)], axis=1)], axis=0)
        eg = eg_s[pl.ds(pl.multiple_of(n * 8, 8), 1), :]
        r = _mm(jnp.concatenate([w_s[pl.ds(r0, L), :], qg_s[pl.ds(r0, L), :]], axis=0), sbd)
        vnb = _bf(u_s[pl.ds(r0, L), :] - r[:L])
        o = r[L:] + _mm(at_s[pl.ds(r0, L), :], _bd_wide(vnb))
        upd = _mm_tn(kd_s[pl.ds(r0, L), :], vnb)
        sa = sa * eg[:, :LANES] + upd[:LANES, :LANES]
        sb = sb * eg[:, LANES:] + upd[LANES:, LANES:]
        z = z_ref[pl.ds(r0, L), :]
        on = jnp.concatenate([_rms_rows(o[:, :LANES], nw), _rms_rows(o[:, LANES:], nw)], axis=1)
        o_ref[pl.ds(r0, L), :] = on * (z * _sigmoid(z))
        return sa, sb

    n_pairs = SEQ // (2 * L)

    def pair(i, states):
        states = step(2 * i + 1, step(2 * i, states))
        prep(i + 1, 0)
        return states

    zero = jnp.zeros((D_HEAD_REC, D_HEAD_REC), F32)
    prep(0, 0)
    states = lax.fori_loop(0, n_pairs - 1, pair, (zero, zero))
    step(2 * n_pairs - 1, step(2 * n_pairs - 2, states))


def _gdn(y3, conv_w, alog_row, dtb_row, norm_w):
    bsz = y3.shape[0]
    H = N_HEADS_REC
    W = 2 * LANES

    def col(off):
        return pl.BlockSpec((None, SEQ, W), lambda b, h, off=off: (b, 0, off // 2 + h))

    def cw(off):
        return pl.BlockSpec((GDN_CONV, W), lambda b, h, off=off: (0, off // 2 + h))

    row = pl.BlockSpec((1, LANES), lambda b, h: (0, 0))
    return pl.pallas_call(
        _gdn_kernel,
        grid=(bsz, H // 2),
        in_specs=[col(CB_A), col(CB_A + H), col(CB_A + 2 * H), col(CB_A + 3 * H),
                  pl.BlockSpec((None, SEQ, LANES), lambda b, h: (b, 0, CB_SMALL)),
                  cw(0), cw(H), cw(2 * H), row, row, row],
        out_specs=pl.BlockSpec((None, SEQ, W), lambda b, h: (b, 0, h)),
        out_shape=jax.ShapeDtypeStruct((bsz, SEQ, H * D_HEAD_REC), F32),
        scratch_shapes=[pltpu.VMEM((SEQ, W), BF16), pltpu.VMEM((SEQ, W), BF16),
                        pltpu.VMEM((SEQ, W), F32), pltpu.VMEM((SEQ, W), BF16),
                        pltpu.VMEM((SEQ, 2 * CHUNK), BF16),
                        pltpu.VMEM((SEQ // CHUNK * 8, W), F32)],
        compiler_params=_params("parallel", "parallel"),
        name="gdn",
    )(y3, y3, y3, y3, y3, conv_w, conv_w, conv_w, alog_row, dtb_row, norm_w)


def _mlstm_kernel(q_ref, k_ref, v_ref, og_ref, sm_ref, bi_ref, bf_ref, nw_ref, o_ref):
    L = CHUNK
    ri = lax.broadcasted_iota(jnp.int32, (L, 2 * L), 0)
    cp = lax.broadcasted_iota(jnp.int32, (L, 2 * L), 1)
    ci = cp & (L - 1)
    first = cp < L
    tril = ri >= ci
    eye = ri == ci
    bi = bi_ref[...]
    bfr = bf_ref[...]
    nw = nw_ref[...]
    ha = pl.program_id(1) * 2
    zs = jnp.zeros((D_HEAD_REC, D_HEAD_REC), BF16)

    def wide(col_a, col_b, rows=L):
        return jnp.concatenate([jnp.broadcast_to(col_a, (rows, LANES)),
                                jnp.broadcast_to(col_b, (rows, LANES))], axis=1)

    def step(n, carry):
        c_a, c_b, n_st, m_a, m_b = carry
        r0 = pl.multiple_of(n * L, L)
        q = q_ref[pl.ds(r0, L), :]
        k = k_ref[pl.ds(r0, L), :] * (D_HEAD_REC ** -0.5)
        vb = _bf(v_ref[pl.ds(r0, L), :])
        x = sm_ref[pl.ds(r0, L), :]
        lfc = _chunk_cumsum(-_softplus(-(x + bfr)))
        itx = x + bi
        b_a, b_b = _pick_lane(lfc, SM_BF + ha), _pick_lane(lfc, SM_BF + ha + 1)
        it_a, it_b = _pick_lane(itx, SM_BI + ha), _pick_lane(itx, SM_BI + ha + 1)
        b_pk = jnp.where(first, b_a, b_b)
        it_pk = jnp.where(first, it_a, it_b)
        b_row = jnp.sum(jnp.where(eye, b_pk, 0.0), axis=0, keepdims=True)
        it_row = jnp.sum(jnp.where(eye, it_pk, 0.0), axis=0, keepdims=True)
        dm = jnp.where(tril, b_pk - b_row + it_row, NEG)
        dmax_a = jnp.max(jnp.where(first, dm, NEG), axis=1, keepdims=True)
        dmax_b = jnp.max(jnp.where(first, NEG, dm), axis=1, keepdims=True)
        qk = _mm_nt(_bf(q), _bd_wide(_bf(k)))
        a_a, a_b = b_a + m_a, b_b + m_b
        mt_a, mt_b = jnp.maximum(a_a, dmax_a), jnp.maximum(a_b, dmax_b)
        sm = jnp.exp(dm - jnp.where(first, mt_a, mt_b)) * qk
        si_a, si_b = jnp.exp(a_a - mt_a), jnp.exp(a_b - mt_b)
        cbd = jnp.concatenate([jnp.concatenate([_bf(c_a), zs], axis=1),
                               jnp.concatenate([zs, _bf(c_b)], axis=1)], axis=0)
        num = _mm(jnp.concatenate([_bf(q * wide(si_a, si_b)), _bf(sm)], axis=1),
                  jnp.concatenate([cbd, _bd_wide(vb)], axis=0))
        qn = q * n_st
        den_a = (si_a * jnp.sum(qn[:, :LANES], axis=1, keepdims=True)
                 + jnp.sum(jnp.where(first, sm, 0.0), axis=1, keepdims=True))
        den_b = (si_b * jnp.sum(qn[:, LANES:], axis=1, keepdims=True)
                 + jnp.sum(jnp.where(first, 0.0, sm), axis=1, keepdims=True))
        hh = num / wide(jnp.maximum(jnp.abs(den_a), jnp.exp(-mt_a)),
                        jnp.maximum(jnp.abs(den_b), jnp.exp(-mt_b)))
        bl_a, bl_b = b_a[L - 1:L, :], b_b[L - 1:L, :]
        ds_a, ds_b = bl_a - b_a + it_a, bl_b - b_b + it_b
        mn_a = jnp.maximum(bl_a + m_a, jnp.max(ds_a, axis=0, keepdims=True))
        mn_b = jnp.maximum(bl_b + m_b, jnp.max(ds_b, axis=0, keepdims=True))
        wk = k * wide(jnp.exp(ds_a - mn_a), jnp.exp(ds_b - mn_b))
        sc_a, sc_b = jnp.exp(bl_a + m_a - mn_a), jnp.exp(bl_b + m_b - mn_b)
        upd = _mm_tn(_bf(wk), vb)
        c_a = sc_a * c_a + upd[:LANES, :LANES]
        c_b = sc_b * c_b + upd[LANES:, LANES:]
        n_st = wide(sc_a, sc_b, 1) * n_st + jnp.sum(wk, axis=0, keepdims=True)
        on = jnp.concatenate([_rms_rows(hh[:, :LANES], nw), _rms_rows(hh[:, LANES:], nw)], axis=1)
        o_ref[pl.ds(r0, L), :] = on * _sigmoid(og_ref[pl.ds(r0, L), :])
        return c_a, c_b, n_st, mn_a, mn_b

    zc = jnp.zeros((D_HEAD_REC, D_HEAD_REC), F32)
    z1 = jnp.zeros((1, 1), F32)
    lax.fori_loop(0, SEQ // L, step, (zc, zc, jnp.zeros((1, 2 * LANES), F32), z1, z1), unroll=4)


def _mlstm(y3, bi_row, bf_row, norm_w):
    bsz = y3.shape[0]
    H = N_HEADS_REC
    W = 2 * LANES

    def col(off):
        return pl.BlockSpec((None, SEQ, W), lambda b, h, off=off: (b, 0, off // 2 + h))

    row = pl.BlockSpec((1, LANES), lambda b, h: (0, 0))
    return pl.pallas_call(
        _mlstm_kernel,
        grid=(bsz, H // 2),
        in_specs=[col(CB_B), col(CB_B + H), col(CB_B + 2 * H), col(CB_B + 3 * H),
                  pl.BlockSpec((None, SEQ, LANES), lambda b, h: (b, 0, CB_SMALL)),
                  row, row, row],
        out_specs=pl.BlockSpec((None, SEQ, W), lambda b, h: (b, 0, h)),
        out_shape=jax.ShapeDtypeStruct((bsz, SEQ, H * D_HEAD_REC), F32),
        compiler_params=_params("parallel", "parallel"),
        name="mlstm",
    )(y3, y3, y3, y3, y3, bi_row, bf_row, norm_w)


def _nsa_prep_kernel(yt_ref, ks_ref, kw_ref, k2_ref, v2_ref, wkl_ref, wkh_ref, wvl_ref, wvh_ref,
                     pekl_ref, pekh_ref, pevl_ref, pevh_ref, qn_ref, kn0_ref, kn1_ref, kn2_ref,
                     qn_out, ks_out, kw_out, ck_out, cv_out, vs_out, vw_out):
    G, K, DH = NSA_GROUPS, NSA_HPG, NSA_DH
    lane = lax.broadcasted_iota(jnp.int32, (Q_BLOCK, LANES), 1)
    lo = lane < DH

    def half_ms(x):
        xx = x * x
        s0 = jnp.sum(jnp.where(lo, xx, 0.0), axis=1, keepdims=True)
        s1 = jnp.sum(jnp.where(lo, 0.0, xx), axis=1, keepdims=True)
        return jnp.where(lo, s0, s1) * (1.0 / DH)

    k2 = k2_ref[...]
    v2 = v2_ref[...]
    ckb = (_mm(_bf(k2 + pekl_ref[...]), wkl_ref[...])
           + pltpu.roll(_mm(_bf(k2 + pekh_ref[...]), wkh_ref[...]), N_SEG - 1, axis=0))
    ckn = ckb * lax.rsqrt(half_ms(ckb) + EPS) * kn0_ref[...]
    ck_out[0] = _bf(jnp.where(lo, ckn, 0.0))
    ck_out[1] = _bf(jnp.where(lo, 0.0, ckn))
    cvb = (_mm_nt(wvl_ref[...], _bf(v2 + pevl_ref[...]))
           + pltpu.roll(_mm_nt(wvh_ref[...], _bf(v2 + pevh_ref[...])), N_SEG - 1, axis=1))

    qg = qn_ref[...] * (DH ** -0.5 * LOG2E)
    zq = jnp.zeros((DH, K * Q_BLOCK), F32)
    zv = jnp.zeros((DH, Q_BLOCK), BF16)
    for g in range(G):
        cv_out[g] = _bf(cvb[g * DH:(g + 1) * DH, :])
        for c in range(N_QBLK):
            cs = slice(c * Q_BLOCK, (c + 1) * Q_BLOCK)
            r0 = TR_Q + g * K * DH
            x = jnp.concatenate([yt_ref[r0 + k * DH:r0 + (k + 1) * DH, cs] for k in range(K)], axis=1)
            xn = x * lax.rsqrt(jnp.mean(x * x, axis=0, keepdims=True) + EPS) * qg
            parts = [xn, zq] if g == 0 else [zq, xn]
            qn_out[g, c] = _bf(jnp.concatenate(parts, axis=0))
            vs_out[g, SEL_PAD // Q_BLOCK + c] = _bf(yt_ref[TR_VS + g * DH:TR_VS + (g + 1) * DH, cs])
            vw_out[g, WINDOW // Q_BLOCK + c] = _bf(yt_ref[TR_VW + g * DH:TR_VW + (g + 1) * DH, cs])
        for j in range(SEL_PAD // Q_BLOCK):
            vs_out[g, j] = zv
        for j in range(WINDOW // Q_BLOCK):
            vw_out[g, j] = zv

    def norm_keys(src, dst, gain, pad):
        dst[0:pad, :] = jnp.zeros((pad, LANES), BF16)

        def body(i, carry):
            r0 = pl.multiple_of(i * Q_BLOCK, Q_BLOCK)
            x = src[pl.ds(r0, Q_BLOCK), :]
            dst[pl.ds(pad + r0, Q_BLOCK), :] = _bf(x * lax.rsqrt(half_ms(x) + EPS) * gain)
            return carry
        lax.fori_loop(0, N_QBLK, body, 0)

    norm_keys(ks_ref, ks_out, kn1_ref[...], SEL_PAD)
    norm_keys(kw_ref, kw_out, kn2_ref[...], WINDOW)


def _nsa_prep(y3, yt, k2, v2, consts):
    bsz = y3.shape[0]
    G = NSA_GROUPS
    W4 = NSA_HPG * Q_BLOCK

    def full(a):
        nd = a.ndim
        return pl.BlockSpec(a.shape, lambda b, nd=nd: (0,) * nd)

    def per_b(shape):
        nd = len(shape)
        return pl.BlockSpec((None,) + shape, lambda b, nd=nd: (b,) + (0,) * nd)

    shapes = [((G, N_QBLK, 2 * NSA_DH, W4), BF16),
              ((SEL_PAD + SEQ, LANES), BF16), ((WINDOW + SEQ, LANES), BF16),
              ((G, N_SEG, LANES), BF16), ((G, NSA_DH, N_SEG), BF16),
              ((G, (SEL_PAD + SEQ) // Q_BLOCK, NSA_DH, Q_BLOCK), BF16),
              ((G, (WINDOW + SEQ) // Q_BLOCK, NSA_DH, Q_BLOCK), BF16)]
    return pl.pallas_call(
        _nsa_prep_kernel,
        grid=(bsz,),
        in_specs=[per_b((N_TROWS, SEQ)),
                  pl.BlockSpec((None, SEQ, LANES), lambda b: (b, 0, CB_CKS)),
                  pl.BlockSpec((None, SEQ, LANES), lambda b: (b, 0, CB_CKW)),
                  per_b((N_SEG, CMP_STRIDE * LANES)), per_b((N_SEG, CMP_STRIDE * LANES))]
                 + [full(a) for a in consts],
        out_specs=[per_b(s) for s, _ in shapes],
        out_shape=[jax.ShapeDtypeStruct((bsz,) + s, dt) for s, dt in shapes],
        compiler_params=_params("parallel"),
        name="nsa_prep",
    )(yt, y3, y3, k2, v2, *consts)


def _nsa_kernel(q_ref, ck_ref, cv_ref, ks_ref, vs_ref, kw_ref, vw_ref, bct_ref, bt_ref, bw_ref, b31_ref,
                cg_ref, o_ref, sel_s, far_s, bc_s):
    c = pl.program_id(1)
    G = NSA_GROUPS
    W4 = NSA_HPG * Q_BLOCK
    r0 = pl.multiple_of(c * Q_BLOCK, Q_BLOCK)
    n_w = WINDOW + Q_BLOCK
    far_blocks = FAR_TILE // SEL_BLOCK
    far_tiles = FAR_TILE // Q_BLOCK

    def block_mask(rows):
        m1 = jnp.concatenate([jnp.broadcast_to(r, (SEL_BLOCK, Q_BLOCK)) for r in rows], axis=0)
        return jnp.concatenate([m1] * NSA_HPG, axis=1) > 0.5

    def v_tiles(ref, g, first, n):
        return jnp.concatenate([ref[g, first + j] for j in range(n)], axis=1)

    jb = lax.broadcasted_iota(jnp.int32, (N_SEL, Q_BLOCK), 0)
    tq = c * Q_BLOCK + lax.broadcasted_iota(jnp.int32, (N_SEL, Q_BLOCK), 1)
    cur = tq >> 6
    jj = lax.broadcasted_iota(jnp.int32, (N_SEL, N_SEG), 0)
    nn = lax.broadcasted_iota(jnp.int32, (N_SEL, N_SEG), 1)
    ratio = SEL_BLOCK // CMP_STRIDE
    ov = jnp.where((nn >= ratio * jj - 1) & (nn <= ratio * jj + ratio - 1) & (nn < N_SEG - 1), 1.0, 0.0)
    nrow = lax.broadcasted_iota(jnp.int32, (N_SEG + 16, W4), 0) - 16
    near0 = (Q_BLOCK // CMP_STRIDE) * c - 16
    wrow_ok = lax.broadcasted_iota(jnp.int32, (n_w, W4), 0) >= WINDOW - c * Q_BLOCK

    def front(g):
        q = q_ref[g]
        b31 = b31_ref[g]
        qk_cmp = _mm(ck_ref[g], q)
        qk_near = _mm(ks_ref[pl.ds(r0, 2 * Q_BLOCK), :], q)
        qk_win = _mm(kw_ref[pl.ds(r0, n_w), :], q)

        bc_s[g] = jnp.where(nrow < near0, b31, NEG)
        bc_s[g, pl.ds(pl.multiple_of(near0 + 16, 8), CMP_NEAR), :] = bct_ref[g]
        s = qk_cmp + bc_s[g, 16:, :]
        e = jnp.where(s > 0.1 * NEG, jnp.exp2(s - jnp.max(s, axis=0, keepdims=True)), 0.0)
        p = e / jnp.maximum(jnp.sum(e, axis=0, keepdims=True), 1e-30)
        o_cmp = _mm(cv_ref[g], _bf(p))

        psum = (p[:, 0:Q_BLOCK] + p[:, Q_BLOCK:2 * Q_BLOCK] + p[:, 2 * Q_BLOCK:3 * Q_BLOCK]
                + p[:, 3 * Q_BLOCK:4 * Q_BLOCK])
        imp = _mm(ov, psum, HP)

        s = jnp.where(wrow_ok, qk_win + bw_ref[g], NEG)
        pw = jnp.exp2(s - jnp.max(s, axis=0, keepdims=True))
        o_win = _mm(v_tiles(vw_ref, g, c, n_w // Q_BLOCK), _bf(pw)) / jnp.sum(pw, axis=0, keepdims=True)

        imp = jnp.where((jb == 0) | (jb == cur), FORCE_SCORE, jnp.where(jb <= cur, imp, -1.0))
        rank = jnp.zeros((N_SEL, Q_BLOCK), F32)
        for jp in range(N_SEL):
            rowv = imp[jp:jp + 1, :]
            rank = rank + jnp.where(rowv > imp, 1.0, 0.0)
            if jp < N_SEL - 1:
                rank = rank + jnp.where(jb > jp, jnp.where(rowv == imp, 1.0, 0.0), 0.0)
        sel = jnp.where(rank < SEL_TOPN, 1.0, 0.0)
        sel_s[g, 0:8, :] = jnp.zeros((8, Q_BLOCK), F32)
        sel_s[g, 8:, :] = sel
        far_sel = jnp.concatenate([jnp.where(jb < 2 * (c - 1), sel, 0.0)] * NSA_HPG, axis=1)
        far_s[g] = jnp.where(far_sel > 0.5, b31, NEG)

        near = block_mask([sel_s[g, pl.ds(6 + 2 * c + j, 1), :] for j in range(4)])
        s = jnp.where(near, qk_near + bt_ref[g], NEG)
        m_run = jnp.max(s, axis=0, keepdims=True)
        pe = jnp.exp2(s - m_run)
        l_run = jnp.sum(pe, axis=0, keepdims=True)
        acc = _mm(v_tiles(vs_ref, g, c, 2), _bf(pe))
        return (q, b31, o_cmp, o_win), (m_run, l_run, acc)

    fronts = [front(g) for g in range(G)]

    def far_step(st, carry):
        k0 = pl.multiple_of(SEL_PAD + st * FAR_TILE, Q_BLOCK)
        b0 = pl.multiple_of(st * far_blocks, 8)
        kf = ks_ref[pl.ds(k0, FAR_TILE), :]
        qk = [_mm(kf, fronts[g][0][0]) for g in range(G)]
        out = []
        for g in range(G):
            m_old, l_old, acc_old = carry[g]
            rows = far_s[g, pl.ds(b0, far_blocks), :]
            add = jnp.concatenate([jnp.broadcast_to(rows[j:j + 1, :], (SEL_BLOCK, W4))
                                   for j in range(far_blocks)], axis=0)
            sc = qk[g] + add
            m_new = jnp.maximum(m_old, jnp.max(sc, axis=0, keepdims=True))
            alpha = jnp.exp2(m_old - m_new)
            pf = jnp.exp2(sc - m_new)
            l_new = alpha * l_old + jnp.sum(pf, axis=0, keepdims=True)
            vt = v_tiles(vs_ref, g, 1 + st * far_tiles, far_tiles)
            out.append((m_new, l_new, alpha * acc_old + _mm(vt, _bf(pf))))
        return tuple(out)

    n_far = (c + 2) // far_tiles
    finals = lax.fori_loop(0, n_far, far_step, tuple(f[1] for f in fronts))

    gates = _sigmoid(cg_ref[...])
    outs = []
    for g in range(G):
        _, _, o_cmp, o_win = fronts[g][0]
        _, l_run, acc = finals[g]

        def gate_row(r):
            rows = [gates[g * CG_ROWS + k * 3 + r:g * CG_ROWS + k * 3 + r + 1, :] for k in range(NSA_HPG)]
            return jnp.concatenate(rows, axis=1)

        res = gate_row(0) * o_cmp + gate_row(1) * (acc / l_run) + gate_row(2) * o_win
        outs += [res[:, k * Q_BLOCK:(k + 1) * Q_BLOCK].T for k in range(NSA_HPG)]
    o_ref[...] = jnp.concatenate(outs, axis=1)


def _nsa_attend(qn, ck, cv, ksn, vst, kwn, vwt, tables, yt):
    bsz = qn.shape[0]
    G = NSA_GROUPS
    W4 = NSA_HPG * Q_BLOCK
    bct, bt, bw, b31 = tables

    def per_b(a):
        shape = a.shape[1:]
        return pl.BlockSpec((None,) + shape, lambda b, c, n=len(shape): (b,) + (0,) * n)

    def full(a):
        return pl.BlockSpec(a.shape, lambda b, c, n=a.ndim: (0,) * n)

    return pl.pallas_call(
        _nsa_kernel,
        grid=(bsz, N_QBLK),
        in_specs=[
            pl.BlockSpec((None, G, None, 2 * NSA_DH, W4), lambda b, c: (b, 0, c, 0, 0)),
            per_b(ck), per_b(cv), per_b(ksn), per_b(vst), per_b(kwn), per_b(vwt),
            full(bct), full(bt), full(bw), full(b31),
            pl.BlockSpec((None, G * CG_ROWS, Q_BLOCK), lambda b, c: (b, TR_CG // (G * CG_ROWS), c)),
        ],
        out_specs=pl.BlockSpec((None, Q_BLOCK, NSA_HEADS * NSA_DH), lambda b, c: (b, c, 0)),
        out_shape=jax.ShapeDtypeStruct((bsz, SEQ, NSA_HEADS * NSA_DH), F32),
        scratch_shapes=[pltpu.VMEM((G, 8 + N_SEL, Q_BLOCK), F32), pltpu.VMEM((G, N_SEL, W4), F32),
                        pltpu.VMEM((G, 16 + N_SEG, W4), F32)],
        compiler_params=_params("parallel", "arbitrary"),
        name="nsa_attend",
    )(qn, ck, cv, ksn, vst, kwn, vwt, bct, bt, bw, b31, yt)


def _bucket_starts():
    max_exact = REL_BUCKETS // 2
    d = np.arange(4 * REL_MAX_DIST)
    large = max_exact + (np.log(np.maximum(d, 1) / max_exact) / math.log(REL_MAX_DIST / max_exact)
                         * (REL_BUCKETS - max_exact)).astype(np.int64)
    bucket = np.where(d < max_exact, d, np.minimum(large, REL_BUCKETS - 1))
    assert np.all(np.diff(bucket) >= 0) and bucket[-1] == REL_BUCKETS - 1
    return [int(np.argmax(bucket >= j)) for j in range(REL_BUCKETS)]


BUCKET_STARTS = _bucket_starts()
TABLE_ROWS = 128


def _bias_kernel(rb_ref, bct_ref, bt_ref, bw_ref):
    W4 = NSA_HPG * Q_BLOCK
    rb = rb_ref[...] * LOG2E

    def table(rows, r0, dist_fn, limit):
        r = r0 + lax.broadcasted_iota(jnp.int32, (rows, W4), 0)
        i = lax.broadcasted_iota(jnp.int32, (rows, W4), 1) & (Q_BLOCK - 1)
        dist = dist_fn(r, i)
        acc = jnp.broadcast_to(rb[0:1, :], (rows, W4))
        for j in range(1, REL_BUCKETS):
            acc = jnp.where(dist >= BUCKET_STARTS[j], rb[j:j + 1, :], acc)
        return jnp.where((dist < 0) | (dist >= limit), NEG, acc)

    big = 1 << 30
    bct_ref[...] = table(CMP_NEAR, 0, lambda r, i: i - CMP_STRIDE * r + (16 * CMP_STRIDE - CMP_BLOCK + 1), big)

    def near_rows(k, carry):
        r0 = pl.multiple_of(k * TABLE_ROWS, TABLE_ROWS)
        bt_ref[pl.ds(r0, TABLE_ROWS), :] = table(TABLE_ROWS, r0, lambda r, i: Q_BLOCK + i - r, big)
        return carry

    lax.fori_loop(0, 2 * Q_BLOCK // TABLE_ROWS, near_rows, 0)

    def win_rows(k, carry):
        r0 = pl.multiple_of(k * TABLE_ROWS, TABLE_ROWS)
        bw_ref[pl.ds(r0, TABLE_ROWS), :] = table(TABLE_ROWS, r0, lambda r, i: WINDOW + i - r, WINDOW)
        return carry

    lax.fori_loop(0, (WINDOW + Q_BLOCK) // TABLE_ROWS, win_rows, 0)


def _bias_tables(rel_bias):
    G, K = NSA_GROUPS, NSA_HPG
    W4 = K * Q_BLOCK
    rb = jnp.repeat(rel_bias.reshape(REL_BUCKETS, G, K).transpose(1, 0, 2), Q_BLOCK, axis=2)
    shapes = [(G, CMP_NEAR, W4), (G, 2 * Q_BLOCK, W4), (G, WINDOW + Q_BLOCK, W4)]
    bct, bt, bw = pl.pallas_call(
        _bias_kernel,
        grid=(G,),
        in_specs=[pl.BlockSpec((None, REL_BUCKETS, W4), lambda g: (g, 0, 0))],
        out_specs=[pl.BlockSpec((None,) + s[1:], lambda g: (g, 0, 0)) for s in shapes],
        out_shape=[jax.ShapeDtypeStruct(s, F32) for s in shapes],
        compiler_params=_params("parallel"),
        name="bias_tables",
    )(rb)
    return bct, bt, bw, rb[:, REL_BUCKETS - 1:, :] * LOG2E


def _nsa(y3, yt, q_norm, k_norm, cmp_pe, w_cmp, tables):
    bsz = y3.shape[0]
    G, DH = NSA_GROUPS, NSA_DH

    def segments(cb):
        return y3[:, :, cb * LANES:(cb + 1) * LANES].reshape(bsz, N_SEG, CMP_STRIDE * LANES)

    def both_groups(w):
        return jnp.einsum('lde,gh->lgdhe', w, jnp.eye(G, dtype=w.dtype)).reshape(CMP_STRIDE * LANES, LANES)

    def pe_row(pe):
        return jnp.tile(pe[:, None, :], (1, G, 1)).reshape(1, CMP_STRIDE * LANES)

    wk = w_cmp[0].reshape(2, CMP_STRIDE, DH, DH)
    wv = w_cmp[1].reshape(2, CMP_STRIDE, DH, DH)
    pe = cmp_pe.reshape(2, 2, CMP_STRIDE, DH)
    gain2 = lambda g: jnp.tile(g, G).reshape(1, LANES)
    consts = [_bf(both_groups(wk[0])), _bf(both_groups(wk[1])),
              _bf(both_groups(wv[0]).T), _bf(both_groups(wv[1]).T),
              pe_row(pe[0, 0]), pe_row(pe[0, 1]), pe_row(pe[1, 0]), pe_row(pe[1, 1]),
              q_norm.reshape(DH, 1), gain2(k_norm[0]), gain2(k_norm[1]), gain2(k_norm[2])]
    qn, ksn, kwn, ck, cv, vst, vwt = _nsa_prep(y3, yt, segments(CB_CKC), segments(CB_CVC), consts)
    return _nsa_attend(qn, ck, cv, ksn, vst, kwn, vwt, tables, yt)


def _merge_kernel(h_ref, g_ref, ya_ref, yb_ref, yc_ref, wg_ref, wb_ref, wo_ref, o_ref):
    x = h_ref[...]
    u = _bf(_rms_rows(x, g_ref[...]))
    acc = None
    for n, y_ref in enumerate((ya_ref, yb_ref, yc_ref)):
        t = _sigmoid(_mm(u, wg_ref[:, n * D_MODEL:(n + 1) * D_MODEL])) * _mm(_bf(y_ref[...]), wb_ref[n])
        acc = t if acc is None else acc + t
    o_ref[...] = x + _mm(_bf(acc), wo_ref[...])


def _merge(h2, gain, ya, yb, yc, wgt, wb, wo):
    m, d = h2.shape
    tm = 512
    row = lambda w: pl.BlockSpec((tm, w), lambda i: (i, 0))
    return pl.pallas_call(
        _merge_kernel,
        grid=(m // tm,),
        in_specs=[row(d), pl.BlockSpec((1, d), lambda i: (0, 0)),
                  row(BRANCH_W), row(BRANCH_W), row(BRANCH_W),
                  pl.BlockSpec((d, N_BRANCH * d), lambda i: (0, 0)),
                  pl.BlockSpec((N_BRANCH, BRANCH_W, d), lambda i: (0, 0, 0)),
                  pl.BlockSpec((d, d), lambda i: (0, 0))],
        out_specs=row(d),
        out_shape=jax.ShapeDtypeStruct((m, d), F32),
        compiler_params=_params("parallel"),
        name="merge",
    )(h2, gain, ya, yb, yc, wgt, wb, wo)


def _ffn_kernel(h_ref, g_ref, wg_ref, wu_ref, wo_ref, o_ref, u_ref, acc_ref):
    f = pl.program_id(1)

    @pl.when(f == 0)
    def _():
        x = h_ref[...]
        u_ref[...] = _bf(_rms_rows(x, g_ref[...]))
        acc_ref[...] = x

    u = u_ref[...]
    gt = _mm(u, wg_ref[...])
    up = _mm(u, wu_ref[...])
    acc_ref[...] += _mm(_bf(gt * _sigmoid(gt) * up), wo_ref[...])

    @pl.when(f == pl.num_programs(1) - 1)
    def _():
        o_ref[...] = acc_ref[...]


def _ffn(h2, gain, w_in, w_out):
    m, d = h2.shape
    tm, nf = 512, 2
    tf = D_FF // nf
    return pl.pallas_call(
        _ffn_kernel,
        grid=(m // tm, nf),
        in_specs=[pl.BlockSpec((tm, d), lambda i, f: (i, 0)),
                  pl.BlockSpec((1, d), lambda i, f: (0, 0)),
                  pl.BlockSpec((d, tf), lambda i, f: (0, f)),
                  pl.BlockSpec((d, tf), lambda i, f: (0, f + nf)),
                  pl.BlockSpec((tf, d), lambda i, f: (f, 0))],
        out_specs=pl.BlockSpec((tm, d), lambda i, f: (i, 0)),
        out_shape=jax.ShapeDtypeStruct((m, d), F32),
        scratch_shapes=[pltpu.VMEM((tm, d), BF16), pltpu.VMEM((tm, d), F32)],
        compiler_params=_params("parallel", "arbitrary"),
        name="ffn",
    )(h2, gain, w_in, w_in, w_out)


def _ple_kernel(h_ref, g_ref, p_ref, wg_ref, wp_ref, o_ref):
    x = h_ref[...]
    gate = _sigmoid(_mm(_bf(_rms_rows(x, g_ref[...])), wg_ref[...]))
    o_ref[...] = x + gate * _mm(_bf(p_ref[...]), wp_ref[...])


def _ple(h2, gain, p2, wg, wp):
    m, d = h2.shape
    tm = 512
    return pl.pallas_call(
        _ple_kernel,
        grid=(m // tm,),
        in_specs=[pl.BlockSpec((tm, d), lambda i: (i, 0)),
                  pl.BlockSpec((1, d), lambda i: (0, 0)),
                  pl.BlockSpec((tm, PLE_DIM), lambda i: (i, 0)),
                  pl.BlockSpec((d, d), lambda i: (0, 0)),
                  pl.BlockSpec((PLE_DIM, d), lambda i: (0, 0))],
        out_specs=pl.BlockSpec((tm, d), lambda i: (i, 0)),
        out_shape=jax.ShapeDtypeStruct((m, d), F32),
        compiler_params=_params("parallel"),
        name="ple",
    )(h2, gain, p2, wg, wp)


W_IN_GROUPS = ((0, 0, 16), (16, 8, 16), (32, 16, 10))
W_IN_GATES = (42, 40, 24)
W_IN_MIX_BLOCKS = 44
W_IN_SCALAR_BLOCKS = (16, 32, 42)
W_IN_SCALAR_LANES = (8, 16, 40)


def _regroup_kernel(src_ref, off_ref, a_ref, b_ref, s0_ref, s1_ref, s2_ref, o_ref, *, offsets, n_main):
    j = pl.program_id(0)
    x = jnp.concatenate([a_ref[...], b_ref[...]], axis=1)
    for off in offsets:
        @pl.when((off_ref[j] == off) & (j < n_main))
        def _(off=off):
            o_ref[...] = _bf(x[:, off:off + LANES])

    @pl.when(j == n_main)
    def _():
        lane = lax.broadcasted_iota(jnp.int32, (D_MODEL, LANES), 1)
        l0, l1, l2 = W_IN_SCALAR_LANES
        sc = jnp.where(lane < l0, s0_ref[...], jnp.where(lane < l1, s1_ref[...], s2_ref[...]))
        o_ref[...] = _bf(jnp.where(lane < l2, sc, 0.0))

    @pl.when(j > n_main)
    def _():
        o_ref[...] = jnp.zeros(o_ref.shape, BF16)


def _regroup(w_in, layer, groups, n_out, with_scalars):
    src = np.concatenate([blk + np.arange(n) for blk, _, n in groups])
    off = np.concatenate([np.full(n, o) for _, o, n in groups])
    n_main = len(src)
    pad = n_out - n_main
    src = np.concatenate([src, np.zeros(pad, np.int64)]).astype(np.int32)
    off = np.concatenate([off, np.zeros(pad, np.int64)]).astype(np.int32)
    blk = lambda f: pl.BlockSpec((None, D_MODEL, LANES), f)
    fixed = [blk(lambda j, s, o, b=b: (layer, 0, b)) for b in W_IN_SCALAR_BLOCKS]
    return pl.pallas_call(
        functools.partial(_regroup_kernel, offsets=tuple(sorted({o for _, o, _ in groups})),
                          n_main=n_main if with_scalars else n_out),
        grid_spec=pltpu.PrefetchScalarGridSpec(
            num_scalar_prefetch=2,
            grid=(n_out,),
            in_specs=[blk(lambda j, s, o: (layer, 0, s[j])), blk(lambda j, s, o: (layer, 0, s[j] + 1))] + fixed,
            out_specs=pl.BlockSpec((D_MODEL, LANES), lambda j, s, o: (0, j))),
        out_shape=jax.ShapeDtypeStruct((D_MODEL, n_out * LANES), BF16),
        compiler_params=_params("arbitrary"),
        name="regroup_w_in",
    )(jnp.asarray(src), jnp.asarray(off), w_in, w_in, w_in, w_in, w_in)


def _split_w_mix(w_mix):
    blk = lambda b, n=1: w_mix[:, b * LANES:(b + n) * LANES]
    main = jnp.concatenate([blk(0, 32), blk(36, 3), blk(40), blk(42, 2)], axis=1)
    main = jnp.pad(main, ((0, 0), (0, N_COLS - main.shape[1])))
    cg = blk(42)[:, SM_CG:SM_CG + NSA_HEADS * 3].reshape(D_MODEL, NSA_GROUPS, NSA_HPG * 3)
    cg = jnp.pad(cg, ((0, 0), (0, 0), (0, CG_ROWS - NSA_HPG * 3))).reshape(D_MODEL, NSA_GROUPS * CG_ROWS)
    return main, jnp.concatenate([blk(32, 4), blk(39), blk(41), cg], axis=1).T


def _lane_row(vals, offset):
    return jnp.zeros((1, LANES), F32).at[0, offset:offset + vals.shape[0]].set(vals)


def kernel(x, p, rel_bias, norm_mix, w_in, conv_w, gdn_a_log, gdn_dt_bias, gdn_norm, mlstm_b_i, mlstm_b_f,
           mlstm_norm, nsa_q_norm, nsa_k_norm, nsa_cmp_pe, nsa_w_cmp, w_branch, w_out, norm_ffn, w_ffn_in,
           w_ffn_out, norm_ple, w_ple_gate, w_ple_proj):
    bsz, seq, d = x.shape
    assert seq == SEQ and d == D_MODEL
    depth = w_in.shape[0]
    m = bsz * seq
    tables = _bias_tables(rel_bias)
    h2 = x.reshape(m, d)
    for l in range(depth):
        w_main, w_tr = _split_w_mix(_regroup(w_in, l, W_IN_GROUPS, W_IN_MIX_BLOCKS, True))
        w_gate = _regroup(w_in, l, (W_IN_GATES,), W_IN_GATES[2], False)
        gain = norm_mix[l].reshape(1, d)
        y2, yt = _in_proj(h2, gain, w_main, w_tr, bsz)
        y3 = y2.reshape(bsz, seq, N_COLS)
        ya = _gdn(y3, conv_w[l], _lane_row(gdn_a_log[l], SM_AA), _lane_row(gdn_dt_bias[l], SM_AA),
                  gdn_norm[l].reshape(1, D_HEAD_REC))
        yb = _mlstm(y3, _lane_row(mlstm_b_i[l], SM_BI), _lane_row(mlstm_b_f[l], SM_BF),
                    mlstm_norm[l].reshape(1, D_HEAD_REC))
        yc = _nsa(y3, yt, nsa_q_norm[l], nsa_k_norm[l], nsa_cmp_pe[l], nsa_w_cmp[l], tables)
        h2 = _merge(h2, gain, ya.reshape(m, BRANCH_W), yb.reshape(m, BRANCH_W),
                    yc.reshape(m, BRANCH_W), w_gate, _bf(w_branch[l]), _bf(w_out[l]))
        h2 = _ffn(h2, norm_ffn[l].reshape(1, d), _bf(w_ffn_in[l]), _bf(w_ffn_out[l]))
        h2 = _ple(h2, norm_ple[l].reshape(1, d), p[l].reshape(m, PLE_DIM), _bf(w_ple_gate[l]),
                  _bf(w_ple_proj[l]))
    return h2.reshape(bsz, seq, d)
```
